```python
import math
import jax, jax.numpy as jnp
from jax import lax
import numpy as np

D_MODEL = 2048
BATCH = 2
SEQ = 4096
DEPTH = 4

CHUNK = 64
PE_DIM = 256
EPS = 1e-6
RNN_WIDTH = 2048
RNN_BLOCKS = 16
RNN_BLOCK = 128
CONV_WIDTH = 4
LRU_C = 8.0
ATT_HEADS = 16
HEAD_DIM = 128
KV_HEADS = 4
ATT_WIDTH = 2048
KV_WIDTH = 512
IDX_HEADS = 16
IDX_DIM = 64
TOPK_MAX = 256
Q_BLOCK = 128
N_BUCKETS = 32
MAX_DISTANCE = 128
SG_CHUNK = 128
SG_GROUPS = 16
SG_GROUP = 128
SG_WIDTH = 2048
EVEN_SPLITS = (RNN_WIDTH, RNN_WIDTH, ATT_WIDTH, KV_WIDTH, KV_WIDTH, ATT_WIDTH, IDX_HEADS * IDX_DIM, IDX_DIM, IDX_HEADS)
EVEN_IN = 2 * RNN_WIDTH + 2 * ATT_WIDTH + 2 * KV_WIDTH + IDX_HEADS * IDX_DIM + IDX_DIM + IDX_HEADS
ODD_IN = 3 * SG_WIDTH
N_EVEN = (DEPTH + 1) // 2
N_ODD = DEPTH // 2

kernel_name = 'hybrid_rglru_dsa_sgu_trunk'


def rms_norm(x, gain):
    xf = x.astype(jnp.float32)
    y = xf * lax.rsqrt(jnp.mean(xf * xf, axis=-1, keepdims=True) + EPS)
    return (y * gain.astype(jnp.float32)).astype(x.dtype)


def layer_norm(x, gain, bias):
    xf = x.astype(jnp.float32)
    mu = jnp.mean(xf, axis=-1, keepdims=True)
    xc = xf - mu
    y = xc * lax.rsqrt(jnp.mean(xc * xc, axis=-1, keepdims=True) + EPS)
    return (y * gain.astype(jnp.float32) + bias.astype(jnp.float32)).astype(x.dtype)


def t5_bucket(rel):
    half = N_BUCKETS // 2
    max_exact = half // 2
    ret = jnp.where(rel > 0, half, 0)
    n = jnp.abs(rel)
    nf = jnp.maximum(n, 1).astype(jnp.float32)
    large = max_exact + (jnp.log(nf / max_exact) / math.log(MAX_DISTANCE / max_exact) * (half - max_exact)).astype(jnp.int32)
    large = jnp.minimum(large, half - 1)
    return ret + jnp.where(n < max_exact, n, large)


def causal_dwconv(x, w, b):
    c = x.shape[-1]
    xp = jnp.pad(x, ((0, 0), (CONV_WIDTH - 1, 0), (0, 0)))
    y = lax.conv_general_dilated(xp, w.astype(x.dtype)[:, None, :], window_strides=(1,), padding='VALID',
                                 dimension_numbers=('NWC', 'WIO', 'NWC'), feature_group_count=c)
    return y + b.astype(x.dtype)


def rg_lru(x, w_r, b_r, w_i, b_i, lam):
    bsz, t = x.shape[:2]
    xb = x.reshape(bsz, t, RNN_BLOCKS, RNN_BLOCK)
    r = jax.nn.sigmoid((jnp.einsum('btgi,gij->btgj', xb, w_r) + b_r).astype(jnp.float32)).reshape(bsz, t, RNN_WIDTH)
    i = jax.nn.sigmoid((jnp.einsum('btgi,gij->btgj', xb, w_i) + b_i).astype(jnp.float32)).reshape(bsz, t, RNN_WIDTH)
    log_a = -LRU_C * r * jax.nn.softplus(-lam.astype(jnp.float32))
    a = jnp.exp(log_a)
    mult = jnp.sqrt(jnp.maximum(-jnp.expm1(2.0 * log_a), 0.0))
    bx = mult * i * x.astype(jnp.float32)

    def combine(left, right):
        a1, b1 = left
        a2, b2 = right
        return a1 * a2, a2 * b1 + b2

    _, h = lax.associative_scan(combine, (a, bx), axis=1)
    return h.astype(x.dtype)


def dsa_attention(q, k, v, iq, ik, iw, rel_table, topk):
    bsz, t = q.shape[:2]
    n_blocks = t // Q_BLOCK
    rep = ATT_HEADS // KV_HEADS
    key_pos = jnp.arange(t)
    idx_scale = (IDX_DIM ** -0.5) * (IDX_HEADS ** -0.5)
    att_scale = HEAD_DIM ** -0.5

    def block(bi):
        t0 = bi * Q_BLOCK
        qb = lax.dynamic_slice_in_dim(q, t0, Q_BLOCK, axis=1)
        iqb = lax.dynamic_slice_in_dim(iq, t0, Q_BLOCK, axis=1)
        iwb = lax.dynamic_slice_in_dim(iw, t0, Q_BLOCK, axis=1)
        q_pos = t0 + jnp.arange(Q_BLOCK)
        limit = (q_pos // CHUNK + 1) * CHUNK
        s_h = jax.nn.relu(jnp.einsum('bqhd,bsd->bqhs', iqb, ik).astype(jnp.float32))
        score = jnp.einsum('bqhs,bqh->bqs', s_h, iwb.astype(jnp.float32)) * idx_scale
        admissible = key_pos[None, :] < limit[:, None]
        score = jnp.where(admissible[None], score, -jnp.inf)
        _, sel = lax.top_k(score, topk)
        valid = sel < limit[None, :, None]
        kg = jax.vmap(lambda kb, ib: kb[ib])(k, sel)
        vg = jax.vmap(lambda vb, ib: vb[ib])(v, sel)
        qg = qb.reshape(bsz, Q_BLOCK, KV_HEADS, rep, HEAD_DIM)
        logits = jnp.einsum('bqgrd,bqkgd->bqgrk', qg, kg).astype(jnp.float32) * att_scale
        bias = rel_table[t5_bucket(sel - q_pos[None, :, None])].astype(jnp.float32)
        bias = bias.reshape(bsz, Q_BLOCK, topk, KV_HEADS, rep).transpose(0, 1, 3, 4, 2)
        logits = jnp.where(valid[:, :, None, None, :], logits + bias, -jnp.inf)
        probs = jax.nn.softmax(logits, axis=-1).astype(v.dtype)
        o = jnp.einsum('bqgrk,bqkgd->bqgrd', probs, vg)
        return o.reshape(bsz, Q_BLOCK, ATT_WIDTH)

    out = lax.map(block, jnp.arange(n_blocks))
    return out.transpose(1, 0, 2, 3).reshape(bsz, t, ATT_WIDTH)


def even_mixer(hn, w_in, conv_w, conv_b, w_r, b_r, w_i, b_i, lam, q_gain, k_gain, rel_table, w_out, topk):
    bsz, t, _ = hn.shape
    offsets = [int(o) for o in np.cumsum(EVEN_SPLITS)[:-1]]
    xa, ga, q, k, v, gb, iq, ik, iw = jnp.split(hn @ w_in, offsets, axis=-1)
    ya = rg_lru(causal_dwconv(xa, conv_w, conv_b), w_r, b_r, w_i, b_i, lam) * jax.nn.silu(ga)
    q = rms_norm(q.reshape(bsz, t, ATT_HEADS, HEAD_DIM), q_gain)
    k = rms_norm(k.reshape(bsz, t, KV_HEADS, HEAD_DIM), k_gain)
    v = v.reshape(bsz, t, KV_HEADS, HEAD_DIM)
    iq = iq.reshape(bsz, t, IDX_HEADS, IDX_DIM)
    yb = dsa_attention(q, k, v, iq, ik, iw, rel_table, topk) * jax.nn.silu(gb)
    return jnp.concatenate([ya, yb], axis=-1) @ w_out


def odd_mixer(hn, w_in, ln_g, ln_b, w_s, b_s, w_out):
    bsz, t, _ = hn.shape
    u, v, g = jnp.split(hn @ w_in, 3, axis=-1)
    u = jax.nn.gelu(u)
    v = layer_norm(jax.nn.gelu(v), ln_g, ln_b)
    n_chunks = t // SG_CHUNK
    vc = v.reshape(bsz, n_chunks, SG_CHUNK, SG_GROUPS, SG_GROUP)
    cpos = jnp.arange(SG_CHUNK) // CHUNK
    mask = cpos[:, None] >= cpos[None, :]
    ws = jnp.where(mask[None], w_s, 0.0).astype(v.dtype)
    mixed = jnp.einsum('gts,bcsgd->bctgd', ws, vc) + b_s.T[:, :, None].astype(v.dtype)
    y = u * mixed.reshape(bsz, t, SG_WIDTH) * jax.nn.silu(g)
    return y @ w_out


def per_layer_embedding(h, p_i, w_pe, gate_norm, w_gate):
    e = p_i.astype(h.dtype) @ w_pe
    gate = jax.nn.sigmoid((rms_norm(h, gate_norm) @ w_gate).astype(jnp.float32)).astype(h.dtype)
    return gate * e


def setup_inputs(seed: int = 0) -> dict:
    key = jax.random.key(seed)
    ks = jax.random.split(key, 26)
    f32 = jnp.float32

    def nrm(k, shape, scale):
        return jax.random.normal(k, shape, f32) * scale

    u = jax.random.uniform(ks[11], (N_EVEN, RNN_WIDTH), f32, 0.9, 0.999)
    a = u ** (1.0 / LRU_C)
    return {
        'x': nrm(ks[0], (BATCH, SEQ, D_MODEL), 1.0),
        'p': nrm(ks[1], (DEPTH, BATCH, SEQ, PE_DIM), 1.0),
        'norm_gain': 1.0 + nrm(ks[2], (DEPTH, D_MODEL), 0.02),
        'rel_bias': nrm(ks[3], (N_BUCKETS, ATT_HEADS), 0.5),
        'even_w_in': nrm(ks[4], (N_EVEN, D_MODEL, EVEN_IN), D_MODEL ** -0.5),
        'conv_w': nrm(ks[5], (N_EVEN, CONV_WIDTH, RNN_WIDTH), CONV_WIDTH ** -0.5),
        'conv_b': nrm(ks[6], (N_EVEN, RNN_WIDTH), 0.01),
        'lru_w_r': nrm(ks[7], (N_EVEN, RNN_BLOCKS, RNN_BLOCK, RNN_BLOCK), RNN_BLOCK ** -0.5),
        'lru_b_r': nrm(ks[8], (N_EVEN, RNN_BLOCKS, RNN_BLOCK), 0.01),
        'lru_w_i': nrm(ks[9], (N_EVEN, RNN_BLOCKS, RNN_BLOCK, RNN_BLOCK), RNN_BLOCK ** -0.5),
        'lru_b_i': nrm(ks[10], (N_EVEN, RNN_BLOCKS, RNN_BLOCK), 0.01),
        'lru_lambda': jnp.log(a) - jnp.log1p(-a),
        'q_norm': 1.0 + nrm(ks[12], (N_EVEN, HEAD_DIM), 0.02),
        'k_norm': 1.0 + nrm(ks[13], (N_EVEN, HEAD_DIM), 0.02),
        'even_w_out': nrm(ks[14], (N_EVEN, RNN_WIDTH + ATT_WIDTH, D_MODEL), (RNN_WIDTH + ATT_WIDTH) ** -0.5),
        'odd_w_in': nrm(ks[15], (N_ODD, D_MODEL, ODD_IN), D_MODEL ** -0.5),
        'sg_ln_g': 1.0 + nrm(ks[16], (N_ODD, SG_WIDTH), 0.02),
        'sg_ln_b': nrm(ks[17], (N_ODD, SG_WIDTH), 0.01),
        'sg_w_s': nrm(ks[18], (N_ODD, SG_GROUPS, SG_CHUNK, SG_CHUNK), SG_CHUNK ** -0.5),
        'sg_b_s': 1.0 + nrm(ks[19], (N_ODD, SG_GROUPS, SG_CHUNK), 0.1),
        'odd_w_out': nrm(ks[20], (N_ODD, SG_WIDTH, D_MODEL), SG_WIDTH ** -0.5),
        'pe_w': nrm(ks[21], (DEPTH, PE_DIM, D_MODEL), PE_DIM ** -0.5),
        'pe_gate_norm': 1.0 + nrm(ks[22], (DEPTH, D_MODEL), 0.02),
        'pe_w_gate': nrm(ks[23], (DEPTH, D_MODEL, D_MODEL), D_MODEL ** -0.5),
    }


def reference(x, p, norm_gain, rel_bias, even_w_in, conv_w, conv_b, lru_w_r, lru_b_r, lru_w_i, lru_b_i,
              lru_lambda, q_norm, k_norm, even_w_out, odd_w_in, sg_ln_g, sg_ln_b, sg_w_s, sg_b_s, odd_w_out,
              pe_w, pe_gate_norm, pe_w_gate):
    topk = min(TOPK_MAX, x.shape[1] // 4)
    h = x
    for layer in range(DEPTH):
        hn = rms_norm(h, norm_gain[layer])
        j = layer // 2
        if layer % 2 == 0:
            mix = even_mixer(hn, even_w_in[j], conv_w[j], conv_b[j], lru_w_r[j], lru_b_r[j], lru_w_i[j],
                             lru_b_i[j], lru_lambda[j], q_norm[j], k_norm[j], rel_bias, even_w_out[j], topk)
        else:
            mix = odd_mixer(hn, odd_w_in[j], sg_ln_g[j], sg_ln_b[j], sg_w_s[j], sg_b_s[j], odd_w_out[j])
        h = h + mix
        h = h + per_layer_embedding(h, p[layer], pe_w[layer], pe_gate_norm[layer], pe_w_gate[layer])
    return h
```

```python
import functools
import math

import jax
import jax.numpy as jnp
from jax import lax
from jax.experimental import pallas as pl
from jax.experimental.pallas import tpu as pltpu

f32 = jnp.float32
bf16 = jnp.bfloat16
i32 = jnp.int32

D_MODEL = 2048
CHUNK = 64
PE_DIM = 256
EPS = 1e-6
RNN_WIDTH = 2048
RNN_BLOCK = 128
CONV_WIDTH = 4
LRU_C = 8.0
ATT_HEADS = 16
HEAD_DIM = 128
KV_HEADS = 4
REP = ATT_HEADS // KV_HEADS
ATT_WIDTH = 2048
KV_WIDTH = 512
IDX_HEADS = 16
IDX_DIM = 64
TOPK_MAX = 256
N_BUCKETS = 32
SG_CHUNK = 128
SG_GROUPS = 16
SG_GROUP = 128
SG_WIDTH = 2048

LANES = 128
SUBLANES = 8
QB = 128
KT = 128
VMEM_LIMIT = 52 * 1024 * 1024
NEG_BIG = -1e30

OFF_XA, OFF_GA, OFF_Q, OFF_GB, OFF_K, OFF_V, OFF_IQ = 0, 2048, 4096, 6144, 8192, 8704, 9216
EVEN_MAIN = 10240


def _cparams(sem):
    return pltpu.CompilerParams(dimension_semantics=sem, vmem_limit_bytes=VMEM_LIMIT)


def _proj_kernel(h_ref, g_ref, w_ref, *rest, has_tail):
    if has_tail:
        wt_ref, o_ref, ot_ref, hn_ref = rest
    else:
        o_ref, hn_ref = rest

    @pl.when(pl.program_id(1) == 0)
    def _():
        x = h_ref[...]
        ms = jnp.mean(x * x, axis=-1, keepdims=True)
        hn_ref[...] = (x * lax.rsqrt(ms + EPS) * g_ref[...]).astype(bf16)
        if has_tail:
            ot_ref[...] = jnp.dot(hn_ref[...], wt_ref[...], preferred_element_type=f32)

    o_ref[...] = jnp.dot(hn_ref[...], w_ref[...], preferred_element_type=f32).astype(o_ref.dtype)


def _proj(h, gain, w, w_tail=None, *, tm=1024, tn=512, out_dtype=f32):
    m, d = h.shape
    n = w.shape[1]
    has_tail = w_tail is not None
    in_specs = [
        pl.BlockSpec((tm, d), lambda i, j: (i, 0)),
        pl.BlockSpec((1, d), lambda i, j: (0, 0)),
        pl.BlockSpec((d, tn), lambda i, j: (0, j)),
    ]
    out_shape = [jax.ShapeDtypeStruct((m, n), out_dtype)]
    out_specs = [pl.BlockSpec((tm, tn), lambda i, j: (i, j))]
    args = [h, gain.reshape(1, d), w]
    if has_tail:
        nt = w_tail.shape[1]
        in_specs.append(pl.BlockSpec((d, nt), lambda i, j: (0, 0)))
        out_shape.append(jax.ShapeDtypeStruct((m, nt), f32))
        out_specs.append(pl.BlockSpec((tm, nt), lambda i, j: (i, 0)))
        args.append(w_tail)
    res = pl.pallas_call(
        functools.partial(_proj_kernel, has_tail=has_tail),
        grid=(m // tm, n // tn),
        in_specs=in_specs,
        out_specs=out_specs,
        out_shape=out_shape,
        scratch_shapes=[pltpu.VMEM((tm, d), bf16)],
        compiler_params=_cparams(("parallel", "arbitrary")),
        name="proj",
    )(*args)
    return res if has_tail else res[0]


def _outproj_kernel(*refs, n_y):
    h_ref = refs[0]
    y_refs = refs[1:1 + n_y]
    w_refs = refs[1 + n_y:1 + 2 * n_y]
    o_ref = refs[1 + 2 * n_y]
    acc = h_ref[...]
    for y_ref, w_ref in zip(y_refs, w_refs):
        acc = acc + jnp.dot(y_ref[...], w_ref[...], preferred_element_type=f32)
    o_ref[...] = acc


def _outproj(h, ys, ws, *, tm=1024, tn=512):
    m, d = h.shape
    n_y = len(ys)
    in_specs = [pl.BlockSpec((tm, tn), lambda i, j: (i, j))]
    in_specs += [pl.BlockSpec((tm, y.shape[1]), lambda i, j: (i, 0)) for y in ys]
    in_specs += [pl.BlockSpec((w.shape[0], tn), lambda i, j: (0, j)) for w in ws]
    return pl.pallas_call(
        functools.partial(_outproj_kernel, n_y=n_y),
        grid=(m // tm, d // tn),
        in_specs=in_specs,
        out_specs=pl.BlockSpec((tm, tn), lambda i, j: (i, j)),
        out_shape=jax.ShapeDtypeStruct((m, d), f32),
        compiler_params=_cparams(("parallel", "parallel")),
        name="outproj",
    )(h, *ys, *ws)


def _pe_kernel(hrow_ref, h_ref, gn_ref, wg_ref, p_ref, pew_ref, o_ref, hn_ref, pb_ref):
    @pl.when(pl.program_id(1) == 0)
    def _():
        x = hrow_ref[...]
        ms = jnp.mean(x * x, axis=-1, keepdims=True)
        hn_ref[...] = (x * lax.rsqrt(ms + EPS) * gn_ref[...]).astype(bf16)
        pb_ref[...] = p_ref[...].astype(bf16)

    gate = jax.nn.sigmoid(jnp.dot(hn_ref[...], wg_ref[...], preferred_element_type=f32))
    e = jnp.dot(pb_ref[...], pew_ref[...], preferred_element_type=f32)
    o_ref[...] = h_ref[...] + gate * e


def _pe(h, gn, wg, p, pew, *, tm=1024, tn=512):
    m, d = h.shape
    return pl.pallas_call(
        _pe_kernel,
        grid=(m // tm, d // tn),
        in_specs=[
            pl.BlockSpec((tm, d), lambda i, j: (i, 0)),
            pl.BlockSpec((tm, tn), lambda i, j: (i, j)),
            pl.BlockSpec((1, d), lambda i, j: (0, 0)),
            pl.BlockSpec((d, tn), lambda i, j: (0, j)),
            pl.BlockSpec((tm, PE_DIM), lambda i, j: (i, 0)),
            pl.BlockSpec((PE_DIM, tn), lambda i, j: (0, j)),
        ],
        out_specs=pl.BlockSpec((tm, tn), lambda i, j: (i, j)),
        out_shape=jax.ShapeDtypeStruct((m, d), f32),
        scratch_shapes=[pltpu.VMEM((tm, d), bf16), pltpu.VMEM((tm, PE_DIM), bf16)],
        compiler_params=_cparams(("parallel", "arbitrary")),
        name="pe",
    )(h, h, gn.reshape(1, d), wg, p, pew)


LRU_TT = 512
LRU_CB = 256


def _lru_kernel(x_ref, ga_ref, cw_ref, cb_ref, wr_ref, br_ref, wi_ref, bi_ref, lam_ref, o_ref,
                xext_ref, hcar_ref, a_ref, b_ref):
    tstep = pl.program_id(2)
    pad = SUBLANES

    @pl.when(tstep == 0)
    def _():
        xext_ref[0:pad, :] = jnp.zeros((pad, LRU_CB), f32)
        hcar_ref[...] = jnp.zeros_like(hcar_ref)

    @pl.when(tstep > 0)
    def _():
        xext_ref[0:pad, :] = xext_ref[LRU_TT:LRU_TT + pad, :]

    xext_ref[pad:, :] = x_ref[...]

    xc = cb_ref[...] + cw_ref[CONV_WIDTH - 1:CONV_WIDTH, :] * x_ref[...]
    for j in range(CONV_WIDTH - 1):
        sh = CONV_WIDTH - 1 - j
        xc = xc + cw_ref[j:j + 1, :] * xext_ref[pad - sh:pad - sh + LRU_TT, :]

    xcb = xc.astype(bf16)
    sp = jax.nn.softplus(-lam_ref[...])
    nb = LRU_CB // RNN_BLOCK
    for g in range(nb):
        sl = slice(g * RNN_BLOCK, (g + 1) * RNN_BLOCK)
        xg = xcb[:, sl]
        r = jax.nn.sigmoid(jnp.dot(xg, wr_ref[g], preferred_element_type=f32) + br_ref[:, sl])
        ig = jax.nn.sigmoid(jnp.dot(xg, wi_ref[g], preferred_element_type=f32) + bi_ref[:, sl])
        log_a = (-LRU_C) * r * sp[:, sl]
        a = jnp.exp(log_a)
        mult = jnp.sqrt(jnp.maximum(jnp.tanh(-log_a) * (1.0 + a * a), 0.0))
        a_ref[:, sl] = a
        b_ref[:, sl] = mult * ig * xc[:, sl]

    nchunk = LRU_TT // SUBLANES
    a3 = a_ref[...].reshape(nchunk, SUBLANES, LRU_CB)
    b3 = b_ref[...].reshape(nchunk, SUBLANES, LRU_CB)
    row = lax.broadcasted_iota(i32, a3.shape, 1)
    d = 1
    while d < SUBLANES:
        keep = row >= d
        a_sh = jnp.where(keep, pltpu.roll(a3, d, axis=1), 1.0)
        b_sh = jnp.where(keep, pltpu.roll(b3, d, axis=1), 0.0)
        b3 = a3 * b_sh + b3
        a3 = a3 * a_sh
        d *= 2
    a_ref[...] = a3.reshape(LRU_TT, LRU_CB)
    b_ref[...] = b3.reshape(LRU_TT, LRU_CB)

    def carry(c, hprev):
        r0 = pl.multiple_of(c * SUBLANES, SUBLANES)
        h8 = a_ref[pl.ds(r0, SUBLANES), :] * hprev + b_ref[pl.ds(r0, SUBLANES), :]
        gate = ga_ref[pl.ds(r0, SUBLANES), :]
        o_ref[pl.ds(r0, SUBLANES), :] = (h8 * (gate * jax.nn.sigmoid(gate))).astype(o_ref.dtype)
        return jnp.broadcast_to(h8[SUBLANES - 1:SUBLANES, :], (SUBLANES, LRU_CB))

    hcar_ref[...] = lax.fori_loop(0, nchunk, carry, hcar_ref[...], unroll=8)


def _lru(proj, cw, cb, wr, br, wi, bi, lam, *, batch, seq):
    m = proj.shape[0]
    nt = seq // LRU_TT
    ncb = RNN_WIDTH // LRU_CB
    gpb = LRU_CB // RNN_BLOCK
    ga_off = OFF_GA // LRU_CB
    row = lambda b, c, t: b * nt + t
    return pl.pallas_call(
        _lru_kernel,
        grid=(batch, ncb, nt),
        in_specs=[
            pl.BlockSpec((LRU_TT, LRU_CB), lambda b, c, t: (row(b, c, t), c)),
            pl.BlockSpec((LRU_TT, LRU_CB), lambda b, c, t: (row(b, c, t), ga_off + c)),
            pl.BlockSpec((CONV_WIDTH, LRU_CB), lambda b, c, t: (0, c)),
            pl.BlockSpec((1, LRU_CB), lambda b, c, t: (0, c)),
            pl.BlockSpec((gpb, RNN_BLOCK, RNN_BLOCK), lambda b, c, t: (c, 0, 0)),
            pl.BlockSpec((1, LRU_CB), lambda b, c, t: (0, c)),
            pl.BlockSpec((gpb, RNN_BLOCK, RNN_BLOCK), lambda b, c, t: (c, 0, 0)),
            pl.BlockSpec((1, LRU_CB), lambda b, c, t: (0, c)),
            pl.BlockSpec((1, LRU_CB), lambda b, c, t: (0, c)),
        ],
        out_specs=pl.BlockSpec((LRU_TT, LRU_CB), lambda b, c, t: (row(b, c, t), c)),
        out_shape=jax.ShapeDtypeStruct((m, RNN_WIDTH), bf16),
        scratch_shapes=[
            pltpu.VMEM((LRU_TT + SUBLANES, LRU_CB), f32),
            pltpu.VMEM((SUBLANES, LRU_CB), f32),
            pltpu.VMEM((LRU_TT, LRU_CB), f32),
            pltpu.VMEM((LRU_TT, LRU_CB), f32),
        ],
        compiler_params=_cparams(("parallel", "parallel", "arbitrary")),
        name="lru",
    )(proj, proj, cw, cb.reshape(1, -1), wr, br.reshape(1, -1), wi, bi.reshape(1, -1), lam.reshape(1, -1))


PREP_T = 512


def _head_rms(x, gain):
    ms = jnp.mean(x * x, axis=-1, keepdims=True)
    return x * lax.rsqrt(ms + EPS) * gain


def _prep_kernel(q_ref, k_ref, v_ref, iq_ref, tail_ref, qg_ref, kg_ref,
                 qT_ref, kn_ref, vT_ref, iqT_ref, ik_ref, iwT_ref):
    for h in range(ATT_HEADS):
        sl = slice(h * HEAD_DIM, (h + 1) * HEAD_DIM)
        qT_ref[0, sl, :] = _head_rms(q_ref[:, sl], qg_ref[...]).T.astype(bf16)
    for g in range(KV_HEADS):
        sl = slice(g * HEAD_DIM, (g + 1) * HEAD_DIM)
        kn_ref[:, sl] = _head_rms(k_ref[:, sl], kg_ref[...]).astype(bf16)
        vT_ref[0, sl, :] = v_ref[:, sl].T.astype(bf16)
    for c in range(IDX_HEADS * IDX_DIM // LANES):
        sl = slice(c * LANES, (c + 1) * LANES)
        iqT_ref[0, sl, :] = iq_ref[:, sl].T.astype(bf16)
    tail_t = tail_ref[...].T
    ik_ref[...] = tail_ref[:, 0:IDX_DIM].astype(bf16)
    iwT_ref[0] = tail_t[IDX_DIM:IDX_DIM + IDX_HEADS, :]


def _prep(proj, tail, q_gain, k_gain, *, batch, seq):
    m = proj.shape[0]
    nt = seq // PREP_T
    bt = lambda i: (i // nt, 0, i % nt)
    return pl.pallas_call(
        _prep_kernel,
        grid=(m // PREP_T,),
        in_specs=[
            pl.BlockSpec((PREP_T, ATT_WIDTH), lambda i: (i, OFF_Q // ATT_WIDTH)),
            pl.BlockSpec((PREP_T, KV_WIDTH), lambda i: (i, OFF_K // KV_WIDTH)),
            pl.BlockSpec((PREP_T, KV_WIDTH), lambda i: (i, OFF_V // KV_WIDTH)),
            pl.BlockSpec((PREP_T, IDX_HEADS * IDX_DIM), lambda i: (i, OFF_IQ // (IDX_HEADS * IDX_DIM))),
            pl.BlockSpec((PREP_T, LANES), lambda i: (i, 0)),
            pl.BlockSpec((1, HEAD_DIM), lambda i: (0, 0)),
            pl.BlockSpec((1, HEAD_DIM), lambda i: (0, 0)),
        ],
        out_specs=[
            pl.BlockSpec((1, ATT_WIDTH, PREP_T), bt),
            pl.BlockSpec((PREP_T, KV_WIDTH), lambda i: (i, 0)),
            pl.BlockSpec((1, KV_WIDTH, PREP_T), bt),
            pl.BlockSpec((1, IDX_HEADS * IDX_DIM, PREP_T), bt),
            pl.BlockSpec((PREP_T, IDX_DIM), lambda i: (i, 0)),
            pl.BlockSpec((1, IDX_HEADS, PREP_T), bt),
        ],
        out_shape=[
            jax.ShapeDtypeStruct((batch, ATT_WIDTH, seq), bf16),
            jax.ShapeDtypeStruct((m, KV_WIDTH), bf16),
            jax.ShapeDtypeStruct((batch, KV_WIDTH, seq), bf16),
            jax.ShapeDtypeStruct((batch, IDX_HEADS * IDX_DIM, seq), bf16),
            jax.ShapeDtypeStruct((m, IDX_DIM), bf16),
            jax.ShapeDtypeStruct((batch, IDX_HEADS, seq), f32),
        ],
        compiler_params=_cparams(("parallel",)),
        name="attn_prep",
    )(proj, proj, proj, proj, tail, q_gain.reshape(1, -1), k_gain.reshape(1, -1))


BAND_ROWS = 3 * KT


def _band_kernel(tab_ref, o_ref):
    jj = lax.broadcasted_iota(i32, (BAND_ROWS, QB), 0)
    qi = lax.broadcasted_iota(i32, (BAND_ROWS, QB), 1)
    rel = jnp.where(jj < KT, -2 * KT, jj - 2 * KT - qi)
    n = jnp.abs(rel)
    large = jnp.full(rel.shape, 8, i32)
    for thr in (12, 16, 23, 32, 46, 64, 91):
        large = large + (n >= thr).astype(i32)
    bucket = jnp.where(rel > 0, N_BUCKETS // 2, 0) + jnp.where(n < 8, n, large)

    def per_head(h, carry):
        acc = jnp.zeros((BAND_ROWS, QB), f32)
        for b in range(N_BUCKETS):
            acc = jnp.where(bucket == b, tab_ref[b, h], acc)
        o_ref[h] = acc
        return carry

    lax.fori_loop(0, ATT_HEADS, per_head, 0)


def _band(rel_bias):
    return pl.pallas_call(
        _band_kernel,
        in_specs=[pl.BlockSpec(memory_space=pltpu.SMEM)],
        out_specs=pl.BlockSpec(memory_space=pltpu.VMEM),
        out_shape=jax.ShapeDtypeStruct((ATT_HEADS, BAND_ROWS, QB), f32),
        name="bias_band",
    )(rel_bias)


def _dsa_kernel(qT_ref, k_ref, vT_ref, ik_ref, iqT_ref, iwT_ref, gb_ref, band_ref, o_ref,
                key_ref, madd_ref, *, topk):
    i = pl.program_id(1)
    nk = i + 1
    t0 = i * QB
    qpos = t0 + lax.broadcasted_iota(i32, (1, QB), 1)
    limit = (qpos // CHUNK + 1) * CHUNK
    sub_iota = lax.broadcasted_iota(i32, (KT, QB), 0)
    idx_scale = (IDX_DIM ** -0.5) * (IDX_HEADS ** -0.5)
    att_scale = HEAD_DIM ** -0.5
    int_min = jnp.int32(-2 ** 31)

    def score_tile(j, carry):
        r0 = pl.multiple_of(j * KT, KT)
        ks = ik_ref[0, pl.ds(r0, KT), :]
        acc = jnp.zeros((KT, QB), f32)
        for h in range(IDX_HEADS):
            s = jnp.dot(ks, iqT_ref[0, h * IDX_DIM:(h + 1) * IDX_DIM, :], preferred_element_type=f32)
            acc = acc + jnp.maximum(s, 0.0) * iwT_ref[0, h:h + 1, :]
        score = acc * idx_scale
        score = jnp.where(score == 0.0, 0.0, score)
        score = jnp.where(r0 + sub_iota < limit, score, -jnp.inf)
        bits = pltpu.bitcast(score, i32)
        key_ref[pl.ds(r0, KT), :] = jnp.where(bits >= 0, bits, bits ^ jnp.int32(0x7FFFFFFF))
        return carry

    lax.fori_loop(0, nk, score_tile, 0)

    def count_rows(pred_fn):
        def body(j, cnt):
            r0 = pl.multiple_of(j * KT, KT)
            return cnt + pred_fn(key_ref[pl.ds(r0, KT), :], r0).astype(i32)
        cnt = lax.fori_loop(0, nk, body, jnp.zeros((KT, QB), i32))
        return jnp.sum(cnt, axis=0, keepdims=True)

    n_nonneg = count_rows(lambda kt, r0: kt >= 0)
    prefix = jnp.where(n_nonneg >= topk, jnp.int32(0), int_min)

    def bit_step(b, prefix):
        cand = prefix | lax.shift_left(jnp.int32(1), 30 - b)
        cnt = count_rows(lambda kt, r0: kt >= cand)
        return jnp.where(cnt >= topk, cand, prefix)

    tau = lax.fori_loop(0, 31, bit_step, prefix)

    n_gt = count_rows(lambda kt, r0: kt > tau)
    n_eq = count_rows(lambda kt, r0: kt == tau)
    room = topk - n_gt
    neg_inf_key = jnp.int32(-2 ** 31 + 0x7FFFFF)
    excess = jnp.logical_and(n_eq > room, tau > neg_inf_key)
    n_iter = jnp.where(jnp.max(excess.astype(i32)) > 0, 13, 0)

    def cut_step(b, cut):
        cand = cut | lax.shift_left(jnp.int32(1), 12 - b)
        cnt = count_rows(lambda kt, r0: jnp.logical_and(kt == tau, r0 + sub_iota < cand))
        return jnp.where(cnt <= room, cand, cut)

    cut = lax.fori_loop(0, n_iter, cut_step, jnp.zeros((1, QB), i32))
    cut = jnp.where(excess, cut, jnp.int32(2 ** 30))

    def mask_tile(j, carry):
        r0 = pl.multiple_of(j * KT, KT)
        kt = key_ref[pl.ds(r0, KT), :]
        pos = r0 + sub_iota
        sel = jnp.logical_or(kt > tau, jnp.logical_and(kt == tau, pos < cut))
        sel = jnp.logical_and(sel, pos < limit)
        madd_ref[pl.ds(r0, KT), :] = jnp.where(sel, 0.0, NEG_BIG)
        return carry

    lax.fori_loop(0, nk, mask_tile, 0)

    for g in range(KV_HEADS):
        heads = range(g * REP, (g + 1) * REP)
        qg = jnp.concatenate([qT_ref[0, h * HEAD_DIM:(h + 1) * HEAD_DIM, :] for h in heads], axis=1)

        def att_tile(j, carry, g=g, heads=heads, qg=qg):
            m_old, l_old, acc = carry
            r0 = pl.multiple_of(j * KT, KT)
            kt = k_ref[0, pl.ds(r0, KT), g * HEAD_DIM:(g + 1) * HEAD_DIM]
            lg = jnp.dot(kt, qg, preferred_element_type=f32) * att_scale
            band_row = pl.multiple_of(jnp.clip(j - i + 2, 0, 2) * KT, KT)
            bias = jnp.concatenate([band_ref[h, pl.ds(band_row, KT), :] for h in heads], axis=1)
            madd = madd_ref[pl.ds(r0, KT), :]
            lg = lg + bias + jnp.concatenate([madd] * REP, axis=1)
            m_new = jnp.maximum(m_old, jnp.max(lg, axis=0, keepdims=True))
            alpha = jnp.exp(m_old - m_new)
            p = jnp.exp(lg - m_new)
            l_new = l_old * alpha + jnp.sum(p, axis=0, keepdims=True)
            vt = vT_ref[0, g * HEAD_DIM:(g + 1) * HEAD_DIM, pl.ds(r0, KT)]
            acc = acc * alpha + jnp.dot(vt, p.astype(bf16), preferred_element_type=f32)
            return m_new, l_new, acc

        init = (jnp.full((1, REP * QB), NEG_BIG, f32), jnp.zeros((1, REP * QB), f32),
                jnp.zeros((HEAD_DIM, REP * QB), f32))
        _, l_fin, acc = lax.fori_loop(0, nk, att_tile, init)
        oT = acc / l_fin
        for r, h in enumerate(heads):
            gate = gb_ref[:, h * HEAD_DIM:(h + 1) * HEAD_DIM]
            o = oT[:, r * QB:(r + 1) * QB].T
            o_ref[:, h * HEAD_DIM:(h + 1) * HEAD_DIM] = (o * (gate * jax.nn.sigmoid(gate))).astype(o_ref.dtype)


def _dsa(qT, kn, vT, ik, iqT, iwT, proj, band, *, batch, seq, topk):
    m = proj.shape[0]
    nq = seq // QB
    kn3 = kn.reshape(batch, seq, KV_WIDTH)
    ik3 = ik.reshape(batch, seq, IDX_DIM)
    return pl.pallas_call(
        functools.partial(_dsa_kernel, topk=topk),
        grid=(batch, nq),
        in_specs=[
            pl.BlockSpec((1, ATT_WIDTH, QB), lambda b, i: (b, 0, i)),
            pl.BlockSpec((1, seq, KV_WIDTH), lambda b, i: (b, 0, 0)),
            pl.BlockSpec((1, KV_WIDTH, seq), lambda b, i: (b, 0, 0)),
            pl.BlockSpec((1, seq, IDX_DIM), lambda b, i: (b, 0, 0)),
            pl.BlockSpec((1, IDX_HEADS * IDX_DIM, QB), lambda b, i: (b, 0, i)),
            pl.BlockSpec((1, IDX_HEADS, QB), lambda b, i: (b, 0, i)),
            pl.BlockSpec((QB, ATT_WIDTH), lambda b, i: (b * nq + i, OFF_GB // ATT_WIDTH)),
            pl.BlockSpec((ATT_HEADS, BAND_ROWS, QB), lambda b, i: (0, 0, 0)),
        ],
        out_specs=pl.BlockSpec((QB, ATT_WIDTH), lambda b, i: (b * nq + i, 0)),
        out_shape=jax.ShapeDtypeStruct((m, ATT_WIDTH), bf16),
        scratch_shapes=[pltpu.VMEM((seq, QB), i32), pltpu.VMEM((seq, QB), f32)],
        compiler_params=_cparams(("parallel", "arbitrary")),
        name="dsa",
    )(qT, kn3, vT, ik3, iqT, iwT, proj, band)


def _gelu(x):
    return 0.5 * x * (1.0 + jnp.tanh(math.sqrt(2.0 / math.pi) * (x + 0.044715 * (x * x * x))))


def _sgu_kernel(u_ref, v_ref, g_ref, lng_ref, lnb_ref, ws_ref, bs_ref, o_ref, wsm_ref):
    @pl.when(pl.program_id(0) == 0)
    def _():
        tpos = lax.broadcasted_iota(i32, (SG_CHUNK, SG_CHUNK), 0) // CHUNK
        spos = lax.broadcasted_iota(i32, (SG_CHUNK, SG_CHUNK), 1) // CHUNK
        for g in range(SG_GROUPS):
            wsm_ref[g] = jnp.where(tpos >= spos, ws_ref[g], 0.0).astype(bf16)

    v = _gelu(v_ref[...])
    mu = jnp.mean(v, axis=-1, keepdims=True)
    vc = v - mu
    var = jnp.mean(vc * vc, axis=-1, keepdims=True)
    vn = (vc * lax.rsqrt(var + EPS) * lng_ref[...] + lnb_ref[...]).astype(bf16)
    for g in range(SG_GROUPS):
        sl = slice(g * SG_GROUP, (g + 1) * SG_GROUP)
        mixed = jnp.dot(wsm_ref[g], vn[:, sl], preferred_element_type=f32) + bs_ref[:, g:g + 1]
        gate = g_ref[:, sl]
        o_ref[:, sl] = (_gelu(u_ref[:, sl]) * mixed * (gate * jax.nn.sigmoid(gate))).astype(o_ref.dtype)


def _sgu(proj, ln_g, ln_b, w_s, b_s):
    m = proj.shape[0]
    return pl.pallas_call(
        _sgu_kernel,
        grid=(m // SG_CHUNK,),
        in_specs=[
            pl.BlockSpec((SG_CHUNK, SG_WIDTH), lambda i: (i, 0)),
            pl.BlockSpec((SG_CHUNK, SG_WIDTH), lambda i: (i, 1)),
            pl.BlockSpec((SG_CHUNK, SG_WIDTH), lambda i: (i, 2)),
            pl.BlockSpec((1, SG_WIDTH), lambda i: (0, 0)),
            pl.BlockSpec((1, SG_WIDTH), lambda i: (0, 0)),
            pl.BlockSpec((SG_GROUPS, SG_CHUNK, SG_CHUNK), lambda i: (0, 0, 0)),
            pl.BlockSpec((SG_CHUNK, SG_GROUPS), lambda i: (0, 0)),
        ],
        out_specs=pl.BlockSpec((SG_CHUNK, SG_WIDTH), lambda i: (i, 0)),
        out_shape=jax.ShapeDtypeStruct((m, SG_WIDTH), bf16),
        scratch_shapes=[pltpu.VMEM((SG_GROUPS, SG_CHUNK, SG_CHUNK), bf16)],
        compiler_params=_cparams(("arbitrary",)),
        name="sgu",
    )(proj, proj, proj, ln_g.reshape(1, -1), ln_b.reshape(1, -1), w_s, b_s.T)


def _even_weights(w_in):
    xa, ga, q, k, v, gb, iq, ik, iw = jnp.split(
        w_in, [2048, 4096, 6144, 6656, 7168, 9216, 10240, 10304], axis=1)
    main = jnp.concatenate([xa, ga, q, gb, k, v, iq], axis=1).astype(bf16)
    pad = jnp.zeros((w_in.shape[0], LANES - IDX_DIM - IDX_HEADS), w_in.dtype)
    tail = jnp.concatenate([ik, iw, pad], axis=1).astype(bf16)
    return main, tail


def kernel(x, p, norm_gain, rel_bias, even_w_in, conv_w, conv_b, lru_w_r, lru_b_r, lru_w_i, lru_b_i,
           lru_lambda, q_norm, k_norm, even_w_out, odd_w_in, sg_ln_g, sg_ln_b, sg_w_s, sg_b_s, odd_w_out,
           pe_w, pe_gate_norm, pe_w_gate):
    batch, seq, d = x.shape
    depth = p.shape[0]
    topk = min(TOPK_MAX, seq // 4)
    m = batch * seq
    h = x.reshape(m, d)
    band = _band(rel_bias)
    for layer in range(depth):
        j = layer // 2
        if layer % 2 == 0:
            w_main, w_tail = _even_weights(even_w_in[j])
            proj, tail = _proj(h, norm_gain[layer], w_main, w_tail)
            ya = _lru(proj, conv_w[j], conv_b[j], lru_w_r[j].astype(bf16), lru_b_r[j].reshape(-1),
                      lru_w_i[j].astype(bf16), lru_b_i[j].reshape(-1), lru_lambda[j], batch=batch, seq=seq)
            qT, kn, vT, iqT, ik, iwT = _prep(proj, tail, q_norm[j], k_norm[j], batch=batch, seq=seq)
            yb = _dsa(qT, kn, vT, ik, iqT, iwT, proj, band, batch=batch, seq=seq, topk=topk)
            w_out = even_w_out[j].astype(bf16)
            h = _outproj(h, [ya, yb], [w_out[:RNN_WIDTH], w_out[RNN_WIDTH:]])
        else:
            proj = _proj(h, norm_gain[layer], odd_w_in[j].astype(bf16))
            y = _sgu(proj, sg_ln_g[j], sg_ln_b[j], sg_w_s[j], sg_b_s[j])
            h = _outproj(h, [y], [odd_w_out[j].astype(bf16)])
        h = _pe(h, pe_gate_norm[layer], pe_w_gate[layer].astype(bf16), p[layer].reshape(m, PE_DIM),
                pe_w[layer].astype(bf16))
    return h.reshape(batch, seq, d)
```

```python
import functools
import math

import jax
import jax.numpy as jnp
from jax import lax
from jax.experimental import pallas as pl
from jax.experimental.pallas import tpu as pltpu

f32 = jnp.float32
bf16 = jnp.bfloat16
i32 = jnp.int32

D_MODEL = 2048
CHUNK = 64
PE_DIM = 256
EPS = 1e-6
RNN_WIDTH = 2048
RNN_BLOCK = 128
CONV_WIDTH = 4
LRU_C = 8.0
ATT_HEADS = 16
HEAD_DIM = 128
KV_HEADS = 4
REP = ATT_HEADS // KV_HEADS
ATT_WIDTH = 2048
KV_WIDTH = 512
IDX_HEADS = 16
IDX_DIM = 64
TOPK_MAX = 256
N_BUCKETS = 32
SG_CHUNK = 128
SG_GROUPS = 16
SG_GROUP = 128
SG_WIDTH = 2048

LANES = 128
SUBLANES = 8
QB = 128
KT = 128
KB = 512
TILES_PER_BLOCK = KB // KT
QK_PIECES = 4
SUM_ROWS = 16
LOG2E = math.log2(math.e)
ATT_C2 = HEAD_DIM ** -0.5 * LOG2E
VMEM_LIMIT = 52 * 1024 * 1024
NEG_BIG = -1e30

OFF_XA, OFF_GA, OFF_Q, OFF_GB, OFF_K, OFF_V, OFF_IQ = 0, 2048, 4096, 6144, 8192, 8704, 9216
EVEN_MAIN = 10240


def _cparams(sem):
    return pltpu.CompilerParams(dimension_semantics=sem, vmem_limit_bytes=VMEM_LIMIT)


def _proj_kernel(h_ref, g_ref, w_ref, *rest, has_tail):
    if has_tail:
        wt_ref, o_ref, ot_ref, hn_ref = rest
    else:
        o_ref, hn_ref = rest

    @pl.when(pl.program_id(1) == 0)
    def _():
        x = h_ref[...]
        ms = jnp.mean(x * x, axis=-1, keepdims=True)
        hn_ref[...] = (x * lax.rsqrt(ms + EPS) * g_ref[...]).astype(bf16)
        if has_tail:
            ot_ref[...] = jnp.dot(hn_ref[...], wt_ref[...], preferred_element_type=f32)

    o_ref[...] = jnp.dot(hn_ref[...], w_ref[...], preferred_element_type=f32).astype(o_ref.dtype)


def _proj(h, gain, w, w_tail=None, *, tm=1024, tn=512, out_dtype=f32):
    m, d = h.shape
    n = w.shape[1]
    has_tail = w_tail is not None
    in_specs = [
        pl.BlockSpec((tm, d), lambda i, j: (i, 0)),
        pl.BlockSpec((1, d), lambda i, j: (0, 0)),
        pl.BlockSpec((d, tn), lambda i, j: (0, j)),
    ]
    out_shape = [jax.ShapeDtypeStruct((m, n), out_dtype)]
    out_specs = [pl.BlockSpec((tm, tn), lambda i, j: (i, j))]
    args = [h, gain.reshape(1, d), w]
    if has_tail:
        nt = w_tail.shape[1]
        in_specs.append(pl.BlockSpec((d, nt), lambda i, j: (0, 0)))
        out_shape.append(jax.ShapeDtypeStruct((m, nt), f32))
        out_specs.append(pl.BlockSpec((tm, nt), lambda i, j: (i, 0)))
        args.append(w_tail)
    res = pl.pallas_call(
        functools.partial(_proj_kernel, has_tail=has_tail),
        grid=(m // tm, n // tn),
        in_specs=in_specs,
        out_specs=out_specs,
        out_shape=out_shape,
        scratch_shapes=[pltpu.VMEM((tm, d), bf16)],
        compiler_params=_cparams(("parallel", "arbitrary")),
        name="proj",
    )(*args)
    return res if has_tail else res[0]


def _outproj_kernel(*refs, n_y):
    h_ref = refs[0]
    y_refs = refs[1:1 + n_y]
    w_refs = refs[1 + n_y:1 + 2 * n_y]
    o_ref = refs[1 + 2 * n_y]
    acc = h_ref[...]
    for y_ref, w_ref in zip(y_refs, w_refs):
        acc = acc + jnp.dot(y_ref[...], w_ref[...], preferred_element_type=f32)
    o_ref[...] = acc


def _outproj(h, ys, ws, *, tm=1024, tn=512):
    m, d = h.shape
    n_y = len(ys)
    in_specs = [pl.BlockSpec((tm, tn), lambda i, j: (i, j))]
    in_specs += [pl.BlockSpec((tm, y.shape[1]), lambda i, j: (i, 0)) for y in ys]
    in_specs += [pl.BlockSpec((w.shape[0], tn), lambda i, j: (0, j)) for w in ws]
    return pl.pallas_call(
        functools.partial(_outproj_kernel, n_y=n_y),
        grid=(m // tm, d // tn),
        in_specs=in_specs,
        out_specs=pl.BlockSpec((tm, tn), lambda i, j: (i, j)),
        out_shape=jax.ShapeDtypeStruct((m, d), f32),
        compiler_params=_cparams(("parallel", "parallel")),
        name="outproj",
    )(h, *ys, *ws)


def _pe_kernel(hrow_ref, h_ref, gn_ref, wg_ref, p_ref, pew_ref, o_ref, hn_ref, pb_ref):
    @pl.when(pl.program_id(1) == 0)
    def _():
        x = hrow_ref[...]
        ms = jnp.mean(x * x, axis=-1, keepdims=True)
        hn_ref[...] = (x * lax.rsqrt(ms + EPS) * gn_ref[...]).astype(bf16)
        pb_ref[...] = p_ref[...].astype(bf16)

    gate = jax.nn.sigmoid(jnp.dot(hn_ref[...], wg_ref[...], preferred_element_type=f32))
    e = jnp.dot(pb_ref[...], pew_ref[...], preferred_element_type=f32)
    o_ref[...] = h_ref[...] + gate * e


def _pe(h, gn, wg, p, pew, *, tm=1024, tn=512):
    m, d = h.shape
    return pl.pallas_call(
        _pe_kernel,
        grid=(m // tm, d // tn),
        in_specs=[
            pl.BlockSpec((tm, d), lambda i, j: (i, 0)),
            pl.BlockSpec((tm, tn), lambda i, j: (i, j)),
            pl.BlockSpec((1, d), lambda i, j: (0, 0)),
            pl.BlockSpec((d, tn), lambda i, j: (0, j)),
            pl.BlockSpec((tm, PE_DIM), lambda i, j: (i, 0)),
            pl.BlockSpec((PE_DIM, tn), lambda i, j: (0, j)),
        ],
        out_specs=pl.BlockSpec((tm, tn), lambda i, j: (i, j)),
        out_shape=jax.ShapeDtypeStruct((m, d), f32),
        scratch_shapes=[pltpu.VMEM((tm, d), bf16), pltpu.VMEM((tm, PE_DIM), bf16)],
        compiler_params=_cparams(("parallel", "arbitrary")),
        name="pe",
    )(h, h, gn.reshape(1, d), wg, p, pew)


LRU_TT = 512
LRU_CB = 256


def _lru_kernel(x_ref, ga_ref, cw_ref, cb_ref, wr_ref, br_ref, wi_ref, bi_ref, lam_ref, o_ref,
                xext_ref, hcar_ref, a_ref, b_ref):
    tstep = pl.program_id(2)
    pad = SUBLANES

    @pl.when(tstep == 0)
    def _():
        xext_ref[0:pad, :] = jnp.zeros((pad, LRU_CB), f32)
        hcar_ref[...] = jnp.zeros_like(hcar_ref)

    @pl.when(tstep > 0)
    def _():
        xext_ref[0:pad, :] = xext_ref[LRU_TT:LRU_TT + pad, :]

    xext_ref[pad:, :] = x_ref[...]

    xc = cb_ref[...] + cw_ref[CONV_WIDTH - 1:CONV_WIDTH, :] * x_ref[...]
    for j in range(CONV_WIDTH - 1):
        sh = CONV_WIDTH - 1 - j
        xc = xc + cw_ref[j:j + 1, :] * xext_ref[pad - sh:pad - sh + LRU_TT, :]

    xcb = xc.astype(bf16)
    sp = jax.nn.softplus(-lam_ref[...])
    nb = LRU_CB // RNN_BLOCK
    for g in range(nb):
        sl = slice(g * RNN_BLOCK, (g + 1) * RNN_BLOCK)
        xg = xcb[:, sl]
        r = jax.nn.sigmoid(jnp.dot(xg, wr_ref[g], preferred_element_type=f32) + br_ref[:, sl])
        ig = jax.nn.sigmoid(jnp.dot(xg, wi_ref[g], preferred_element_type=f32) + bi_ref[:, sl])
        log_a = (-LRU_C) * r * sp[:, sl]
        a = jnp.exp(log_a)
        mult = jnp.sqrt(jnp.maximum(jnp.tanh(-log_a) * (1.0 + a * a), 0.0))
        a_ref[:, sl] = a
        b_ref[:, sl] = mult * ig * xc[:, sl]

    nchunk = LRU_TT // SUBLANES
    a3 = a_ref[...].reshape(nchunk, SUBLANES, LRU_CB)
    b3 = b_ref[...].reshape(nchunk, SUBLANES, LRU_CB)
    row = lax.broadcasted_iota(i32, a3.shape, 1)
    d = 1
    while d < SUBLANES:
        keep = row >= d
        a_sh = jnp.where(keep, pltpu.roll(a3, d, axis=1), 1.0)
        b_sh = jnp.where(keep, pltpu.roll(b3, d, axis=1), 0.0)
        b3 = a3 * b_sh + b3
        a3 = a3 * a_sh
        d *= 2
    a_ref[...] = a3.reshape(LRU_TT, LRU_CB)
    b_ref[...] = b3.reshape(LRU_TT, LRU_CB)

    def carry(c, hprev):
        r0 = pl.multiple_of(c * SUBLANES, SUBLANES)
        h8 = a_ref[pl.ds(r0, SUBLANES), :] * hprev + b_ref[pl.ds(r0, SUBLANES), :]
        gate = ga_ref[pl.ds(r0, SUBLANES), :]
        o_ref[pl.ds(r0, SUBLANES), :] = (h8 * (gate * jax.nn.sigmoid(gate))).astype(o_ref.dtype)
        return jnp.broadcast_to(h8[SUBLANES - 1:SUBLANES, :], (SUBLANES, LRU_CB))

    hcar_ref[...] = lax.fori_loop(0, nchunk, carry, hcar_ref[...], unroll=8)


def _lru(proj, cw, cb, wr, br, wi, bi, lam, *, batch, seq):
    m = proj.shape[0]
    nt = seq // LRU_TT
    ncb = RNN_WIDTH // LRU_CB
    gpb = LRU_CB // RNN_BLOCK
    ga_off = OFF_GA // LRU_CB
    row = lambda b, c, t: b * nt + t
    return pl.pallas_call(
        _lru_kernel,
        grid=(batch, ncb, nt),
        in_specs=[
            pl.BlockSpec((LRU_TT, LRU_CB), lambda b, c, t: (row(b, c, t), c)),
            pl.BlockSpec((LRU_TT, LRU_CB), lambda b, c, t: (row(b, c, t), ga_off + c)),
            pl.BlockSpec((CONV_WIDTH, LRU_CB), lambda b, c, t: (0, c)),
            pl.BlockSpec((1, LRU_CB), lambda b, c, t: (0, c)),
            pl.BlockSpec((gpb, RNN_BLOCK, RNN_BLOCK), lambda b, c, t: (c, 0, 0)),
            pl.BlockSpec((1, LRU_CB), lambda b, c, t: (0, c)),
            pl.BlockSpec((gpb, RNN_BLOCK, RNN_BLOCK), lambda b, c, t: (c, 0, 0)),
            pl.BlockSpec((1, LRU_CB), lambda b, c, t: (0, c)),
            pl.BlockSpec((1, LRU_CB), lambda b, c, t: (0, c)),
        ],
        out_specs=pl.BlockSpec((LRU_TT, LRU_CB), lambda b, c, t: (row(b, c, t), c)),
        out_shape=jax.ShapeDtypeStruct((m, RNN_WIDTH), bf16),
        scratch_shapes=[
            pltpu.VMEM((LRU_TT + SUBLANES, LRU_CB), f32),
            pltpu.VMEM((SUBLANES, LRU_CB), f32),
            pltpu.VMEM((LRU_TT, LRU_CB), f32),
            pltpu.VMEM((LRU_TT, LRU_CB), f32),
        ],
        compiler_params=_cparams(("parallel", "parallel", "arbitrary")),
        name="lru",
    )(proj, proj, cw, cb.reshape(1, -1), wr, br.reshape(1, -1), wi, bi.reshape(1, -1), lam.reshape(1, -1))


PREP_T = 512


def _head_rms(x, gain):
    ms = jnp.mean(x * x, axis=-1, keepdims=True)
    return x * lax.rsqrt(ms + EPS) * gain


def _prep_kernel(q_ref, k_ref, v_ref, iq_ref, tail_ref, qg_ref, kg_ref,
                 qT_ref, kn_ref, vT_ref, iqT_ref, ik_ref, iwT_ref):
    for h in range(ATT_HEADS):
        sl = slice(h * HEAD_DIM, (h + 1) * HEAD_DIM)
        qT_ref[0, sl, :] = (_head_rms(q_ref[:, sl], qg_ref[...]) * ATT_C2).T.astype(bf16)
    for g in range(KV_HEADS):
        sl = slice(g * HEAD_DIM, (g + 1) * HEAD_DIM)
        kn_ref[:, sl] = _head_rms(k_ref[:, sl], kg_ref[...]).astype(bf16)
        vT_ref[0, sl, :] = v_ref[:, sl].T.astype(bf16)
    for c in range(IDX_HEADS * IDX_DIM // LANES):
        sl = slice(c * LANES, (c + 1) * LANES)
        iqT_ref[0, sl, :] = iq_ref[:, sl].T.astype(bf16)
    tail_t = tail_ref[...].T
    ik_ref[...] = tail_ref[:, 0:IDX_DIM].astype(bf16)
    iwT_ref[0] = tail_t[IDX_DIM:IDX_DIM + IDX_HEADS, :]


def _prep(proj, tail, q_gain, k_gain, *, batch, seq):
    m = proj.shape[0]
    nt = seq // PREP_T
    bt = lambda i: (i // nt, 0, i % nt)
    return pl.pallas_call(
        _prep_kernel,
        grid=(m // PREP_T,),
        in_specs=[
            pl.BlockSpec((PREP_T, ATT_WIDTH), lambda i: (i, OFF_Q // ATT_WIDTH)),
            pl.BlockSpec((PREP_T, KV_WIDTH), lambda i: (i, OFF_K // KV_WIDTH)),
            pl.BlockSpec((PREP_T, KV_WIDTH), lambda i: (i, OFF_V // KV_WIDTH)),
            pl.BlockSpec((PREP_T, IDX_HEADS * IDX_DIM), lambda i: (i, OFF_IQ // (IDX_HEADS * IDX_DIM))),
            pl.BlockSpec((PREP_T, LANES), lambda i: (i, 0)),
            pl.BlockSpec((1, HEAD_DIM), lambda i: (0, 0)),
            pl.BlockSpec((1, HEAD_DIM), lambda i: (0, 0)),
        ],
        out_specs=[
            pl.BlockSpec((1, ATT_WIDTH, PREP_T), bt),
            pl.BlockSpec((PREP_T, KV_WIDTH), lambda i: (i, 0)),
            pl.BlockSpec((1, KV_WIDTH, PREP_T), bt),
            pl.BlockSpec((1, IDX_HEADS * IDX_DIM, PREP_T), bt),
            pl.BlockSpec((PREP_T, IDX_DIM), lambda i: (i, 0)),
            pl.BlockSpec((1, IDX_HEADS, PREP_T), bt),
        ],
        out_shape=[
            jax.ShapeDtypeStruct((batch, ATT_WIDTH, seq), bf16),
            jax.ShapeDtypeStruct((m, KV_WIDTH), bf16),
            jax.ShapeDtypeStruct((batch, KV_WIDTH, seq), bf16),
            jax.ShapeDtypeStruct((batch, IDX_HEADS * IDX_DIM, seq), bf16),
            jax.ShapeDtypeStruct((m, IDX_DIM), bf16),
            jax.ShapeDtypeStruct((batch, IDX_HEADS, seq), f32),
        ],
        compiler_params=_cparams(("parallel",)),
        name="attn_prep",
    )(proj, proj, proj, proj, tail, q_gain.reshape(1, -1), k_gain.reshape(1, -1))


BAND_ROWS = 3 * KT
FAR_BUCKET = N_BUCKETS // 2 - 1


def _band_kernel(tab_ref, o_ref):
    jj = lax.broadcasted_iota(i32, (BAND_ROWS, QB), 0)
    qi = lax.broadcasted_iota(i32, (BAND_ROWS, QB), 1)
    rel = jnp.where(jj < KT, -2 * KT, jj - 2 * KT - qi)
    n = jnp.abs(rel)
    large = jnp.full(rel.shape, 8, i32)
    for thr in (12, 16, 23, 32, 46, 64, 91):
        large = large + (n >= thr).astype(i32)
    bucket = jnp.where(rel > 0, N_BUCKETS // 2, 0) + jnp.where(n < 8, n, large)

    def per_head(h, carry):
        acc = jnp.zeros((BAND_ROWS, QB), f32)
        for b in range(N_BUCKETS):
            acc = jnp.where(bucket == b, tab_ref[b, h], acc)
        o_ref[h] = (acc - tab_ref[FAR_BUCKET, h]) * LOG2E
        return carry

    lax.fori_loop(0, ATT_HEADS, per_head, 0)


def _band(rel_bias):
    return pl.pallas_call(
        _band_kernel,
        in_specs=[pl.BlockSpec(memory_space=pltpu.SMEM)],
        out_specs=pl.BlockSpec(memory_space=pltpu.VMEM),
        out_shape=jax.ShapeDtypeStruct((ATT_HEADS, BAND_ROWS, QB), f32),
        name="bias_band",
    )(rel_bias)


def _dsa_kernel(qT_ref, k_ref, vT_ref, ik_ref, iqT_ref, iwT_ref, gb_ref, band_ref, o_ref,
                key_ref, madd_ref, *state_refs, topk):
    m_refs = state_refs[:KV_HEADS]
    acc_refs = state_refs[KV_HEADS:]
    i = pl.program_id(1)
    nb = i // TILES_PER_BLOCK + 1
    t0 = i * QB
    qpos = t0 + lax.broadcasted_iota(i32, (1, QB), 1)
    limit = (qpos // CHUNK + 1) * CHUNK
    sub_iota = lax.broadcasted_iota(i32, (KB, QB), 0)
    idx_scale = (IDX_DIM ** -0.5) * (IDX_HEADS ** -0.5)
    int_min = jnp.int32(-2 ** 31)

    def score_block(j, carry):
        r0 = pl.multiple_of(j * KB, KB)
        ks = ik_ref[0, pl.ds(r0, KB), :]
        acc = jnp.zeros((KB, QB), f32)
        for hp in range(IDX_HEADS // 2):
            h0, h1 = 2 * hp, 2 * hp + 1
            w = jnp.concatenate([iqT_ref[0, h0 * IDX_DIM:(h0 + 1) * IDX_DIM, :],
                                 iqT_ref[0, h1 * IDX_DIM:(h1 + 1) * IDX_DIM, :]], axis=1)
            s = jnp.dot(ks, w, preferred_element_type=f32)
            acc = acc + jnp.maximum(s[:, :QB], 0.0) * iwT_ref[0, h0:h0 + 1, :]
            acc = acc + jnp.maximum(s[:, QB:], 0.0) * iwT_ref[0, h1:h1 + 1, :]
        score = acc * idx_scale
        score = jnp.where(score == 0.0, 0.0, score)
        score = jnp.where(r0 + sub_iota < limit, score, -jnp.inf)
        bits = pltpu.bitcast(score, i32)
        key_ref[pl.ds(r0, KB), :] = jnp.where(bits >= 0, bits, bits ^ jnp.int32(0x7FFFFFFF))
        return carry

    lax.fori_loop(0, nb, score_block, 0)

    def count_rows(pred_fn):
        def body(j, cnt8):
            r0 = pl.multiple_of(j * KB, KB)
            hit = pred_fn(key_ref[pl.ds(r0, KB), :], r0).astype(i32)
            return cnt8 + jnp.sum(hit.reshape(KB // SUBLANES, SUBLANES, QB), axis=0)
        cnt8 = lax.fori_loop(0, nb, body, jnp.zeros((SUBLANES, QB), i32))
        return jnp.sum(cnt8, axis=0, keepdims=True)

    n_nonneg = count_rows(lambda kt, r0: kt >= 0)
    nonneg = n_nonneg >= topk
    prefix0 = jnp.where(nonneg, jnp.int32(0), int_min)
    cur0 = jnp.where(nonneg, n_nonneg, nb * KB)
    n_bits = 31
    bits_per_check = 4

    def bits_cond(state):
        b, _, _, more = state
        return jnp.logical_and(b < n_bits, more > 0)

    def bits_body(state):
        b, prefix, cur, _ = state
        for u in range(bits_per_check):
            shift = jnp.maximum(n_bits - 1 - (b + u), 0)
            bit = jnp.where(b + u < n_bits, lax.shift_left(jnp.int32(1), shift), 0)
            cand = prefix | bit
            cnt = count_rows(lambda kt, r0, cand=cand: kt >= cand)
            take = cnt >= topk
            prefix = jnp.where(take, cand, prefix)
            cur = jnp.where(take, cnt, cur)
        more = jnp.max((cur > topk).astype(i32))
        return b + bits_per_check, prefix, cur, more

    more0 = jnp.max((cur0 > topk).astype(i32))
    _, tau, cur, _ = lax.while_loop(bits_cond, bits_body, (jnp.int32(0), prefix0, cur0, more0))

    neg_inf_key = jnp.int32(-2 ** 31 + 0x7FFFFF)
    excess = jnp.logical_and(cur > topk, tau > neg_inf_key)
    any_excess = jnp.max(excess.astype(i32)) > 0
    n_gt = lax.cond(any_excess, lambda: count_rows(lambda kt, r0: kt > tau), lambda: jnp.zeros((1, QB), i32))
    room = topk - n_gt
    n_iter = jnp.where(any_excess, 13, 0)

    def cut_step(b, cut):
        cand = cut | lax.shift_left(jnp.int32(1), 12 - b)
        cnt = count_rows(lambda kt, r0: jnp.logical_and(kt == tau, r0 + sub_iota < cand))
        return jnp.where(cnt <= room, cand, cut)

    cut = lax.fori_loop(0, n_iter, cut_step, jnp.zeros((1, QB), i32))
    cut = jnp.where(excess, cut, jnp.int32(2 ** 30))

    def mask_block(j, carry):
        r0 = pl.multiple_of(j * KB, KB)
        kt = key_ref[pl.ds(r0, KB), :]
        pos = r0 + sub_iota
        sel = jnp.logical_or(kt > tau, jnp.logical_and(kt == tau, pos < cut))
        sel = jnp.logical_and(sel, pos < limit)
        madd_ref[pl.ds(r0, KB), :] = jnp.where(sel, 0.0, NEG_BIG)
        return carry

    lax.fori_loop(0, nb, mask_block, 0)

    for m_ref, acc_ref in zip(m_refs, acc_refs):
        m_ref[...] = jnp.full(m_ref.shape, NEG_BIG, f32)
        acc_ref[...] = jnp.zeros(acc_ref.shape, f32)

    ones_rows = jnp.ones((SUM_ROWS, KB), bf16)

    def att_block(j, carry, near):
        r0 = pl.multiple_of(j * KB, KB)
        madd = madd_ref[pl.ds(r0, KB), :]

        def qk(g):
            qg = jnp.concatenate([qT_ref[0, h * HEAD_DIM:(h + 1) * HEAD_DIM, :]
                                  for h in range(g * REP, (g + 1) * REP)], axis=1)
            piece = KB // QK_PIECES
            parts = []
            for c in range(QK_PIECES):
                kt = k_ref[0, pl.ds(r0 + c * piece, piece), g * HEAD_DIM:(g + 1) * HEAD_DIM]
                parts.append(jnp.dot(kt, qg, preferred_element_type=f32))
            return jnp.concatenate(parts, axis=0)

        ts = {0: qk(0)}
        for g in range(KV_HEADS):
            heads = range(g * REP, (g + 1) * REP)
            if g + 1 < KV_HEADS:
                ts[g + 1] = qk(g + 1)
            t = ts.pop(g)
            if near:
                rows = []
                for u in range(TILES_PER_BLOCK):
                    band_row = pl.multiple_of(jnp.clip(j * TILES_PER_BLOCK + u - i + 2, 0, 2) * KT, KT)
                    mu = madd[u * KT:(u + 1) * KT, :]
                    rows.append(jnp.concatenate(
                        [band_ref[h, pl.ds(band_row, KT), :] + mu for h in heads], axis=1))
                t = t + jnp.concatenate(rows, axis=0)
            else:
                t = t + jnp.concatenate([madd] * REP, axis=1)
            m_old = m_refs[g][...]
            m_new = jnp.maximum(m_old, jnp.max(t, axis=0, keepdims=True))
            alpha = jnp.exp2(m_old - m_new)
            p = jnp.exp2(t - m_new)
            m_refs[g][...] = m_new
            vt = jnp.concatenate([vT_ref[0, g * HEAD_DIM:(g + 1) * HEAD_DIM, pl.ds(r0, KB)], ones_rows], axis=0)
            acc_refs[g][...] = acc_refs[g][...] * alpha + jnp.dot(vt, p.astype(bf16), preferred_element_type=f32)
        return carry

    n_far = jnp.maximum(i - 1, 0) // TILES_PER_BLOCK
    lax.fori_loop(0, n_far, functools.partial(att_block, near=False), 0)
    lax.fori_loop(n_far, nb, functools.partial(att_block, near=True), 0)

    for g in range(KV_HEADS):
        heads = range(g * REP, (g + 1) * REP)
        oT = acc_refs[g][0:HEAD_DIM, :] / acc_refs[g][HEAD_DIM:HEAD_DIM + 1, :]
        for r, h in enumerate(heads):
            gate = gb_ref[:, h * HEAD_DIM:(h + 1) * HEAD_DIM]
            o = oT[:, r * QB:(r + 1) * QB].T
            o_ref[:, h * HEAD_DIM:(h + 1) * HEAD_DIM] = (o * (gate * jax.nn.sigmoid(gate))).astype(o_ref.dtype)


def _dsa(qT, kn, vT, ik, iqT, iwT, proj, band, *, batch, seq, topk):
    m = proj.shape[0]
    nq = seq // QB
    kn3 = kn.reshape(batch, seq, KV_WIDTH)
    ik3 = ik.reshape(batch, seq, IDX_DIM)
    return pl.pallas_call(
        functools.partial(_dsa_kernel, topk=topk),
        grid=(batch, nq),
        in_specs=[
            pl.BlockSpec((1, ATT_WIDTH, QB), lambda b, i: (b, 0, i)),
            pl.BlockSpec((1, seq, KV_WIDTH), lambda b, i: (b, 0, 0)),
            pl.BlockSpec((1, KV_WIDTH, seq), lambda b, i: (b, 0, 0)),
            pl.BlockSpec((1, seq, IDX_DIM), lambda b, i: (b, 0, 0)),
            pl.BlockSpec((1, IDX_HEADS * IDX_DIM, QB), lambda b, i: (b, 0, i)),
            pl.BlockSpec((1, IDX_HEADS, QB), lambda b, i: (b, 0, i)),
            pl.BlockSpec((QB, ATT_WIDTH), lambda b, i: (b * nq + i, OFF_GB // ATT_WIDTH)),
            pl.BlockSpec((ATT_HEADS, BAND_ROWS, QB), lambda b, i: (0, 0, 0)),
        ],
        out_specs=pl.BlockSpec((QB, ATT_WIDTH), lambda b, i: (b * nq + i, 0)),
        out_shape=jax.ShapeDtypeStruct((m, ATT_WIDTH), bf16),
        scratch_shapes=[
            pltpu.VMEM((seq, QB), i32),
            pltpu.VMEM((seq, QB), f32),
        ] + [pltpu.VMEM((1, REP * QB), f32)] * KV_HEADS
          + [pltpu.VMEM((HEAD_DIM + SUM_ROWS, REP * QB), f32)] * KV_HEADS,
        compiler_params=_cparams(("parallel", "arbitrary")),
        name="dsa",
    )(qT, kn3, vT, ik3, iqT, iwT, proj, band)


def _gelu(x):
    return 0.5 * x * (1.0 + jnp.tanh(math.sqrt(2.0 / math.pi) * (x + 0.044715 * (x * x * x))))


def _sgu_kernel(u_ref, v_ref, g_ref, lng_ref, lnb_ref, ws_ref, bs_ref, o_ref, wsm_ref):
    @pl.when(pl.program_id(0) == 0)
    def _():
        tpos = lax.broadcasted_iota(i32, (SG_CHUNK, SG_CHUNK), 0) // CHUNK
        spos = lax.broadcasted_iota(i32, (SG_CHUNK, SG_CHUNK), 1) // CHUNK
        for g in range(SG_GROUPS):
            wsm_ref[g] = jnp.where(tpos >= spos, ws_ref[g], 0.0).astype(bf16)

    v = _gelu(v_ref[...])
    mu = jnp.mean(v, axis=-1, keepdims=True)
    vc = v - mu
    var = jnp.mean(vc * vc, axis=-1, keepdims=True)
    vn = (vc * lax.rsqrt(var + EPS) * lng_ref[...] + lnb_ref[...]).astype(bf16)
    for g in range(SG_GROUPS):
        sl = slice(g * SG_GROUP, (g + 1) * SG_GROUP)
        mixed = jnp.dot(wsm_ref[g], vn[:, sl], preferred_element_type=f32) + bs_ref[:, g:g + 1]
        gate = g_ref[:, sl]
        o_ref[:, sl] = (_gelu(u_ref[:, sl]) * mixed * (gate * jax.nn.sigmoid(gate))).astype(o_ref.dtype)


def _sgu(proj, ln_g, ln_b, w_s, b_s):
    m = proj.shape[0]
    return pl.pallas_call(
        _sgu_kernel,
        grid=(m // SG_CHUNK,),
        in_specs=[
            pl.BlockSpec((SG_CHUNK, SG_WIDTH), lambda i: (i, 0)),
            pl.BlockSpec((SG_CHUNK, SG_WIDTH), lambda i: (i, 1)),
            pl.BlockSpec((SG_CHUNK, SG_WIDTH), lambda i: (i, 2)),
            pl.BlockSpec((1, SG_WIDTH), lambda i: (0, 0)),
            pl.BlockSpec((1, SG_WIDTH), lambda i: (0, 0)),
            pl.BlockSpec((SG_GROUPS, SG_CHUNK, SG_CHUNK), lambda i: (0, 0, 0)),
            pl.BlockSpec((SG_CHUNK, SG_GROUPS), lambda i: (0, 0)),
        ],
        out_specs=pl.BlockSpec((SG_CHUNK, SG_WIDTH), lambda i: (i, 0)),
        out_shape=jax.ShapeDtypeStruct((m, SG_WIDTH), bf16),
        scratch_shapes=[pltpu.VMEM((SG_GROUPS, SG_CHUNK, SG_CHUNK), bf16)],
        compiler_params=_cparams(("arbitrary",)),
        name="sgu",
    )(proj, proj, proj, ln_g.reshape(1, -1), ln_b.reshape(1, -1), w_s, b_s.T)


def _even_weights(w_in):
    xa, ga, q, k, v, gb, iq, ik, iw = jnp.split(
        w_in, [2048, 4096, 6144, 6656, 7168, 9216, 10240, 10304], axis=1)
    main = jnp.concatenate([xa, ga, q, gb, k, v, iq], axis=1).astype(bf16)
    pad = jnp.zeros((w_in.shape[0], LANES - IDX_DIM - IDX_HEADS), w_in.dtype)
    tail = jnp.concatenate([ik, iw, pad], axis=1).astype(bf16)
    return main, tail


def kernel(x, p, norm_gain, rel_bias, even_w_in, conv_w, conv_b, lru_w_r, lru_b_r, lru_w_i, lru_b_i,
           lru_lambda, q_norm, k_norm, even_w_out, odd_w_in, sg_ln_g, sg_ln_b, sg_w_s, sg_b_s, odd_w_out,
           pe_w, pe_gate_norm, pe_w_gate):
    batch, seq, d = x.shape
    depth = p.shape[0]
    topk = min(TOPK_MAX, seq // 4)
    m = batch * seq
    h = x.reshape(m, d)
    band = _band(rel_bias)
    for layer in range(depth):
        j = layer // 2
        if layer % 2 == 0:
            w_main, w_tail = _even_weights(even_w_in[j])
            proj, tail = _proj(h, norm_gain[layer], w_main, w_tail)
            ya = _lru(proj, conv_w[j], conv_b[j], lru_w_r[j].astype(bf16), lru_b_r[j].reshape(-1),
                      lru_w_i[j].astype(bf16), lru_b_i[j].reshape(-1), lru_lambda[j], batch=batch, seq=seq)
            qT, kn, vT, iqT, ik, iwT = _prep(proj, tail, q_norm[j], k_norm[j], batch=batch, seq=seq)
            yb = _dsa(qT, kn, vT, ik, iqT, iwT, proj, band, batch=batch, seq=seq, topk=topk)
            w_out = even_w_out[j].astype(bf16)
            h = _outproj(h, [ya, yb], [w_out[:RNN_WIDTH], w_out[RNN_WIDTH:]])
        else:
            proj = _proj(h, norm_gain[layer], odd_w_in[j].astype(bf16))
            y = _sgu(proj, sg_ln_g[j], sg_ln_b[j], sg_w_s[j], sg_b_s[j])
            h = _outproj(h, [y], [odd_w_out[j].astype(bf16)])
        h = _pe(h, pe_gate_norm[layer], pe_w_gate[layer].astype(bf16), p[layer].reshape(m, PE_DIM),
                pe_w[layer].astype(bf16))
    return h.reshape(batch, seq, d)
```

```python
import functools
import math

import jax
import jax.numpy as jnp
from jax import lax
from jax.experimental import pallas as pl
from jax.experimental.pallas import tpu as pltpu

f32 = jnp.float32
bf16 = jnp.bfloat16
i32 = jnp.int32

D_MODEL = 2048
CHUNK = 64
PE_DIM = 256
EPS = 1e-6
RNN_WIDTH = 2048
RNN_BLOCK = 128
CONV_WIDTH = 4
LRU_C = 8.0
ATT_HEADS = 16
HEAD_DIM = 128
KV_HEADS = 4
REP = ATT_HEADS // KV_HEADS
ATT_WIDTH = 2048
KV_WIDTH = 512
IDX_HEADS = 16
IDX_DIM = 64
TOPK_MAX = 256
N_BUCKETS = 32
SG_CHUNK = 128
SG_GROUPS = 16
SG_GROUP = 128
SG_WIDTH = 2048

LANES = 128
SUBLANES = 8
QB = 128
KT = 128
KB = 512
TILES_PER_BLOCK = KB // KT
QK_PIECES = 4
SUM_ROWS = 16
LOG2E = math.log2(math.e)
ATT_C2 = HEAD_DIM ** -0.5 * LOG2E
VMEM_LIMIT = 52 * 1024 * 1024
NEG_BIG = -1e30
FAST_LOGIT_LIMIT = 60.0

OFF_XA, OFF_GA, OFF_Q, OFF_GB, OFF_K, OFF_V, OFF_IQ = 0, 2048, 4096, 6144, 8192, 8704, 9216
EVEN_MAIN = 10240


def _cparams(sem):
    return pltpu.CompilerParams(dimension_semantics=sem, vmem_limit_bytes=VMEM_LIMIT)


def _sigmoid(x):
    return 0.5 * jnp.tanh(0.5 * x) + 0.5


def _silu(x):
    return x * _sigmoid(x)


def _block_lookup(table):
    def f(j):
        out = jnp.int32(table[-1])
        for idx in range(len(table) - 2, -1, -1):
            out = jnp.where(j == idx, table[idx], out)
        return out
    return f


def _proj_kernel(h_ref, g_ref, w_ref, *rest, has_tail):
    if has_tail:
        wt_ref, o_ref, ot_ref, hn_ref = rest
    else:
        o_ref, hn_ref = rest

    @pl.when(pl.program_id(1) == 0)
    def _():
        x = h_ref[...]
        ms = jnp.mean(x * x, axis=-1, keepdims=True)
        hn_ref[...] = (x * lax.rsqrt(ms + EPS) * g_ref[...]).astype(bf16)
        if has_tail:
            ot_ref[...] = jnp.dot(hn_ref[...], wt_ref[...], preferred_element_type=f32)

    o_ref[...] = jnp.dot(hn_ref[...], w_ref[...].astype(bf16), preferred_element_type=f32).astype(o_ref.dtype)


def _proj(h, gains, layer, ws, widx, col_blocks, w_tail=None, *, tm=1024, tn=512, out_dtype=f32):
    m, d = h.shape
    n = len(col_blocks) * tn
    has_tail = w_tail is not None
    colmap = _block_lookup(col_blocks)
    in_specs = [
        pl.BlockSpec((tm, d), lambda i, j: (i, 0)),
        pl.BlockSpec((None, 1, d), lambda i, j: (layer, 0, 0)),
        pl.BlockSpec((None, d, tn), lambda i, j: (widx, 0, colmap(j))),
    ]
    out_shape = [jax.ShapeDtypeStruct((m, n), out_dtype)]
    out_specs = [pl.BlockSpec((tm, tn), lambda i, j: (i, j))]
    args = [h, gains.reshape(gains.shape[0], 1, d), ws]
    if has_tail:
        nt = w_tail.shape[1]
        in_specs.append(pl.BlockSpec((d, nt), lambda i, j: (0, 0)))
        out_shape.append(jax.ShapeDtypeStruct((m, nt), f32))
        out_specs.append(pl.BlockSpec((tm, nt), lambda i, j: (i, 0)))
        args.append(w_tail)
    res = pl.pallas_call(
        functools.partial(_proj_kernel, has_tail=has_tail),
        grid=(m // tm, n // tn),
        in_specs=in_specs,
        out_specs=out_specs,
        out_shape=out_shape,
        scratch_shapes=[pltpu.VMEM((tm, d), bf16)],
        compiler_params=_cparams(("parallel", "arbitrary")),
        name="proj",
    )(*args)
    return res if has_tail else res[0]


def _outproj_kernel(*refs, n_y):
    h_ref = refs[0]
    y_refs = refs[1:1 + n_y]
    w_refs = refs[1 + n_y:1 + 2 * n_y]
    o_ref = refs[1 + 2 * n_y]
    acc = h_ref[...]
    for y_ref, w_ref in zip(y_refs, w_refs):
        acc = acc + jnp.dot(y_ref[...], w_ref[...].astype(bf16), preferred_element_type=f32)
    o_ref[...] = acc


def _outproj(h, ys, ws, widx, *, tm=1024, tn=512):
    m, d = h.shape
    n_y = len(ys)
    kw = ys[0].shape[1]
    in_specs = [pl.BlockSpec((tm, tn), lambda i, j: (i, j))]
    in_specs += [pl.BlockSpec((tm, kw), lambda i, j: (i, 0)) for _ in ys]
    in_specs += [pl.BlockSpec((None, kw, tn), lambda i, j, c=c: (widx, c, j)) for c in range(n_y)]
    return pl.pallas_call(
        functools.partial(_outproj_kernel, n_y=n_y),
        grid=(m // tm, d // tn),
        in_specs=in_specs,
        out_specs=pl.BlockSpec((tm, tn), lambda i, j: (i, j)),
        out_shape=jax.ShapeDtypeStruct((m, d), f32),
        compiler_params=_cparams(("parallel", "parallel")),
        name="outproj",
    )(h, *ys, *([ws] * n_y))


def _pe_kernel(hrow_ref, h_ref, gn_ref, wg_ref, p_ref, pew_ref, o_ref, hn_ref, pb_ref):
    @pl.when(pl.program_id(1) == 0)
    def _():
        x = hrow_ref[...]
        ms = jnp.mean(x * x, axis=-1, keepdims=True)
        hn_ref[...] = (x * lax.rsqrt(ms + EPS) * gn_ref[...]).astype(bf16)
        pb_ref[...] = p_ref[...].astype(bf16)

    gate = _sigmoid(jnp.dot(hn_ref[...], wg_ref[...].astype(bf16), preferred_element_type=f32))
    e = jnp.dot(pb_ref[...], pew_ref[...].astype(bf16), preferred_element_type=f32)
    o_ref[...] = h_ref[...] + gate * e


def _pe(h, gns, wgs, p, pews, layer, *, tm=1024, tn=512):
    m, d = h.shape
    depth = p.shape[0]
    return pl.pallas_call(
        _pe_kernel,
        grid=(m // tm, d // tn),
        in_specs=[
            pl.BlockSpec((tm, d), lambda i, j: (i, 0)),
            pl.BlockSpec((tm, tn), lambda i, j: (i, j)),
            pl.BlockSpec((None, 1, d), lambda i, j: (layer, 0, 0)),
            pl.BlockSpec((None, d, tn), lambda i, j: (layer, 0, j)),
            pl.BlockSpec((None, tm, PE_DIM), lambda i, j: (layer, i, 0)),
            pl.BlockSpec((None, PE_DIM, tn), lambda i, j: (layer, 0, j)),
        ],
        out_specs=pl.BlockSpec((tm, tn), lambda i, j: (i, j)),
        out_shape=jax.ShapeDtypeStruct((m, d), f32),
        scratch_shapes=[pltpu.VMEM((tm, d), bf16), pltpu.VMEM((tm, PE_DIM), bf16)],
        compiler_params=_cparams(("parallel", "arbitrary")),
        name="pe",
    )(h, h, gns.reshape(depth, 1, d), wgs, p.reshape(depth, m, PE_DIM), pews)


LRU_TT = 512
LRU_CB = 256


def _lru_kernel(x_ref, ga_ref, cw_ref, cb_ref, wr_ref, br_ref, wi_ref, bi_ref, lam_ref, o_ref,
                xext_ref, hcar_ref, a_ref, b_ref):
    tstep = pl.program_id(2)
    pad = SUBLANES

    @pl.when(tstep == 0)
    def _():
        xext_ref[0:pad, :] = jnp.zeros((pad, LRU_CB), f32)
        hcar_ref[...] = jnp.zeros_like(hcar_ref)

    @pl.when(tstep > 0)
    def _():
        xext_ref[0:pad, :] = xext_ref[LRU_TT:LRU_TT + pad, :]

    xext_ref[pad:, :] = x_ref[...]

    xc = cb_ref[...] + cw_ref[CONV_WIDTH - 1:CONV_WIDTH, :] * x_ref[...]
    for j in range(CONV_WIDTH - 1):
        sh = CONV_WIDTH - 1 - j
        xc = xc + cw_ref[j:j + 1, :] * xext_ref[pad - sh:pad - sh + LRU_TT, :]

    xcb = xc.astype(bf16)
    sp = jax.nn.softplus(-lam_ref[...])
    nb = LRU_CB // RNN_BLOCK
    for g in range(nb):
        sl = slice(g * RNN_BLOCK, (g + 1) * RNN_BLOCK)
        xg = xcb[:, sl]
        r = _sigmoid(jnp.dot(xg, wr_ref[g].astype(bf16), preferred_element_type=f32) + br_ref[:, sl])
        ig = _sigmoid(jnp.dot(xg, wi_ref[g].astype(bf16), preferred_element_type=f32) + bi_ref[:, sl])
        log_a = (-LRU_C) * r * sp[:, sl]
        a = jnp.exp(log_a)
        z = jnp.tanh(-log_a) * (1.0 + a * a)
        mult = jnp.where(z > 0.0, z * lax.rsqrt(z), 0.0)
        a_ref[:, sl] = a
        b_ref[:, sl] = mult * ig * xc[:, sl]

    nchunk = LRU_TT // SUBLANES
    a3 = a_ref[...].reshape(nchunk, SUBLANES, LRU_CB)
    b3 = b_ref[...].reshape(nchunk, SUBLANES, LRU_CB)
    row = lax.broadcasted_iota(i32, a3.shape, 1)
    d = 1
    while d < SUBLANES:
        keep = row >= d
        a_sh = jnp.where(keep, pltpu.roll(a3, d, axis=1), 1.0)
        b_sh = jnp.where(keep, pltpu.roll(b3, d, axis=1), 0.0)
        b3 = a3 * b_sh + b3
        a3 = a3 * a_sh
        d *= 2
    a_ref[...] = a3.reshape(LRU_TT, LRU_CB)
    b_ref[...] = b3.reshape(LRU_TT, LRU_CB)

    def carry(c, hprev):
        r0 = pl.multiple_of(c * SUBLANES, SUBLANES)
        h8 = a_ref[pl.ds(r0, SUBLANES), :] * hprev + b_ref[pl.ds(r0, SUBLANES), :]
        gate = ga_ref[pl.ds(r0, SUBLANES), :]
        o_ref[pl.ds(r0, SUBLANES), :] = (h8 * _silu(gate)).astype(o_ref.dtype)
        return jnp.broadcast_to(h8[SUBLANES - 1:SUBLANES, :], (SUBLANES, LRU_CB))

    hcar_ref[...] = lax.fori_loop(0, nchunk, carry, hcar_ref[...], unroll=8)


def _lru(proj, cw, cb, wr, br, wi, bi, lam, lj, *, batch, seq):
    m = proj.shape[0]
    ne = cw.shape[0]
    nt = seq // LRU_TT
    ncb = RNN_WIDTH // LRU_CB
    gpb = LRU_CB // RNN_BLOCK
    ga_off = OFF_GA // LRU_CB
    row = lambda b, c, t: b * nt + t
    vec = pl.BlockSpec((None, 1, LRU_CB), lambda b, c, t: (lj, 0, c))
    gate_w = pl.BlockSpec((None, gpb, RNN_BLOCK, RNN_BLOCK), lambda b, c, t: (lj, c, 0, 0))
    as_vec = lambda a: a.reshape(ne, 1, RNN_WIDTH)
    return pl.pallas_call(
        _lru_kernel,
        grid=(batch, ncb, nt),
        in_specs=[
            pl.BlockSpec((LRU_TT, LRU_CB), lambda b, c, t: (row(b, c, t), c)),
            pl.BlockSpec((LRU_TT, LRU_CB), lambda b, c, t: (row(b, c, t), ga_off + c)),
            pl.BlockSpec((None, CONV_WIDTH, LRU_CB), lambda b, c, t: (lj, 0, c)),
            vec, gate_w, vec, gate_w, vec, vec,
        ],
        out_specs=pl.BlockSpec((LRU_TT, LRU_CB), lambda b, c, t: (row(b, c, t), c)),
        out_shape=jax.ShapeDtypeStruct((m, RNN_WIDTH), bf16),
        scratch_shapes=[
            pltpu.VMEM((LRU_TT + SUBLANES, LRU_CB), f32),
            pltpu.VMEM((SUBLANES, LRU_CB), f32),
            pltpu.VMEM((LRU_TT, LRU_CB), f32),
            pltpu.VMEM((LRU_TT, LRU_CB), f32),
        ],
        compiler_params=_cparams(("parallel", "parallel", "arbitrary")),
        name="lru",
    )(proj, proj, cw, as_vec(cb), wr, as_vec(br), wi, as_vec(bi), as_vec(lam))


PREP_T = 512


def _head_rms(x, gain):
    ms = jnp.mean(x * x, axis=-1, keepdims=True)
    return x * lax.rsqrt(ms + EPS) * gain


def _prep_kernel(q_ref, k_ref, v_ref, iq_ref, tail_ref, qg_ref, kg_ref,
                 qT_ref, kn_ref, vT_ref, iqT_ref, ik_ref, iwT_ref):
    for h in range(ATT_HEADS):
        sl = slice(h * HEAD_DIM, (h + 1) * HEAD_DIM)
        qT_ref[0, sl, :] = (_head_rms(q_ref[:, sl], qg_ref[...]) * ATT_C2).T.astype(bf16)
    for g in range(KV_HEADS):
        sl = slice(g * HEAD_DIM, (g + 1) * HEAD_DIM)
        kn_ref[:, sl] = _head_rms(k_ref[:, sl], kg_ref[...]).astype(bf16)
        vT_ref[0, sl, :] = v_ref[:, sl].T.astype(bf16)
    for c in range(IDX_HEADS * IDX_DIM // LANES):
        sl = slice(c * LANES, (c + 1) * LANES)
        iqT_ref[0, sl, :] = iq_ref[:, sl].T.astype(bf16)
    tail_t = tail_ref[...].T
    ik_ref[...] = tail_ref[:, 0:IDX_DIM].astype(bf16)
    iwT_ref[0] = tail_t[IDX_DIM:IDX_DIM + IDX_HEADS, :]


def _prep(proj, tail, q_gains, k_gains, lj, *, batch, seq):
    m = proj.shape[0]
    ne = q_gains.shape[0]
    nt = seq // PREP_T
    bt = lambda i: (i // nt, 0, i % nt)
    return pl.pallas_call(
        _prep_kernel,
        grid=(m // PREP_T,),
        in_specs=[
            pl.BlockSpec((PREP_T, ATT_WIDTH), lambda i: (i, OFF_Q // ATT_WIDTH)),
            pl.BlockSpec((PREP_T, KV_WIDTH), lambda i: (i, OFF_K // KV_WIDTH)),
            pl.BlockSpec((PREP_T, KV_WIDTH), lambda i: (i, OFF_V // KV_WIDTH)),
            pl.BlockSpec((PREP_T, IDX_HEADS * IDX_DIM), lambda i: (i, OFF_IQ // (IDX_HEADS * IDX_DIM))),
            pl.BlockSpec((PREP_T, LANES), lambda i: (i, 0)),
            pl.BlockSpec((None, 1, HEAD_DIM), lambda i: (lj, 0, 0)),
            pl.BlockSpec((None, 1, HEAD_DIM), lambda i: (lj, 0, 0)),
        ],
        out_specs=[
            pl.BlockSpec((1, ATT_WIDTH, PREP_T), bt),
            pl.BlockSpec((PREP_T, KV_WIDTH), lambda i: (i, 0)),
            pl.BlockSpec((1, KV_WIDTH, PREP_T), bt),
            pl.BlockSpec((1, IDX_HEADS * IDX_DIM, PREP_T), bt),
            pl.BlockSpec((PREP_T, IDX_DIM), lambda i: (i, 0)),
            pl.BlockSpec((1, IDX_HEADS, PREP_T), bt),
        ],
        out_shape=[
            jax.ShapeDtypeStruct((batch, ATT_WIDTH, seq), bf16),
            jax.ShapeDtypeStruct((m, KV_WIDTH), bf16),
            jax.ShapeDtypeStruct((batch, KV_WIDTH, seq), bf16),
            jax.ShapeDtypeStruct((batch, IDX_HEADS * IDX_DIM, seq), bf16),
            jax.ShapeDtypeStruct((m, IDX_DIM), bf16),
            jax.ShapeDtypeStruct((batch, IDX_HEADS, seq), f32),
        ],
        compiler_params=_cparams(("parallel",)),
        name="attn_prep",
    )(proj, proj, proj, proj, tail, q_gains.reshape(ne, 1, HEAD_DIM), k_gains.reshape(ne, 1, HEAD_DIM))


BAND_ROWS = 3 * KT
FAR_BUCKET = N_BUCKETS // 2 - 1


def _band_kernel(tab_ref, o_ref):
    jj = lax.broadcasted_iota(i32, (BAND_ROWS, QB), 0)
    qi = lax.broadcasted_iota(i32, (BAND_ROWS, QB), 1)
    rel = jnp.where(jj < KT, -2 * KT, jj - 2 * KT - qi)
    n = jnp.abs(rel)
    large = jnp.full(rel.shape, 8, i32)
    for thr in (12, 16, 23, 32, 46, 64, 91):
        large = large + (n >= thr).astype(i32)
    bucket = jnp.where(rel > 0, N_BUCKETS // 2, 0) + jnp.where(n < 8, n, large)

    def per_head(h, carry):
        acc = jnp.zeros((BAND_ROWS, QB), f32)
        for b in range(N_BUCKETS):
            acc = jnp.where(bucket == b, tab_ref[b, h], acc)
        o_ref[h] = (acc - tab_ref[FAR_BUCKET, h]) * LOG2E
        return carry

    lax.fori_loop(0, ATT_HEADS, per_head, 0)


def _band(rel_bias):
    return pl.pallas_call(
        _band_kernel,
        in_specs=[pl.BlockSpec(memory_space=pltpu.SMEM)],
        out_specs=pl.BlockSpec(memory_space=pltpu.VMEM),
        out_shape=jax.ShapeDtypeStruct((ATT_HEADS, BAND_ROWS, QB), f32),
        name="bias_band",
    )(rel_bias)


def _dsa_kernel(qT_ref, k_ref, vT_ref, ik_ref, iqT_ref, iwT_ref, gb_ref, band_ref, o_ref,
                key_ref, madd_ref, eye_ref, bound_ref, *state_refs, topk):
    m_refs = state_refs[:KV_HEADS]
    acc_refs = state_refs[KV_HEADS:]
    i = pl.program_id(1)
    nb = i // TILES_PER_BLOCK + 1
    t0 = i * QB
    qpos = t0 + lax.broadcasted_iota(i32, (1, QB), 1)
    limit = (qpos // CHUNK + 1) * CHUNK
    sub_iota = lax.broadcasted_iota(i32, (KB, QB), 0)
    idx_scale = (IDX_DIM ** -0.5) * (IDX_HEADS ** -0.5)
    int_min = jnp.int32(-2 ** 31)

    def score_block(j, carry):
        r0 = pl.multiple_of(j * KB, KB)
        ks = ik_ref[0, pl.ds(r0, KB), :]
        acc = jnp.zeros((KB, QB), f32)
        for hp in range(IDX_HEADS // 2):
            h0, h1 = 2 * hp, 2 * hp + 1
            w = jnp.concatenate([iqT_ref[0, h0 * IDX_DIM:(h0 + 1) * IDX_DIM, :],
                                 iqT_ref[0, h1 * IDX_DIM:(h1 + 1) * IDX_DIM, :]], axis=1)
            s = jnp.dot(ks, w, preferred_element_type=f32)
            acc = acc + jnp.maximum(s[:, :QB], 0.0) * iwT_ref[0, h0:h0 + 1, :]
            acc = acc + jnp.maximum(s[:, QB:], 0.0) * iwT_ref[0, h1:h1 + 1, :]
        score = acc * idx_scale
        score = jnp.where(score == 0.0, 0.0, score)
        score = jnp.where(r0 + sub_iota < limit, score, -jnp.inf)
        bits = pltpu.bitcast(score, i32)
        key_ref[pl.ds(r0, KB), :] = jnp.where(bits >= 0, bits, bits ^ jnp.int32(0x7FFFFFFF))
        return carry

    lax.fori_loop(0, nb, score_block, 0)

    def count_rows(pred_fn):
        def body(j, cnt8):
            r0 = pl.multiple_of(j * KB, KB)
            hit = pred_fn(key_ref[pl.ds(r0, KB), :], r0).astype(i32)
            return cnt8 + jnp.sum(hit.reshape(KB // SUBLANES, SUBLANES, QB), axis=0)
        cnt8 = lax.fori_loop(0, nb, body, jnp.zeros((SUBLANES, QB), i32))
        return jnp.sum(cnt8, axis=0, keepdims=True)

    n_nonneg = count_rows(lambda kt, r0: kt >= 0)
    nonneg = n_nonneg >= topk
    prefix0 = jnp.where(nonneg, jnp.int32(0), int_min)
    cur0 = jnp.where(nonneg, n_nonneg, nb * KB)
    n_bits = 31
    bits_per_check = 4

    def bits_cond(state):
        b, _, _, more = state
        return jnp.logical_and(b < n_bits, more > 0)

    def bits_body(state):
        b, prefix, cur, _ = state
        for u in range(bits_per_check):
            shift = jnp.maximum(n_bits - 1 - (b + u), 0)
            bit = jnp.where(b + u < n_bits, lax.shift_left(jnp.int32(1), shift), 0)
            cand = prefix | bit
            cnt = count_rows(lambda kt, r0, cand=cand: kt >= cand)
            take = cnt >= topk
            prefix = jnp.where(take, cand, prefix)
            cur = jnp.where(take, cnt, cur)
        more = jnp.max((cur > topk).astype(i32))
        return b + bits_per_check, prefix, cur, more

    more0 = jnp.max((cur0 > topk).astype(i32))
    _, tau, cur, _ = lax.while_loop(bits_cond, bits_body, (jnp.int32(0), prefix0, cur0, more0))

    neg_inf_key = jnp.int32(-2 ** 31 + 0x7FFFFF)
    excess = jnp.logical_and(cur > topk, tau > neg_inf_key)
    any_excess = jnp.max(excess.astype(i32)) > 0
    n_gt = lax.cond(any_excess, lambda: count_rows(lambda kt, r0: kt > tau), lambda: jnp.zeros((1, QB), i32))
    room = topk - n_gt
    n_iter = jnp.where(any_excess, 13, 0)

    def cut_step(b, cut):
        cand = cut | lax.shift_left(jnp.int32(1), 12 - b)
        cnt = count_rows(lambda kt, r0: jnp.logical_and(kt == tau, r0 + sub_iota < cand))
        return jnp.where(cnt <= room, cand, cut)

    cut = lax.fori_loop(0, n_iter, cut_step, jnp.zeros((1, QB), i32))
    cut = jnp.where(excess, cut, jnp.int32(2 ** 30))

    def mask_block(j, carry):
        r0 = pl.multiple_of(j * KB, KB)
        kt = key_ref[pl.ds(r0, KB), :]
        pos = r0 + sub_iota
        sel = jnp.logical_or(kt > tau, jnp.logical_and(kt == tau, pos < cut))
        sel = jnp.logical_and(sel, pos < limit)
        madd_ref[pl.ds(r0, KB), :] = jnp.where(sel, 0.0, NEG_BIG).astype(bf16)
        return carry

    lax.fori_loop(0, nb, mask_block, 0)

    @pl.when(i == 0)
    def _():
        rowi = lax.broadcasted_iota(i32, (QB, REP * QB), 0)
        coli = lax.broadcasted_iota(i32, (QB, REP * QB), 1)
        eye_ref[...] = ((coli & (QB - 1)) == rowi).astype(bf16)
        for g in range(KV_HEADS):
            def kmax_body(j, mx, g=g):
                r0 = pl.multiple_of(j * KB, KB)
                kf = k_ref[0, pl.ds(r0, KB), g * HEAD_DIM:(g + 1) * HEAD_DIM].astype(f32)
                n2 = jnp.sum(kf * kf, axis=-1, keepdims=True)
                return jnp.maximum(mx, jnp.max(n2, axis=0, keepdims=True))
            k2 = lax.fori_loop(0, k_ref.shape[1] // KB, kmax_body, jnp.zeros((1, 1), f32))
            bound_ref[g:g + 1, :] = jnp.broadcast_to(k2, (1, QB))
        bmax = jnp.max(jnp.abs(band_ref[...]).reshape(ATT_HEADS * BAND_ROWS, QB), axis=0, keepdims=True)
        room = FAST_LOGIT_LIMIT - jnp.max(bmax, axis=1, keepdims=True)
        bound_ref[KV_HEADS:KV_HEADS + 1, :] = jnp.broadcast_to(jnp.where(room > 0.0, room * room, -1.0), (1, QB))

    over = jnp.zeros((1, QB), i32)
    for g in range(KV_HEADS):
        q2 = jnp.zeros((1, QB), f32)
        for h in range(g * REP, (g + 1) * REP):
            qh = qT_ref[0, h * HEAD_DIM:(h + 1) * HEAD_DIM, :].astype(f32)
            q2 = jnp.maximum(q2, jnp.sum(qh * qh, axis=0, keepdims=True))
        over = over | (q2 * bound_ref[g:g + 1, :] * 1.002 > bound_ref[KV_HEADS:KV_HEADS + 1, :]).astype(i32)
    needs_shift = jnp.max(over) > 0

    for m_ref, acc_ref in zip(m_refs, acc_refs):
        m_ref[...] = jnp.full(m_ref.shape, NEG_BIG, f32)
        acc_ref[...] = jnp.zeros(acc_ref.shape, f32)

    ones_rows = jnp.ones((SUM_ROWS, KB), bf16)

    def att_block(j, carry, near, online):
        r0 = pl.multiple_of(j * KB, KB)

        def qk(g):
            qg = jnp.concatenate([qT_ref[0, h * HEAD_DIM:(h + 1) * HEAD_DIM, :]
                                  for h in range(g * REP, (g + 1) * REP)], axis=1)
            rhs = jnp.concatenate([qg, eye_ref[...]], axis=0)
            piece = KB // QK_PIECES
            parts = []
            for c in range(QK_PIECES):
                rows = pl.ds(r0 + c * piece, piece)
                lhs = jnp.concatenate([k_ref[0, rows, g * HEAD_DIM:(g + 1) * HEAD_DIM], madd_ref[rows, :]],
                                      axis=1)
                parts.append(jnp.dot(lhs, rhs, preferred_element_type=f32))
            return jnp.concatenate(parts, axis=0)

        ts = {0: qk(0)}
        for g in range(KV_HEADS):
            heads = range(g * REP, (g + 1) * REP)
            if g + 1 < KV_HEADS:
                ts[g + 1] = qk(g + 1)
            t = ts.pop(g)
            if near:
                rows = []
                for u in range(TILES_PER_BLOCK):
                    band_row = pl.multiple_of(jnp.clip(j * TILES_PER_BLOCK + u - i + 2, 0, 2) * KT, KT)
                    rows.append(jnp.concatenate([band_ref[h, pl.ds(band_row, KT), :] for h in heads], axis=1))
                t = t + jnp.concatenate(rows, axis=0)
            vt = jnp.concatenate([vT_ref[0, g * HEAD_DIM:(g + 1) * HEAD_DIM, pl.ds(r0, KB)], ones_rows], axis=0)
            if online:
                m_old = m_refs[g][...]
                m_new = jnp.maximum(m_old, jnp.max(t, axis=0, keepdims=True))
                alpha = jnp.exp2(m_old - m_new)
                p = jnp.exp2(t - m_new)
                m_refs[g][...] = m_new
                acc_refs[g][...] = (acc_refs[g][...] * alpha
                                    + jnp.dot(vt, p.astype(bf16), preferred_element_type=f32))
            else:
                p = jnp.exp2(t)
                acc_refs[g][...] = acc_refs[g][...] + jnp.dot(vt, p.astype(bf16), preferred_element_type=f32)
        return carry

    n_far = jnp.maximum(i - 1, 0) // TILES_PER_BLOCK

    def run_blocks(online):
        lax.fori_loop(0, n_far, functools.partial(att_block, near=False, online=online), 0)
        lax.fori_loop(n_far, nb, functools.partial(att_block, near=True, online=online), 0)

    lax.cond(needs_shift, functools.partial(run_blocks, True), functools.partial(run_blocks, False))

    for g in range(KV_HEADS):
        heads = range(g * REP, (g + 1) * REP)
        oT = acc_refs[g][0:HEAD_DIM, :] / acc_refs[g][HEAD_DIM:HEAD_DIM + 1, :]
        for r, h in enumerate(heads):
            gate = gb_ref[:, h * HEAD_DIM:(h + 1) * HEAD_DIM]
            o = oT[:, r * QB:(r + 1) * QB].T
            o_ref[:, h * HEAD_DIM:(h + 1) * HEAD_DIM] = (o * _silu(gate)).astype(o_ref.dtype)


def _dsa(qT, kn, vT, ik, iqT, iwT, proj, band, *, batch, seq, topk):
    m = proj.shape[0]
    nq = seq // QB
    kn3 = kn.reshape(batch, seq, KV_WIDTH)
    ik3 = ik.reshape(batch, seq, IDX_DIM)
    return pl.pallas_call(
        functools.partial(_dsa_kernel, topk=topk),
        grid=(batch, nq),
        in_specs=[
            pl.BlockSpec((1, ATT_WIDTH, QB), lambda b, i: (b, 0, i)),
            pl.BlockSpec((1, seq, KV_WIDTH), lambda b, i: (b, 0, 0)),
            pl.BlockSpec((1, KV_WIDTH, seq), lambda b, i: (b, 0, 0)),
            pl.BlockSpec((1, seq, IDX_DIM), lambda b, i: (b, 0, 0)),
            pl.BlockSpec((1, IDX_HEADS * IDX_DIM, QB), lambda b, i: (b, 0, i)),
            pl.BlockSpec((1, IDX_HEADS, QB), lambda b, i: (b, 0, i)),
            pl.BlockSpec((QB, ATT_WIDTH), lambda b, i: (b * nq + i, OFF_GB // ATT_WIDTH)),
            pl.BlockSpec((ATT_HEADS, BAND_ROWS, QB), lambda b, i: (0, 0, 0)),
        ],
        out_specs=pl.BlockSpec((QB, ATT_WIDTH), lambda b, i: (b * nq + i, 0)),
        out_shape=jax.ShapeDtypeStruct((m, ATT_WIDTH), bf16),
        scratch_shapes=[
            pltpu.VMEM((seq, QB), i32),
            pltpu.VMEM((seq, QB), bf16),
            pltpu.VMEM((QB, REP * QB), bf16),
            pltpu.VMEM((SUBLANES, QB), f32),
        ] + [pltpu.VMEM((1, REP * QB), f32)] * KV_HEADS
          + [pltpu.VMEM((HEAD_DIM + SUM_ROWS, REP * QB), f32)] * KV_HEADS,
        compiler_params=_cparams(("parallel", "arbitrary")),
        name="dsa",
    )(qT, kn3, vT, ik3, iqT, iwT, proj, band)


def _gelu(x):
    return 0.5 * x * (1.0 + jnp.tanh(math.sqrt(2.0 / math.pi) * (x + 0.044715 * (x * x * x))))


def _sgu_kernel(u_ref, v_ref, g_ref, lng_ref, lnb_ref, ws_ref, bsg_ref, o_ref, wsm_ref, bs_ref):
    @pl.when(pl.program_id(0) == 0)
    def _():
        tpos = lax.broadcasted_iota(i32, (SG_CHUNK, SG_CHUNK), 0) // CHUNK
        spos = lax.broadcasted_iota(i32, (SG_CHUNK, SG_CHUNK), 1) // CHUNK
        for g in range(SG_GROUPS):
            wsm_ref[g] = jnp.where(tpos >= spos, ws_ref[g], 0.0).astype(bf16)
        bs_ref[...] = bsg_ref[...].T

    v = _gelu(v_ref[...])
    mu = jnp.mean(v, axis=-1, keepdims=True)
    vc = v - mu
    var = jnp.mean(vc * vc, axis=-1, keepdims=True)
    vn = (vc * lax.rsqrt(var + EPS) * lng_ref[...] + lnb_ref[...]).astype(bf16)
    for g in range(SG_GROUPS):
        sl = slice(g * SG_GROUP, (g + 1) * SG_GROUP)
        mixed = jnp.dot(wsm_ref[g], vn[:, sl], preferred_element_type=f32) + bs_ref[:, g:g + 1]
        gate = g_ref[:, sl]
        o_ref[:, sl] = (_gelu(u_ref[:, sl]) * mixed * _silu(gate)).astype(o_ref.dtype)


def _sgu(proj, ln_g, ln_b, w_s, b_s, lj):
    m = proj.shape[0]
    no = ln_g.shape[0]
    return pl.pallas_call(
        _sgu_kernel,
        grid=(m // SG_CHUNK,),
        in_specs=[
            pl.BlockSpec((SG_CHUNK, SG_WIDTH), lambda i: (i, 0)),
            pl.BlockSpec((SG_CHUNK, SG_WIDTH), lambda i: (i, 1)),
            pl.BlockSpec((SG_CHUNK, SG_WIDTH), lambda i: (i, 2)),
            pl.BlockSpec((None, 1, SG_WIDTH), lambda i: (lj, 0, 0)),
            pl.BlockSpec((None, 1, SG_WIDTH), lambda i: (lj, 0, 0)),
            pl.BlockSpec((None, SG_GROUPS, SG_CHUNK, SG_CHUNK), lambda i: (lj, 0, 0, 0)),
            pl.BlockSpec((None, SG_GROUPS, SG_CHUNK), lambda i: (lj, 0, 0)),
        ],
        out_specs=pl.BlockSpec((SG_CHUNK, SG_WIDTH), lambda i: (i, 0)),
        out_shape=jax.ShapeDtypeStruct((m, SG_WIDTH), bf16),
        scratch_shapes=[pltpu.VMEM((SG_GROUPS, SG_CHUNK, SG_CHUNK), bf16), pltpu.VMEM((SG_CHUNK, SG_GROUPS), f32)],
        compiler_params=_cparams(("arbitrary",)),
        name="sgu",
    )(proj, proj, proj, ln_g.reshape(no, 1, SG_WIDTH), ln_b.reshape(no, 1, SG_WIDTH), w_s, b_s)


PROJ_TN = 512
EVEN_COL_BLOCKS = (tuple(range(0, 12)) + tuple(range(14, 18)) + (12, 13) + (18, 19))
ODD_COL_BLOCKS = tuple(range(3 * SG_WIDTH // PROJ_TN))
TAIL_START = 10240


def _even_tail(w_in):
    pad = LANES - IDX_DIM - IDX_HEADS
    return jnp.pad(w_in[:, TAIL_START:], ((0, 0), (0, pad))).astype(bf16)


def kernel(x, p, norm_gain, rel_bias, even_w_in, conv_w, conv_b, lru_w_r, lru_b_r, lru_w_i, lru_b_i,
           lru_lambda, q_norm, k_norm, even_w_out, odd_w_in, sg_ln_g, sg_ln_b, sg_w_s, sg_b_s, odd_w_out,
           pe_w, pe_gate_norm, pe_w_gate):
    batch, seq, d = x.shape
    depth = p.shape[0]
    topk = min(TOPK_MAX, seq // 4)
    m = batch * seq
    h = x.reshape(m, d)
    band = _band(rel_bias)
    for layer in range(depth):
        j = layer // 2
        if layer % 2 == 0:
            proj, tail = _proj(h, norm_gain, layer, even_w_in, j, EVEN_COL_BLOCKS, _even_tail(even_w_in[j]),
                               tn=PROJ_TN)
            ya = _lru(proj, conv_w, conv_b, lru_w_r, lru_b_r, lru_w_i, lru_b_i, lru_lambda, j,
                      batch=batch, seq=seq)
            qT, kn, vT, iqT, ik, iwT = _prep(proj, tail, q_norm, k_norm, j, batch=batch, seq=seq)
            yb = _dsa(qT, kn, vT, ik, iqT, iwT, proj, band, batch=batch, seq=seq, topk=topk)
            h = _outproj(h, [ya, yb], even_w_out, j)
        else:
            proj = _proj(h, norm_gain, layer, odd_w_in, j, ODD_COL_BLOCKS, tn=PROJ_TN)
            y = _sgu(proj, sg_ln_g, sg_ln_b, sg_w_s, sg_b_s, j)
            h = _outproj(h, [y], odd_w_out, j)
        h = _pe(h, pe_gate_norm, pe_w_gate, p, pe_w, layer)
    return h.reshape(batch, seq, d)
```

```python
import functools
import math

import jax
import jax.numpy as jnp
from jax import lax
from jax.experimental import pallas as pl
from jax.experimental.pallas import tpu as pltpu

f32 = jnp.float32
bf16 = jnp.bfloat16
i32 = jnp.int32

D_MODEL = 2048
CHUNK = 64
PE_DIM = 256
EPS = 1e-6
RNN_WIDTH = 2048
RNN_BLOCK = 128
CONV_WIDTH = 4
LRU_C = 8.0
ATT_HEADS = 16
HEAD_DIM = 128
KV_HEADS = 4
REP = ATT_HEADS // KV_HEADS
ATT_WIDTH = 2048
KV_WIDTH = 512
IDX_HEADS = 16
IDX_DIM = 64
TOPK_MAX = 256
N_BUCKETS = 32
SG_CHUNK = 128
SG_GROUPS = 16
SG_GROUP = 128
SG_WIDTH = 2048

LANES = 128
SUBLANES = 8
QB = 128
KT = 128
KB = 512
TILES_PER_BLOCK = KB // KT
QK_PIECES = 4
SUM_ROWS = 16
LOG2E = math.log2(math.e)
ATT_C2 = HEAD_DIM ** -0.5 * LOG2E
VMEM_LIMIT = 52 * 1024 * 1024
NEG_BIG = -1e30
FAST_LOGIT_LIMIT = 60.0

OFF_XA, OFF_GA, OFF_Q, OFF_GB, OFF_K, OFF_V, OFF_IQ = 0, 2048, 4096, 6144, 8192, 8704, 9216
EVEN_MAIN = 10240


def _cparams(sem):
    return pltpu.CompilerParams(dimension_semantics=sem, vmem_limit_bytes=VMEM_LIMIT)


def _sigmoid(x):
    return 0.5 * jnp.tanh(0.5 * x) + 0.5


def _silu(x):
    return x * _sigmoid(x)


def _block_lookup(table):
    def f(j):
        out = jnp.int32(table[-1])
        for idx in range(len(table) - 2, -1, -1):
            out = jnp.where(j == idx, table[idx], out)
        return out
    return f


def _rms(x, gain):
    ms = jnp.mean(x * x, axis=-1, keepdims=True)
    return x * lax.rsqrt(ms + EPS) * gain


def _norm_kernel(h_ref, g_ref, o_ref):
    o_ref[...] = _rms(h_ref[...], g_ref[...]).astype(o_ref.dtype)


def _norm(h, gains, layer, *, tm=512):
    m, d = h.shape
    return pl.pallas_call(
        _norm_kernel,
        grid=(m // tm,),
        in_specs=[pl.BlockSpec((tm, d), lambda i: (i, 0)),
                  pl.BlockSpec((None, 1, d), lambda i: (layer, 0, 0))],
        out_specs=pl.BlockSpec((tm, d), lambda i: (i, 0)),
        out_shape=jax.ShapeDtypeStruct((m, d), bf16),
        compiler_params=_cparams(("parallel",)),
        name="norm",
    )(h, gains.reshape(gains.shape[0], 1, d))


_NT = (((1,), (1,)), ((), ()))


def _proj_kernel(hn_ref, w_ref, o_ref, wb_ref, *, transposed):
    @pl.when(pl.program_id(1) == 0)
    def _():
        wb_ref[...] = w_ref[...].astype(bf16)

    if transposed:
        acc = lax.dot_general(hn_ref[...], wb_ref[...], _NT, preferred_element_type=f32)
    else:
        acc = jnp.dot(hn_ref[...], wb_ref[...], preferred_element_type=f32)
    o_ref[...] = acc.astype(o_ref.dtype)


def _proj(hn, ws, widx, col_blocks, *, transposed=False, tm=1024, tn=1024, out_dtype=f32):
    m, d = hn.shape
    n = len(col_blocks) * tn
    colmap = _block_lookup(col_blocks)
    if transposed:
        w_spec = pl.BlockSpec((None, tn, d), lambda j, i: (widx, colmap(j), 0))
        wb_shape = (tn, d)
    else:
        w_spec = pl.BlockSpec((None, d, tn), lambda j, i: (widx, 0, colmap(j)))
        wb_shape = (d, tn)
    return pl.pallas_call(
        functools.partial(_proj_kernel, transposed=transposed),
        grid=(n // tn, m // tm),
        in_specs=[pl.BlockSpec((tm, d), lambda j, i: (i, 0)), w_spec],
        out_specs=pl.BlockSpec((tm, tn), lambda j, i: (i, j)),
        out_shape=jax.ShapeDtypeStruct((m, n), out_dtype),
        scratch_shapes=[pltpu.VMEM(wb_shape, bf16)],
        compiler_params=_cparams(("parallel", "arbitrary")),
        name="proj",
    )(hn, ws)


def _proj_tail_kernel(hn_ref, wt_ref, o_ref, *, tail_rows):
    row = lax.broadcasted_iota(i32, wt_ref.shape, 0)
    wt = jnp.where(row < tail_rows, wt_ref[...], 0.0).astype(bf16)
    o_ref[...] = lax.dot_general(hn_ref[...], wt, _NT, preferred_element_type=f32)


def _proj_tail(hn, wts, widx, tail_start, *, tm=1024):
    m, d = hn.shape
    tail_rows = wts.shape[1] - tail_start
    assert 0 < tail_rows < LANES and tail_start % LANES == 0
    return pl.pallas_call(
        functools.partial(_proj_tail_kernel, tail_rows=tail_rows),
        grid=(m // tm,),
        in_specs=[
            pl.BlockSpec((tm, d), lambda i: (i, 0)),
            pl.BlockSpec((None, LANES, d), lambda i: (widx, tail_start // LANES, 0)),
        ],
        out_specs=pl.BlockSpec((tm, LANES), lambda i: (i, 0)),
        out_shape=jax.ShapeDtypeStruct((m, LANES), f32),
        compiler_params=_cparams(("parallel",)),
        name="proj_tail",
    )(hn, wts)


def _outproj_kernel(*refs, n_y):
    h_ref = refs[0]
    y_refs = refs[1:1 + n_y]
    w_refs = refs[1 + n_y:1 + 2 * n_y]
    o_ref = refs[1 + 2 * n_y]
    wb_refs = refs[2 + 2 * n_y:]

    @pl.when(pl.program_id(1) == 0)
    def _():
        for w_ref, wb_ref in zip(w_refs, wb_refs):
            wb_ref[...] = w_ref[...].astype(bf16)

    acc = h_ref[...]
    for y_ref, wb_ref in zip(y_refs, wb_refs):
        acc = acc + jnp.dot(y_ref[...], wb_ref[...], preferred_element_type=f32)
    o_ref[...] = acc


def _outproj(h, ys, ws, widx, *, tm=1024, tn=512):
    m, d = h.shape
    n_y = len(ys)
    kw = ys[0].shape[1]
    in_specs = [pl.BlockSpec((tm, tn), lambda j, i: (i, j))]
    in_specs += [pl.BlockSpec((tm, kw), lambda j, i: (i, 0)) for _ in ys]
    in_specs += [pl.BlockSpec((None, kw, tn), lambda j, i, c=c: (widx, c, j)) for c in range(n_y)]
    return pl.pallas_call(
        functools.partial(_outproj_kernel, n_y=n_y),
        grid=(d // tn, m // tm),
        in_specs=in_specs,
        out_specs=pl.BlockSpec((tm, tn), lambda j, i: (i, j)),
        out_shape=jax.ShapeDtypeStruct((m, d), f32),
        scratch_shapes=[pltpu.VMEM((kw, tn), bf16)] * n_y,
        compiler_params=_cparams(("parallel", "arbitrary")),
        name="outproj",
    )(h, *ys, *([ws] * n_y))


def _pe_kernel(h_ref, gn_ref, wg_ref, p_ref, pew_ref, *rest, tn, emit_next):
    if emit_next:
        gnext_ref, o_ref, hnext_ref, hn_ref, pb_ref = rest
    else:
        o_ref, hn_ref, pb_ref = rest
    j = pl.program_id(1)

    @pl.when(j == 0)
    def _():
        hn_ref[...] = _rms(h_ref[...], gn_ref[...]).astype(bf16)
        pb_ref[...] = p_ref[...].astype(bf16)

    cols = pl.ds(pl.multiple_of(j * tn, tn), tn)
    gate = _sigmoid(jnp.dot(hn_ref[...], wg_ref[...].astype(bf16), preferred_element_type=f32))
    e = jnp.dot(pb_ref[...], pew_ref[...].astype(bf16), preferred_element_type=f32)
    o_ref[:, cols] = h_ref[:, cols] + gate * e

    if emit_next:
        @pl.when(j == pl.num_programs(1) - 1)
        def _():
            hnext_ref[...] = _rms(o_ref[...], gnext_ref[...]).astype(bf16)


def _pe(h, gns, wgs, p, pews, layer, next_gains=None, next_layer=None, *, tm=512, tn=512):
    m, d = h.shape
    depth = p.shape[0]
    emit_next = next_gains is not None
    in_specs = [
        pl.BlockSpec((tm, d), lambda i, j: (i, 0)),
        pl.BlockSpec((None, 1, d), lambda i, j: (layer, 0, 0)),
        pl.BlockSpec((None, d, tn), lambda i, j: (layer, 0, j)),
        pl.BlockSpec((None, tm, PE_DIM), lambda i, j: (layer, i, 0)),
        pl.BlockSpec((None, PE_DIM, tn), lambda i, j: (layer, 0, j)),
    ]
    args = [h, gns.reshape(depth, 1, d), wgs, p.reshape(depth, m, PE_DIM), pews]
    out_specs = [pl.BlockSpec((tm, d), lambda i, j: (i, 0))]
    out_shape = [jax.ShapeDtypeStruct((m, d), f32)]
    if emit_next:
        in_specs.append(pl.BlockSpec((None, 1, d), lambda i, j: (next_layer, 0, 0)))
        args.append(next_gains.reshape(next_gains.shape[0], 1, d))
        out_specs.append(pl.BlockSpec((tm, d), lambda i, j: (i, 0)))
        out_shape.append(jax.ShapeDtypeStruct((m, d), bf16))
    res = pl.pallas_call(
        functools.partial(_pe_kernel, tn=tn, emit_next=emit_next),
        grid=(m // tm, d // tn),
        in_specs=in_specs,
        out_specs=out_specs,
        out_shape=out_shape,
        scratch_shapes=[pltpu.VMEM((tm, d), bf16), pltpu.VMEM((tm, PE_DIM), bf16)],
        compiler_params=_cparams(("parallel", "arbitrary")),
        name="pe",
    )(*args)
    return res if emit_next else (res[0], None)


LRU_TT = 512
LRU_CB = 512


def _lru_kernel(x_ref, ga_ref, cw_ref, cb_ref, wr_ref, br_ref, wi_ref, bi_ref, lam_ref, o_ref,
                xext_ref, hcar_ref, a_ref, b_ref):
    tstep = pl.program_id(2)
    pad = SUBLANES

    @pl.when(tstep == 0)
    def _():
        xext_ref[0:pad, :] = jnp.zeros((pad, LRU_CB), f32)
        hcar_ref[...] = jnp.zeros_like(hcar_ref)

    @pl.when(tstep > 0)
    def _():
        xext_ref[0:pad, :] = xext_ref[LRU_TT:LRU_TT + pad, :]

    xext_ref[pad:, :] = x_ref[...]

    xc = cb_ref[...] + cw_ref[CONV_WIDTH - 1:CONV_WIDTH, :] * x_ref[...]
    for j in range(CONV_WIDTH - 1):
        sh = CONV_WIDTH - 1 - j
        xc = xc + cw_ref[j:j + 1, :] * xext_ref[pad - sh:pad - sh + LRU_TT, :]

    xcb = xc.astype(bf16)
    sp = jax.nn.softplus(-lam_ref[...])
    nb = LRU_CB // RNN_BLOCK
    for g in range(nb):
        sl = slice(g * RNN_BLOCK, (g + 1) * RNN_BLOCK)
        xg = xcb[:, sl]
        r = _sigmoid(jnp.dot(xg, wr_ref[g].astype(bf16), preferred_element_type=f32) + br_ref[:, sl])
        ig = _sigmoid(jnp.dot(xg, wi_ref[g].astype(bf16), preferred_element_type=f32) + bi_ref[:, sl])
        log_a = (-LRU_C) * r * sp[:, sl]
        a = jnp.exp(log_a)
        z = jnp.tanh(-log_a) * (1.0 + a * a)
        mult = jnp.where(z > 0.0, z * lax.rsqrt(z), 0.0)
        a_ref[:, sl] = a
        b_ref[:, sl] = mult * ig * xc[:, sl]

    nchunk = LRU_TT // SUBLANES
    a3 = a_ref[...].reshape(nchunk, SUBLANES, LRU_CB)
    b3 = b_ref[...].reshape(nchunk, SUBLANES, LRU_CB)
    row = lax.broadcasted_iota(i32, a3.shape, 1)
    d = 1
    while d < SUBLANES:
        keep = row >= d
        a_sh = jnp.where(keep, pltpu.roll(a3, d, axis=1), 1.0)
        b_sh = jnp.where(keep, pltpu.roll(b3, d, axis=1), 0.0)
        b3 = a3 * b_sh + b3
        a3 = a3 * a_sh
        d *= 2
    a_ref[...] = a3.reshape(LRU_TT, LRU_CB)
    b_ref[...] = b3.reshape(LRU_TT, LRU_CB)

    def carry(c, hprev):
        r0 = pl.multiple_of(c * SUBLANES, SUBLANES)
        h8 = a_ref[pl.ds(r0, SUBLANES), :] * hprev + b_ref[pl.ds(r0, SUBLANES), :]
        gate = ga_ref[pl.ds(r0, SUBLANES), :]
        o_ref[pl.ds(r0, SUBLANES), :] = (h8 * _silu(gate)).astype(o_ref.dtype)
        return jnp.broadcast_to(h8[SUBLANES - 1:SUBLANES, :], (SUBLANES, LRU_CB))

    hcar_ref[...] = lax.fori_loop(0, nchunk, carry, hcar_ref[...], unroll=8)


def _lru(proj, cw, cb, wr, br, wi, bi, lam, lj, *, batch, seq):
    m = proj.shape[0]
    ne = cw.shape[0]
    nt = seq // LRU_TT
    ncb = RNN_WIDTH // LRU_CB
    gpb = LRU_CB // RNN_BLOCK
    ga_off = OFF_GA // LRU_CB
    row = lambda b, c, t: b * nt + t
    vec = pl.BlockSpec((None, 1, LRU_CB), lambda b, c, t: (lj, 0, c))
    gate_w = pl.BlockSpec((None, gpb, RNN_BLOCK, RNN_BLOCK), lambda b, c, t: (lj, c, 0, 0))
    as_vec = lambda a: a.reshape(ne, 1, RNN_WIDTH)
    return pl.pallas_call(
        _lru_kernel,
        grid=(batch, ncb, nt),
        in_specs=[
            pl.BlockSpec((LRU_TT, LRU_CB), lambda b, c, t: (row(b, c, t), c)),
            pl.BlockSpec((LRU_TT, LRU_CB), lambda b, c, t: (row(b, c, t), ga_off + c)),
            pl.BlockSpec((None, CONV_WIDTH, LRU_CB), lambda b, c, t: (lj, 0, c)),
            vec, gate_w, vec, gate_w, vec, vec,
        ],
        out_specs=pl.BlockSpec((LRU_TT, LRU_CB), lambda b, c, t: (row(b, c, t), c)),
        out_shape=jax.ShapeDtypeStruct((m, RNN_WIDTH), bf16),
        scratch_shapes=[
            pltpu.VMEM((LRU_TT + SUBLANES, LRU_CB), f32),
            pltpu.VMEM((SUBLANES, LRU_CB), f32),
            pltpu.VMEM((LRU_TT, LRU_CB), f32),
            pltpu.VMEM((LRU_TT, LRU_CB), f32),
        ],
        compiler_params=_cparams(("parallel", "parallel", "arbitrary")),
        name="lru",
    )(proj, proj, cw, as_vec(cb), wr, as_vec(br), wi, as_vec(bi), as_vec(lam))


PREP_T = 512


def _head_rms(x, gain):
    ms = jnp.mean(x * x, axis=-1, keepdims=True)
    return x * lax.rsqrt(ms + EPS) * gain


def _prep_kernel(q_ref, k_ref, v_ref, iq_ref, tail_ref, qg_ref, kg_ref,
                 qT_ref, kn_ref, vT_ref, iqT_ref, ik_ref, iwT_ref):
    for h in range(ATT_HEADS):
        sl = slice(h * HEAD_DIM, (h + 1) * HEAD_DIM)
        qT_ref[0, sl, :] = (_head_rms(q_ref[:, sl], qg_ref[...]) * ATT_C2).T.astype(bf16)
    for g in range(KV_HEADS):
        sl = slice(g * HEAD_DIM, (g + 1) * HEAD_DIM)
        kn_ref[:, sl] = _head_rms(k_ref[:, sl], kg_ref[...]).astype(bf16)
        vT_ref[0, sl, :] = v_ref[:, sl].T.astype(bf16)
    for c in range(IDX_HEADS * IDX_DIM // LANES):
        sl = slice(c * LANES, (c + 1) * LANES)
        iqT_ref[0, sl, :] = iq_ref[:, sl].T.astype(bf16)
    tail_t = tail_ref[...].T
    ik_ref[...] = tail_ref[:, 0:IDX_DIM].astype(bf16)
    iwT_ref[0] = tail_t[IDX_DIM:IDX_DIM + IDX_HEADS, :]


def _prep(proj, tail, q_gains, k_gains, lj, *, batch, seq):
    m = proj.shape[0]
    ne = q_gains.shape[0]
    nt = seq // PREP_T
    bt = lambda i: (i // nt, 0, i % nt)
    return pl.pallas_call(
        _prep_kernel,
        grid=(m // PREP_T,),
        in_specs=[
            pl.BlockSpec((PREP_T, ATT_WIDTH), lambda i: (i, OFF_Q // ATT_WIDTH)),
            pl.BlockSpec((PREP_T, KV_WIDTH), lambda i: (i, OFF_K // KV_WIDTH)),
            pl.BlockSpec((PREP_T, KV_WIDTH), lambda i: (i, OFF_V // KV_WIDTH)),
            pl.BlockSpec((PREP_T, IDX_HEADS * IDX_DIM), lambda i: (i, OFF_IQ // (IDX_HEADS * IDX_DIM))),
            pl.BlockSpec((PREP_T, LANES), lambda i: (i, 0)),
            pl.BlockSpec((None, 1, HEAD_DIM), lambda i: (lj, 0, 0)),
            pl.BlockSpec((None, 1, HEAD_DIM), lambda i: (lj, 0, 0)),
        ],
        out_specs=[
            pl.BlockSpec((1, ATT_WIDTH, PREP_T), bt),
            pl.BlockSpec((PREP_T, KV_WIDTH), lambda i: (i, 0)),
            pl.BlockSpec((1, KV_WIDTH, PREP_T), bt),
            pl.BlockSpec((1, IDX_HEADS * IDX_DIM, PREP_T), bt),
            pl.BlockSpec((PREP_T, IDX_DIM), lambda i: (i, 0)),
            pl.BlockSpec((1, IDX_HEADS, PREP_T), bt),
        ],
        out_shape=[
            jax.ShapeDtypeStruct((batch, ATT_WIDTH, seq), bf16),
            jax.ShapeDtypeStruct((m, KV_WIDTH), bf16),
            jax.ShapeDtypeStruct((batch, KV_WIDTH, seq), bf16),
            jax.ShapeDtypeStruct((batch, IDX_HEADS * IDX_DIM, seq), bf16),
            jax.ShapeDtypeStruct((m, IDX_DIM), bf16),
            jax.ShapeDtypeStruct((batch, IDX_HEADS, seq), f32),
        ],
        compiler_params=_cparams(("parallel",)),
        name="attn_prep",
    )(proj, proj, proj, proj, tail, q_gains.reshape(ne, 1, HEAD_DIM), k_gains.reshape(ne, 1, HEAD_DIM))


BAND_ROWS = 3 * KT
FAR_BUCKET = N_BUCKETS // 2 - 1


def _band_kernel(tab_ref, o_ref):
    jj = lax.broadcasted_iota(i32, (BAND_ROWS, QB), 0)
    qi = lax.broadcasted_iota(i32, (BAND_ROWS, QB), 1)
    rel = jnp.where(jj < KT, -2 * KT, jj - 2 * KT - qi)
    n = jnp.abs(rel)
    large = jnp.full(rel.shape, 8, i32)
    for thr in (12, 16, 23, 32, 46, 64, 91):
        large = large + (n >= thr).astype(i32)
    bucket = jnp.where(rel > 0, N_BUCKETS // 2, 0) + jnp.where(n < 8, n, large)

    def per_head(h, carry):
        acc = jnp.zeros((BAND_ROWS, QB), f32)
        for b in range(N_BUCKETS):
            acc = jnp.where(bucket == b, tab_ref[b, h], acc)
        o_ref[h] = (acc - tab_ref[FAR_BUCKET, h]) * LOG2E
        return carry

    lax.fori_loop(0, ATT_HEADS, per_head, 0)


def _band(rel_bias):
    return pl.pallas_call(
        _band_kernel,
        in_specs=[pl.BlockSpec(memory_space=pltpu.SMEM)],
        out_specs=pl.BlockSpec(memory_space=pltpu.VMEM),
        out_shape=jax.ShapeDtypeStruct((ATT_HEADS, BAND_ROWS, QB), f32),
        name="bias_band",
    )(rel_bias)


def _dsa_kernel(qT_ref, k_ref, vT_ref, ik_ref, iqT_ref, iwT_ref, gb_ref, band_ref, o_ref,
                key_ref, madd_ref, eye_ref, bound_ref, *state_refs, topk):
    m_refs = state_refs[:KV_HEADS]
    acc_refs = state_refs[KV_HEADS:]
    i = pl.program_id(1)
    nb = i // TILES_PER_BLOCK + 1
    t0 = i * QB
    qpos = t0 + lax.broadcasted_iota(i32, (1, QB), 1)
    limit = (qpos // CHUNK + 1) * CHUNK
    sub_iota = lax.broadcasted_iota(i32, (KB, QB), 0)
    idx_scale = (IDX_DIM ** -0.5) * (IDX_HEADS ** -0.5)
    int_min = jnp.int32(-2 ** 31)

    def score_block(j, carry):
        r0 = pl.multiple_of(j * KB, KB)
        ks = ik_ref[0, pl.ds(r0, KB), :]
        acc = jnp.zeros((KB, QB), f32)
        for hp in range(IDX_HEADS // 2):
            h0, h1 = 2 * hp, 2 * hp + 1
            w = jnp.concatenate([iqT_ref[0, h0 * IDX_DIM:(h0 + 1) * IDX_DIM, :],
                                 iqT_ref[0, h1 * IDX_DIM:(h1 + 1) * IDX_DIM, :]], axis=1)
            s = jnp.dot(ks, w, preferred_element_type=f32)
            acc = acc + jnp.maximum(s[:, :QB], 0.0) * iwT_ref[0, h0:h0 + 1, :]
            acc = acc + jnp.maximum(s[:, QB:], 0.0) * iwT_ref[0, h1:h1 + 1, :]
        score = acc * idx_scale
        score = jnp.where(score == 0.0, 0.0, score)
        score = jnp.where(r0 + sub_iota < limit, score, -jnp.inf)
        bits = pltpu.bitcast(score, i32)
        key_ref[pl.ds(r0, KB), :] = jnp.where(bits >= 0, bits, bits ^ jnp.int32(0x7FFFFFFF))
        return carry

    lax.fori_loop(0, nb, score_block, 0)

    def count_rows(pred_fn):
        def body(j, cnt8):
            r0 = pl.multiple_of(j * KB, KB)
            hit = pred_fn(key_ref[pl.ds(r0, KB), :], r0).astype(i32)
            return cnt8 + jnp.sum(hit.reshape(KB // SUBLANES, SUBLANES, QB), axis=0)
        cnt8 = lax.fori_loop(0, nb, body, jnp.zeros((SUBLANES, QB), i32))
        return jnp.sum(cnt8, axis=0, keepdims=True)

    n_nonneg = count_rows(lambda kt, r0: kt >= 0)
    nonneg = n_nonneg >= topk
    prefix0 = jnp.where(nonneg, jnp.int32(0), int_min)
    cur0 = jnp.where(nonneg, n_nonneg, nb * KB)
    n_bits = 31
    bits_per_check = 4

    def bits_cond(state):
        b, _, _, more = state
        return jnp.logical_and(b < n_bits, more > 0)

    def bits_body(state):
        b, prefix, cur, _ = state
        for u in range(bits_per_check):
            shift = jnp.maximum(n_bits - 1 - (b + u), 0)
            bit = jnp.where(b + u < n_bits, lax.shift_left(jnp.int32(1), shift), 0)
            cand = prefix | bit
            cnt = count_rows(lambda kt, r0, cand=cand: kt >= cand)
            take = cnt >= topk
            prefix = jnp.where(take, cand, prefix)
            cur = jnp.where(take, cnt, cur)
        more = jnp.max((cur > topk).astype(i32))
        return b + bits_per_check, prefix, cur, more

    more0 = jnp.max((cur0 > topk).astype(i32))
    _, tau, cur, _ = lax.while_loop(bits_cond, bits_body, (jnp.int32(0), prefix0, cur0, more0))

    neg_inf_key = jnp.int32(-2 ** 31 + 0x7FFFFF)
    excess = jnp.logical_and(cur > topk, tau > neg_inf_key)
    any_excess = jnp.max(excess.astype(i32)) > 0
    n_gt = lax.cond(any_excess, lambda: count_rows(lambda kt, r0: kt > tau), lambda: jnp.zeros((1, QB), i32))
    room = topk - n_gt
    n_iter = jnp.where(any_excess, 13, 0)

    def cut_step(b, cut):
        cand = cut | lax.shift_left(jnp.int32(1), 12 - b)
        cnt = count_rows(lambda kt, r0: jnp.logical_and(kt == tau, r0 + sub_iota < cand))
        return jnp.where(cnt <= room, cand, cut)

    cut = lax.fori_loop(0, n_iter, cut_step, jnp.zeros((1, QB), i32))
    cut = jnp.where(excess, cut, jnp.int32(2 ** 30))

    def mask_block(j, carry):
        r0 = pl.multiple_of(j * KB, KB)
        kt = key_ref[pl.ds(r0, KB), :]
        pos = r0 + sub_iota
        sel = jnp.logical_or(kt > tau, jnp.logical_and(kt == tau, pos < cut))
        sel = jnp.logical_and(sel, pos < limit)
        madd_ref[pl.ds(r0, KB), :] = jnp.where(sel, 0.0, NEG_BIG).astype(bf16)
        return carry

    lax.fori_loop(0, nb, mask_block, 0)

    @pl.when(i == 0)
    def _():
        rowi = lax.broadcasted_iota(i32, (QB, REP * QB), 0)
        coli = lax.broadcasted_iota(i32, (QB, REP * QB), 1)
        eye_ref[...] = ((coli & (QB - 1)) == rowi).astype(bf16)
        for g in range(KV_HEADS):
            def kmax_body(j, mx, g=g):
                r0 = pl.multiple_of(j * KB, KB)
                kf = k_ref[0, pl.ds(r0, KB), g * HEAD_DIM:(g + 1) * HEAD_DIM].astype(f32)
                n2 = jnp.sum(kf * kf, axis=-1, keepdims=True)
                return jnp.maximum(mx, jnp.max(n2, axis=0, keepdims=True))
            k2 = lax.fori_loop(0, k_ref.shape[1] // KB, kmax_body, jnp.zeros((1, 1), f32))
            bound_ref[g:g + 1, :] = jnp.broadcast_to(k2, (1, QB))
        bmax = jnp.max(jnp.abs(band_ref[...]).reshape(ATT_HEADS * BAND_ROWS, QB), axis=0, keepdims=True)
        room = FAST_LOGIT_LIMIT - jnp.max(bmax, axis=1, keepdims=True)
        bound_ref[KV_HEADS:KV_HEADS + 1, :] = jnp.broadcast_to(jnp.where(room > 0.0, room * room, -1.0), (1, QB))

    over = jnp.zeros((1, QB), i32)
    for g in range(KV_HEADS):
        q2 = jnp.zeros((1, QB), f32)
        for h in range(g * REP, (g + 1) * REP):
            qh = qT_ref[0, h * HEAD_DIM:(h + 1) * HEAD_DIM, :].astype(f32)
            q2 = jnp.maximum(q2, jnp.sum(qh * qh, axis=0, keepdims=True))
        over = over | (q2 * bound_ref[g:g + 1, :] * 1.002 > bound_ref[KV_HEADS:KV_HEADS + 1, :]).astype(i32)
    needs_shift = jnp.max(over) > 0

    for m_ref, acc_ref in zip(m_refs, acc_refs):
        m_ref[...] = jnp.full(m_ref.shape, NEG_BIG, f32)
        acc_ref[...] = jnp.zeros(acc_ref.shape, f32)

    ones_rows = jnp.ones((SUM_ROWS, KB), bf16)

    def att_block(j, carry, near, online):
        r0 = pl.multiple_of(j * KB, KB)

        def qk(g):
            qg = jnp.concatenate([qT_ref[0, h * HEAD_DIM:(h + 1) * HEAD_DIM, :]
                                  for h in range(g * REP, (g + 1) * REP)], axis=1)
            rhs = jnp.concatenate([qg, eye_ref[...]], axis=0)
            piece = KB // QK_PIECES
            parts = []
            for c in range(QK_PIECES):
                rows = pl.ds(r0 + c * piece, piece)
                lhs = jnp.concatenate([k_ref[0, rows, g * HEAD_DIM:(g + 1) * HEAD_DIM], madd_ref[rows, :]],
                                      axis=1)
                parts.append(jnp.dot(lhs, rhs, preferred_element_type=f32))
            return jnp.concatenate(parts, axis=0)

        ts = {0: qk(0)}
        for g in range(KV_HEADS):
            heads = range(g * REP, (g + 1) * REP)
            if g + 1 < KV_HEADS:
                ts[g + 1] = qk(g + 1)
            t = ts.pop(g)
            if near:
                rows = []
                for u in range(TILES_PER_BLOCK):
                    band_row = pl.multiple_of(jnp.clip(j * TILES_PER_BLOCK + u - i + 2, 0, 2) * KT, KT)
                    rows.append(jnp.concatenate([band_ref[h, pl.ds(band_row, KT), :] for h in heads], axis=1))
                t = t + jnp.concatenate(rows, axis=0)
            vt = jnp.concatenate([vT_ref[0, g * HEAD_DIM:(g + 1) * HEAD_DIM, pl.ds(r0, KB)], ones_rows], axis=0)
            if online:
                m_old = m_refs[g][...]
                m_new = jnp.maximum(m_old, jnp.max(t, axis=0, keepdims=True))
                alpha = jnp.exp2(m_old - m_new)
                p = jnp.exp2(t - m_new)
                m_refs[g][...] = m_new
                acc_refs[g][...] = (acc_refs[g][...] * alpha
                                    + jnp.dot(vt, p.astype(bf16), preferred_element_type=f32))
            else:
                p = jnp.exp2(t)
                acc_refs[g][...] = acc_refs[g][...] + jnp.dot(vt, p.astype(bf16), preferred_element_type=f32)
        return carry

    n_far = jnp.maximum(i - 1, 0) // TILES_PER_BLOCK

    def run_blocks(online):
        lax.fori_loop(0, n_far, functools.partial(att_block, near=False, online=online), 0)
        lax.fori_loop(n_far, nb, functools.partial(att_block, near=True, online=online), 0)

    lax.cond(needs_shift, functools.partial(run_blocks, True), functools.partial(run_blocks, False))

    for g in range(KV_HEADS):
        heads = range(g * REP, (g + 1) * REP)
        oT = acc_refs[g][0:HEAD_DIM, :] / acc_refs[g][HEAD_DIM:HEAD_DIM + 1, :]
        for r, h in enumerate(heads):
            gate = gb_ref[:, h * HEAD_DIM:(h + 1) * HEAD_DIM]
            o = oT[:, r * QB:(r + 1) * QB].T
            o_ref[:, h * HEAD_DIM:(h + 1) * HEAD_DIM] = (o * _silu(gate)).astype(o_ref.dtype)


def _dsa(qT, kn, vT, ik, iqT, iwT, proj, band, *, batch, seq, topk):
    m = proj.shape[0]
    nq = seq // QB
    kn3 = kn.reshape(batch, seq, KV_WIDTH)
    ik3 = ik.reshape(batch, seq, IDX_DIM)
    return pl.pallas_call(
        functools.partial(_dsa_kernel, topk=topk),
        grid=(batch, nq),
        in_specs=[
            pl.BlockSpec((1, ATT_WIDTH, QB), lambda b, i: (b, 0, i)),
            pl.BlockSpec((1, seq, KV_WIDTH), lambda b, i: (b, 0, 0)),
            pl.BlockSpec((1, KV_WIDTH, seq), lambda b, i: (b, 0, 0)),
            pl.BlockSpec((1, seq, IDX_DIM), lambda b, i: (b, 0, 0)),
            pl.BlockSpec((1, IDX_HEADS * IDX_DIM, QB), lambda b, i: (b, 0, i)),
            pl.BlockSpec((1, IDX_HEADS, QB), lambda b, i: (b, 0, i)),
            pl.BlockSpec((QB, ATT_WIDTH), lambda b, i: (b * nq + i, OFF_GB // ATT_WIDTH)),
            pl.BlockSpec((ATT_HEADS, BAND_ROWS, QB), lambda b, i: (0, 0, 0)),
        ],
        out_specs=pl.BlockSpec((QB, ATT_WIDTH), lambda b, i: (b * nq + i, 0)),
        out_shape=jax.ShapeDtypeStruct((m, ATT_WIDTH), bf16),
        scratch_shapes=[
            pltpu.VMEM((seq, QB), i32),
            pltpu.VMEM((seq, QB), bf16),
            pltpu.VMEM((QB, REP * QB), bf16),
            pltpu.VMEM((SUBLANES, QB), f32),
        ] + [pltpu.VMEM((1, REP * QB), f32)] * KV_HEADS
          + [pltpu.VMEM((HEAD_DIM + SUM_ROWS, REP * QB), f32)] * KV_HEADS,
        compiler_params=_cparams(("parallel", "arbitrary")),
        name="dsa",
    )(qT, kn3, vT, ik3, iqT, iwT, proj, band)


def _gelu(x):
    return 0.5 * x * (1.0 + jnp.tanh(math.sqrt(2.0 / math.pi) * (x + 0.044715 * (x * x * x))))


def _sgu_kernel(u_ref, v_ref, g_ref, lng_ref, lnb_ref, ws_ref, bsg_ref, o_ref, wsm_ref, bs_ref):
    @pl.when(pl.program_id(0) == 0)
    def _():
        tpos = lax.broadcasted_iota(i32, (SG_CHUNK, SG_CHUNK), 0) // CHUNK
        spos = lax.broadcasted_iota(i32, (SG_CHUNK, SG_CHUNK), 1) // CHUNK
        for g in range(SG_GROUPS):
            wsm_ref[g] = jnp.where(tpos >= spos, ws_ref[g], 0.0).astype(bf16)
        bs_ref[...] = bsg_ref[...].T

    v = _gelu(v_ref[...])
    mu = jnp.mean(v, axis=-1, keepdims=True)
    vc = v - mu
    var = jnp.mean(vc * vc, axis=-1, keepdims=True)
    vn = (vc * lax.rsqrt(var + EPS) * lng_ref[...] + lnb_ref[...]).astype(bf16)
    for g in range(SG_GROUPS):
        sl = slice(g * SG_GROUP, (g + 1) * SG_GROUP)
        mixed = jnp.dot(wsm_ref[g], vn[:, sl], preferred_element_type=f32) + bs_ref[:, g:g + 1]
        gate = g_ref[:, sl]
        o_ref[:, sl] = (_gelu(u_ref[:, sl]) * mixed * _silu(gate)).astype(o_ref.dtype)


def _sgu(proj, ln_g, ln_b, w_s, b_s, lj):
    m = proj.shape[0]
    no = ln_g.shape[0]
    return pl.pallas_call(
        _sgu_kernel,
        grid=(m // SG_CHUNK,),
        in_specs=[
            pl.BlockSpec((SG_CHUNK, SG_WIDTH), lambda i: (i, 0)),
            pl.BlockSpec((SG_CHUNK, SG_WIDTH), lambda i: (i, 1)),
            pl.BlockSpec((SG_CHUNK, SG_WIDTH), lambda i: (i, 2)),
            pl.BlockSpec((None, 1, SG_WIDTH), lambda i: (lj, 0, 0)),
            pl.BlockSpec((None, 1, SG_WIDTH), lambda i: (lj, 0, 0)),
            pl.BlockSpec((None, SG_GROUPS, SG_CHUNK, SG_CHUNK), lambda i: (lj, 0, 0, 0)),
            pl.BlockSpec((None, SG_GROUPS, SG_CHUNK), lambda i: (lj, 0, 0)),
        ],
        out_specs=pl.BlockSpec((SG_CHUNK, SG_WIDTH), lambda i: (i, 0)),
        out_shape=jax.ShapeDtypeStruct((m, SG_WIDTH), bf16),
        scratch_shapes=[pltpu.VMEM((SG_GROUPS, SG_CHUNK, SG_CHUNK), bf16), pltpu.VMEM((SG_CHUNK, SG_GROUPS), f32)],
        compiler_params=_cparams(("arbitrary",)),
        name="sgu",
    )(proj, proj, proj, ln_g.reshape(no, 1, SG_WIDTH), ln_b.reshape(no, 1, SG_WIDTH), w_s, b_s)


PROJ_TN = 1024
EVEN_COL_BLOCKS = (0, 1, 2, 3, 4, 5, 7, 8, 6, 9)
ODD_COL_BLOCKS = tuple(range(3 * SG_WIDTH // PROJ_TN))
TAIL_START = 10240


def kernel(x, p, norm_gain, rel_bias, even_w_in, conv_w, conv_b, lru_w_r, lru_b_r, lru_w_i, lru_b_i,
           lru_lambda, q_norm, k_norm, even_w_out, odd_w_in, sg_ln_g, sg_ln_b, sg_w_s, sg_b_s, odd_w_out,
           pe_w, pe_gate_norm, pe_w_gate):
    batch, seq, d = x.shape
    depth = p.shape[0]
    topk = min(TOPK_MAX, seq // 4)
    m = batch * seq
    h = x.reshape(m, d)
    band = _band(rel_bias)
    hn = _norm(h, norm_gain, 0)
    even_w_in_t = jnp.swapaxes(even_w_in, 1, 2)
    for layer in range(depth):
        j = layer // 2
        if layer % 2 == 0:
            proj = _proj(hn, even_w_in_t, j, EVEN_COL_BLOCKS, transposed=True, tn=PROJ_TN)
            tail = _proj_tail(hn, even_w_in_t, j, TAIL_START)
            ya = _lru(proj, conv_w, conv_b, lru_w_r, lru_b_r, lru_w_i, lru_b_i, lru_lambda, j,
                      batch=batch, seq=seq)
            qT, kn, vT, iqT, ik, iwT = _prep(proj, tail, q_norm, k_norm, j, batch=batch, seq=seq)
            yb = _dsa(qT, kn, vT, ik, iqT, iwT, proj, band, batch=batch, seq=seq, topk=topk)
            h = _outproj(h, [ya, yb], even_w_out, j)
        else:
            proj = _proj(hn, odd_w_in, j, ODD_COL_BLOCKS, tn=PROJ_TN)
            y = _sgu(proj, sg_ln_g, sg_ln_b, sg_w_s, sg_b_s, j)
            h = _outproj(h, [y], odd_w_out, j)
        if layer + 1 < depth:
            h, hn = _pe(h, pe_gate_norm, pe_w_gate, p, pe_w, layer, norm_gain, layer + 1)
        else:
            h, _ = _pe(h, pe_gate_norm, pe_w_gate, p, pe_w, layer)
    return h.reshape(batch, seq, d)
```

```python
import functools
import math

import jax
import jax.numpy as jnp
from jax import lax
from jax.experimental import pallas as pl
from jax.experimental.pallas import tpu as pltpu

f32 = jnp.float32
bf16 = jnp.bfloat16
i32 = jnp.int32

D_MODEL = 2048
CHUNK = 64
PE_DIM = 256
EPS = 1e-6
RNN_WIDTH = 2048
RNN_BLOCK = 128
CONV_WIDTH = 4
LRU_C = 8.0
ATT_HEADS = 16
HEAD_DIM = 128
KV_HEADS = 4
REP = ATT_HEADS // KV_HEADS
ATT_WIDTH = 2048
KV_WIDTH = 512
IDX_HEADS = 16
IDX_DIM = 64
TOPK_MAX = 256
N_BUCKETS = 32
SG_CHUNK = 128
SG_GROUPS = 16
SG_GROUP = 128
SG_WIDTH = 2048

LANES = 128
SUBLANES = 8
QB = 128
KT = 128
KB = 512
TILES_PER_BLOCK = KB // KT
QK_PIECES = 4
SUM_ROWS = 16
LOG2E = math.log2(math.e)
ATT_C2 = HEAD_DIM ** -0.5 * LOG2E
VMEM_LIMIT = 52 * 1024 * 1024
NEG_BIG = -1e30
FAST_LOGIT_LIMIT = 60.0

OFF_XA, OFF_GA, OFF_Q, OFF_GB, OFF_K, OFF_V, OFF_IQ = 0, 2048, 4096, 6144, 8192, 8704, 9216
EVEN_MAIN = 10240


def _cparams(sem):
    return pltpu.CompilerParams(dimension_semantics=sem, vmem_limit_bytes=VMEM_LIMIT)


def _sigmoid(x):
    return 0.5 * jnp.tanh(0.5 * x) + 0.5


def _silu(x):
    return x * _sigmoid(x)


def _block_lookup(table):
    def f(j):
        out = jnp.int32(table[-1])
        for idx in range(len(table) - 2, -1, -1):
            out = jnp.where(j == idx, table[idx], out)
        return out
    return f


def _rms(x, gain):
    ms = jnp.mean(x * x, axis=-1, keepdims=True)
    return x * lax.rsqrt(ms + EPS) * gain


def _norm_kernel(h_ref, g_ref, o_ref):
    o_ref[...] = _rms(h_ref[...], g_ref[...]).astype(o_ref.dtype)


def _norm(h, gains, layer, *, tm=512):
    m, d = h.shape
    return pl.pallas_call(
        _norm_kernel,
        grid=(m // tm,),
        in_specs=[pl.BlockSpec((tm, d), lambda i: (i, 0)),
                  pl.BlockSpec((None, 1, d), lambda i: (layer, 0, 0))],
        out_specs=pl.BlockSpec((tm, d), lambda i: (i, 0)),
        out_shape=jax.ShapeDtypeStruct((m, d), bf16),
        compiler_params=_cparams(("parallel",)),
        name="norm",
    )(h, gains.reshape(gains.shape[0], 1, d))


_NT = (((1,), (1,)), ((), ()))


def _proj_kernel(hn_ref, w_ref, o_ref, wb_ref, *, transposed):
    @pl.when(pl.program_id(1) == 0)
    def _():
        wb_ref[...] = w_ref[...].astype(bf16)

    if transposed:
        acc = lax.dot_general(hn_ref[...], wb_ref[...], _NT, preferred_element_type=f32)
    else:
        acc = jnp.dot(hn_ref[...], wb_ref[...], preferred_element_type=f32)
    o_ref[...] = acc.astype(o_ref.dtype)


def _proj(hn, ws, widx, col_blocks, *, transposed=False, tm=1024, tn=1024, out_dtype=f32):
    m, d = hn.shape
    n = len(col_blocks) * tn
    colmap = _block_lookup(col_blocks)
    if transposed:
        w_spec = pl.BlockSpec((None, tn, d), lambda j, i: (widx, colmap(j), 0))
        wb_shape = (tn, d)
    else:
        w_spec = pl.BlockSpec((None, d, tn), lambda j, i: (widx, 0, colmap(j)))
        wb_shape = (d, tn)
    return pl.pallas_call(
        functools.partial(_proj_kernel, transposed=transposed),
        grid=(n // tn, m // tm),
        in_specs=[pl.BlockSpec((tm, d), lambda j, i: (i, 0)), w_spec],
        out_specs=pl.BlockSpec((tm, tn), lambda j, i: (i, j)),
        out_shape=jax.ShapeDtypeStruct((m, n), out_dtype),
        scratch_shapes=[pltpu.VMEM(wb_shape, bf16)],
        compiler_params=_cparams(("parallel", "arbitrary")),
        name="proj",
    )(hn, ws)


def _proj_tail_kernel(hn_ref, wt_ref, o_ref, *, tail_rows):
    row = lax.broadcasted_iota(i32, wt_ref.shape, 0)
    wt = jnp.where(row < tail_rows, wt_ref[...], 0.0).astype(bf16)
    o_ref[...] = lax.dot_general(hn_ref[...], wt, _NT, preferred_element_type=f32)


def _proj_tail(hn, wts, widx, tail_start, *, tm=1024):
    m, d = hn.shape
    tail_rows = wts.shape[1] - tail_start
    assert 0 < tail_rows < LANES and tail_start % LANES == 0
    return pl.pallas_call(
        functools.partial(_proj_tail_kernel, tail_rows=tail_rows),
        grid=(m // tm,),
        in_specs=[
            pl.BlockSpec((tm, d), lambda i: (i, 0)),
            pl.BlockSpec((None, LANES, d), lambda i: (widx, tail_start // LANES, 0)),
        ],
        out_specs=pl.BlockSpec((tm, LANES), lambda i: (i, 0)),
        out_shape=jax.ShapeDtypeStruct((m, LANES), f32),
        compiler_params=_cparams(("parallel",)),
        name="proj_tail",
    )(hn, wts)


def _cast_kernel(w_ref, o_ref):
    o_ref[...] = w_ref[...].astype(o_ref.dtype)


def _cast_bf16(w, *, tr=512):
    lead, rows, cols = w.shape
    tr = min(tr, rows)
    return pl.pallas_call(
        _cast_kernel,
        grid=(lead, rows // tr),
        in_specs=[pl.BlockSpec((None, tr, cols), lambda a, r: (a, r, 0))],
        out_specs=pl.BlockSpec((None, tr, cols), lambda a, r: (a, r, 0)),
        out_shape=jax.ShapeDtypeStruct(w.shape, bf16),
        compiler_params=_cparams(("parallel", "parallel")),
        name="cast_bf16",
    )(w)


TAIL_VMEM_LIMIT = 56 * 1024 * 1024


def _tail_kernel(*refs, n_y, tn, n_col, emit_next):
    h_ref = refs[0]
    y_refs = refs[1:1 + n_y]
    wo_refs = refs[1 + n_y:1 + 2 * n_y]
    gn_ref, wg_ref, p_ref, pew_ref = refs[1 + 2 * n_y:5 + 2 * n_y]
    rest = refs[5 + 2 * n_y:]
    if emit_next:
        gnext_ref, o_ref, hnext_ref, h1_ref, hn_ref, pb_ref = rest
    else:
        o_ref, h1_ref, hn_ref, pb_ref = rest
    j = pl.program_id(1)

    @pl.when(j < n_col)
    def _():
        cols = pl.ds(pl.multiple_of(j * tn, tn), tn)
        acc = h_ref[...]
        for y_ref, wo_ref in zip(y_refs, wo_refs):
            acc = acc + jnp.dot(y_ref[...], wo_ref[...], preferred_element_type=f32)
        h1_ref[:, cols] = acc

    @pl.when(j == n_col)
    def _():
        hn_ref[...] = _rms(h1_ref[...], gn_ref[...]).astype(bf16)
        pb_ref[...] = p_ref[...].astype(bf16)

    @pl.when(j >= n_col)
    def _():
        cols = pl.ds(pl.multiple_of((j - n_col) * tn, tn), tn)
        gate = _sigmoid(jnp.dot(hn_ref[...], wg_ref[...], preferred_element_type=f32))
        e = jnp.dot(pb_ref[...], pew_ref[...], preferred_element_type=f32)
        o_ref[:, cols] = h1_ref[:, cols] + gate * e

    if emit_next:
        @pl.when(j == 2 * n_col - 1)
        def _():
            hnext_ref[...] = _rms(o_ref[...], gnext_ref[...]).astype(bf16)


def _tail(h, ys, wo_b, widx, gns, wg_b, p, pew_b, layer, next_gains=None, next_layer=None, *, tm=512, tn=512):
    m, d = h.shape
    depth = p.shape[0]
    n_y = len(ys)
    kw = ys[0].shape[1]
    n_col = d // tn
    emit_next = next_gains is not None
    first = lambda j: jnp.minimum(j, n_col - 1)
    second = lambda j: jnp.maximum(j - n_col, 0)
    in_specs = [pl.BlockSpec((tm, tn), lambda i, j: (i, first(j)))]
    in_specs += [pl.BlockSpec((tm, kw), lambda i, j: (i, 0)) for _ in ys]
    in_specs += [pl.BlockSpec((None, kw, tn), lambda i, j, c=c: (widx, c, first(j))) for c in range(n_y)]
    in_specs += [
        pl.BlockSpec((None, 1, d), lambda i, j: (layer, 0, 0)),
        pl.BlockSpec((None, d, tn), lambda i, j: (layer, 0, second(j))),
        pl.BlockSpec((None, tm, PE_DIM), lambda i, j: (layer, i, 0)),
        pl.BlockSpec((None, PE_DIM, tn), lambda i, j: (layer, 0, second(j))),
    ]
    args = [h, *ys, *([wo_b] * n_y), gns.reshape(depth, 1, d), wg_b, p.reshape(depth, m, PE_DIM), pew_b]
    out_specs = [pl.BlockSpec((tm, d), lambda i, j: (i, 0))]
    out_shape = [jax.ShapeDtypeStruct((m, d), f32)]
    if emit_next:
        in_specs.append(pl.BlockSpec((None, 1, d), lambda i, j: (next_layer, 0, 0)))
        args.append(next_gains.reshape(next_gains.shape[0], 1, d))
        out_specs.append(pl.BlockSpec((tm, d), lambda i, j: (i, 0)))
        out_shape.append(jax.ShapeDtypeStruct((m, d), bf16))
    res = pl.pallas_call(
        functools.partial(_tail_kernel, n_y=n_y, tn=tn, n_col=n_col, emit_next=emit_next),
        grid=(m // tm, 2 * n_col),
        in_specs=in_specs,
        out_specs=out_specs,
        out_shape=out_shape,
        scratch_shapes=[pltpu.VMEM((tm, d), f32), pltpu.VMEM((tm, d), bf16), pltpu.VMEM((tm, PE_DIM), bf16)],
        compiler_params=pltpu.CompilerParams(dimension_semantics=("parallel", "arbitrary"),
                                             vmem_limit_bytes=TAIL_VMEM_LIMIT),
        name="tail",
    )(*args)
    return res if emit_next else (res[0], None)


def _outproj_kernel(*refs, n_y):
    h_ref = refs[0]
    y_refs = refs[1:1 + n_y]
    w_refs = refs[1 + n_y:1 + 2 * n_y]
    o_ref = refs[1 + 2 * n_y]
    wb_refs = refs[2 + 2 * n_y:]

    @pl.when(pl.program_id(1) == 0)
    def _():
        for w_ref, wb_ref in zip(w_refs, wb_refs):
            wb_ref[...] = w_ref[...].astype(bf16)

    acc = h_ref[...]
    for y_ref, wb_ref in zip(y_refs, wb_refs):
        acc = acc + jnp.dot(y_ref[...], wb_ref[...], preferred_element_type=f32)
    o_ref[...] = acc


def _outproj(h, ys, ws, widx, *, tm=1024, tn=512):
    m, d = h.shape
    n_y = len(ys)
    kw = ys[0].shape[1]
    in_specs = [pl.BlockSpec((tm, tn), lambda j, i: (i, j))]
    in_specs += [pl.BlockSpec((tm, kw), lambda j, i: (i, 0)) for _ in ys]
    in_specs += [pl.BlockSpec((None, kw, tn), lambda j, i, c=c: (widx, c, j)) for c in range(n_y)]
    return pl.pallas_call(
        functools.partial(_outproj_kernel, n_y=n_y),
        grid=(d // tn, m // tm),
        in_specs=in_specs,
        out_specs=pl.BlockSpec((tm, tn), lambda j, i: (i, j)),
        out_shape=jax.ShapeDtypeStruct((m, d), f32),
        scratch_shapes=[pltpu.VMEM((kw, tn), bf16)] * n_y,
        compiler_params=_cparams(("parallel", "arbitrary")),
        name="outproj",
    )(h, *ys, *([ws] * n_y))


def _pe_kernel(h_ref, gn_ref, wg_ref, p_ref, pew_ref, *rest, tn, emit_next):
    if emit_next:
        gnext_ref, o_ref, hnext_ref, hn_ref, pb_ref = rest
    else:
        o_ref, hn_ref, pb_ref = rest
    j = pl.program_id(1)

    @pl.when(j == 0)
    def _():
        hn_ref[...] = _rms(h_ref[...], gn_ref[...]).astype(bf16)
        pb_ref[...] = p_ref[...].astype(bf16)

    cols = pl.ds(pl.multiple_of(j * tn, tn), tn)
    gate = _sigmoid(jnp.dot(hn_ref[...], wg_ref[...].astype(bf16), preferred_element_type=f32))
    e = jnp.dot(pb_ref[...], pew_ref[...].astype(bf16), preferred_element_type=f32)
    o_ref[:, cols] = h_ref[:, cols] + gate * e

    if emit_next:
        @pl.when(j == pl.num_programs(1) - 1)
        def _():
            hnext_ref[...] = _rms(o_ref[...], gnext_ref[...]).astype(bf16)


def _pe(h, gns, wgs, p, pews, layer, next_gains=None, next_layer=None, *, tm=512, tn=512):
    m, d = h.shape
    depth = p.shape[0]
    emit_next = next_gains is not None
    in_specs = [
        pl.BlockSpec((tm, d), lambda i, j: (i, 0)),
        pl.BlockSpec((None, 1, d), lambda i, j: (layer, 0, 0)),
        pl.BlockSpec((None, d, tn), lambda i, j: (layer, 0, j)),
        pl.BlockSpec((None, tm, PE_DIM), lambda i, j: (layer, i, 0)),
        pl.BlockSpec((None, PE_DIM, tn), lambda i, j: (layer, 0, j)),
    ]
    args = [h, gns.reshape(depth, 1, d), wgs, p.reshape(depth, m, PE_DIM), pews]
    out_specs = [pl.BlockSpec((tm, d), lambda i, j: (i, 0))]
    out_shape = [jax.ShapeDtypeStruct((m, d), f32)]
    if emit_next:
        in_specs.append(pl.BlockSpec((None, 1, d), lambda i, j: (next_layer, 0, 0)))
        args.append(next_gains.reshape(next_gains.shape[0], 1, d))
        out_specs.append(pl.BlockSpec((tm, d), lambda i, j: (i, 0)))
        out_shape.append(jax.ShapeDtypeStruct((m, d), bf16))
    res = pl.pallas_call(
        functools.partial(_pe_kernel, tn=tn, emit_next=emit_next),
        grid=(m // tm, d // tn),
        in_specs=in_specs,
        out_specs=out_specs,
        out_shape=out_shape,
        scratch_shapes=[pltpu.VMEM((tm, d), bf16), pltpu.VMEM((tm, PE_DIM), bf16)],
        compiler_params=_cparams(("parallel", "arbitrary")),
        name="pe",
    )(*args)
    return res if emit_next else (res[0], None)


LRU_TT = 512
LRU_CB = 512


def _lru_kernel(x_ref, ga_ref, cw_ref, cb_ref, wr_ref, br_ref, wi_ref, bi_ref, lam_ref, o_ref,
                xext_ref, hcar_ref, a_ref, b_ref):
    tstep = pl.program_id(2)
    pad = SUBLANES

    @pl.when(tstep == 0)
    def _():
        xext_ref[0:pad, :] = jnp.zeros((pad, LRU_CB), f32)
        hcar_ref[...] = jnp.zeros_like(hcar_ref)

    @pl.when(tstep > 0)
    def _():
        xext_ref[0:pad, :] = xext_ref[LRU_TT:LRU_TT + pad, :]

    xext_ref[pad:, :] = x_ref[...]

    xc = cb_ref[...] + cw_ref[CONV_WIDTH - 1:CONV_WIDTH, :] * x_ref[...]
    for j in range(CONV_WIDTH - 1):
        sh = CONV_WIDTH - 1 - j
        xc = xc + cw_ref[j:j + 1, :] * xext_ref[pad - sh:pad - sh + LRU_TT, :]

    xcb = xc.astype(bf16)
    sp = jax.nn.softplus(-lam_ref[...])
    nb = LRU_CB // RNN_BLOCK
    for g in range(nb):
        sl = slice(g * RNN_BLOCK, (g + 1) * RNN_BLOCK)
        xg = xcb[:, sl]
        r = _sigmoid(jnp.dot(xg, wr_ref[g].astype(bf16), preferred_element_type=f32) + br_ref[:, sl])
        ig = _sigmoid(jnp.dot(xg, wi_ref[g].astype(bf16), preferred_element_type=f32) + bi_ref[:, sl])
        log_a = (-LRU_C) * r * sp[:, sl]
        a = jnp.exp(log_a)
        z = jnp.tanh(-log_a) * (1.0 + a * a)
        mult = jnp.where(z > 0.0, z * lax.rsqrt(z), 0.0)
        a_ref[:, sl] = a
        b_ref[:, sl] = mult * ig * xc[:, sl]

    nchunk = LRU_TT // SUBLANES
    a3 = a_ref[...].reshape(nchunk, SUBLANES, LRU_CB)
    b3 = b_ref[...].reshape(nchunk, SUBLANES, LRU_CB)
    row = lax.broadcasted_iota(i32, a3.shape, 1)
    d = 1
    while d < SUBLANES:
        keep = row >= d
        a_sh = jnp.where(keep, pltpu.roll(a3, d, axis=1), 1.0)
        b_sh = jnp.where(keep, pltpu.roll(b3, d, axis=1), 0.0)
        b3 = a3 * b_sh + b3
        a3 = a3 * a_sh
        d *= 2
    a_ref[...] = a3.reshape(LRU_TT, LRU_CB)
    b_ref[...] = b3.reshape(LRU_TT, LRU_CB)

    def carry(c, hprev):
        r0 = pl.multiple_of(c * SUBLANES, SUBLANES)
        h8 = a_ref[pl.ds(r0, SUBLANES), :] * hprev + b_ref[pl.ds(r0, SUBLANES), :]
        gate = ga_ref[pl.ds(r0, SUBLANES), :]
        o_ref[pl.ds(r0, SUBLANES), :] = (h8 * _silu(gate)).astype(o_ref.dtype)
        return jnp.broadcast_to(h8[SUBLANES - 1:SUBLANES, :], (SUBLANES, LRU_CB))

    hcar_ref[...] = lax.fori_loop(0, nchunk, carry, hcar_ref[...], unroll=8)


def _lru(proj, cw, cb, wr, br, wi, bi, lam, lj, *, batch, seq):
    m = proj.shape[0]
    ne = cw.shape[0]
    nt = seq // LRU_TT
    ncb = RNN_WIDTH // LRU_CB
    gpb = LRU_CB // RNN_BLOCK
    ga_off = OFF_GA // LRU_CB
    row = lambda b, c, t: b * nt + t
    vec = pl.BlockSpec((None, 1, LRU_CB), lambda b, c, t: (lj, 0, c))
    gate_w = pl.BlockSpec((None, gpb, RNN_BLOCK, RNN_BLOCK), lambda b, c, t: (lj, c, 0, 0))
    as_vec = lambda a: a.reshape(ne, 1, RNN_WIDTH)
    return pl.pallas_call(
        _lru_kernel,
        grid=(batch, ncb, nt),
        in_specs=[
            pl.BlockSpec((LRU_TT, LRU_CB), lambda b, c, t: (row(b, c, t), c)),
            pl.BlockSpec((LRU_TT, LRU_CB), lambda b, c, t: (row(b, c, t), ga_off + c)),
            pl.BlockSpec((None, CONV_WIDTH, LRU_CB), lambda b, c, t: (lj, 0, c)),
            vec, gate_w, vec, gate_w, vec, vec,
        ],
        out_specs=pl.BlockSpec((LRU_TT, LRU_CB), lambda b, c, t: (row(b, c, t), c)),
        out_shape=jax.ShapeDtypeStruct((m, RNN_WIDTH), bf16),
        scratch_shapes=[
            pltpu.VMEM((LRU_TT + SUBLANES, LRU_CB), f32),
            pltpu.VMEM((SUBLANES, LRU_CB), f32),
            pltpu.VMEM((LRU_TT, LRU_CB), f32),
            pltpu.VMEM((LRU_TT, LRU_CB), f32),
        ],
        compiler_params=_cparams(("parallel", "parallel", "arbitrary")),
        name="lru",
    )(proj, proj, cw, as_vec(cb), wr, as_vec(br), wi, as_vec(bi), as_vec(lam))


PREP_T = 512


def _head_rms(x, gain):
    ms = jnp.mean(x * x, axis=-1, keepdims=True)
    return x * lax.rsqrt(ms + EPS) * gain


def _prep_kernel(q_ref, k_ref, v_ref, iq_ref, tail_ref, qg_ref, kg_ref,
                 qT_ref, kn_ref, vT_ref, iqT_ref, ik_ref, iwT_ref):
    for h in range(ATT_HEADS):
        sl = slice(h * HEAD_DIM, (h + 1) * HEAD_DIM)
        qT_ref[0, sl, :] = (_head_rms(q_ref[:, sl], qg_ref[...]) * ATT_C2).T.astype(bf16)
    for g in range(KV_HEADS):
        sl = slice(g * HEAD_DIM, (g + 1) * HEAD_DIM)
        kn_ref[:, sl] = _head_rms(k_ref[:, sl], kg_ref[...]).astype(bf16)
        vT_ref[0, sl, :] = v_ref[:, sl].T.astype(bf16)
    for c in range(IDX_HEADS * IDX_DIM // LANES):
        sl = slice(c * LANES, (c + 1) * LANES)
        iqT_ref[0, sl, :] = iq_ref[:, sl].T.astype(bf16)
    tail_t = tail_ref[...].T
    ik_ref[...] = tail_ref[:, 0:IDX_DIM].astype(bf16)
    iwT_ref[0] = tail_t[IDX_DIM:IDX_DIM + IDX_HEADS, :]


def _prep(proj, tail, q_gains, k_gains, lj, *, batch, seq):
    m = proj.shape[0]
    ne = q_gains.shape[0]
    nt = seq // PREP_T
    bt = lambda i: (i // nt, 0, i % nt)
    return pl.pallas_call(
        _prep_kernel,
        grid=(m // PREP_T,),
        in_specs=[
            pl.BlockSpec((PREP_T, ATT_WIDTH), lambda i: (i, OFF_Q // ATT_WIDTH)),
            pl.BlockSpec((PREP_T, KV_WIDTH), lambda i: (i, OFF_K // KV_WIDTH)),
            pl.BlockSpec((PREP_T, KV_WIDTH), lambda i: (i, OFF_V // KV_WIDTH)),
            pl.BlockSpec((PREP_T, IDX_HEADS * IDX_DIM), lambda i: (i, OFF_IQ // (IDX_HEADS * IDX_DIM))),
            pl.BlockSpec((PREP_T, LANES), lambda i: (i, 0)),
            pl.BlockSpec((None, 1, HEAD_DIM), lambda i: (lj, 0, 0)),
            pl.BlockSpec((None, 1, HEAD_DIM), lambda i: (lj, 0, 0)),
        ],
        out_specs=[
            pl.BlockSpec((1, ATT_WIDTH, PREP_T), bt),
            pl.BlockSpec((PREP_T, KV_WIDTH), lambda i: (i, 0)),
            pl.BlockSpec((1, KV_WIDTH, PREP_T), bt),
            pl.BlockSpec((1, IDX_HEADS * IDX_DIM, PREP_T), bt),
            pl.BlockSpec((PREP_T, IDX_DIM), lambda i: (i, 0)),
            pl.BlockSpec((1, IDX_HEADS, PREP_T), bt),
        ],
        out_shape=[
            jax.ShapeDtypeStruct((batch, ATT_WIDTH, seq), bf16),
            jax.ShapeDtypeStruct((m, KV_WIDTH), bf16),
            jax.ShapeDtypeStruct((batch, KV_WIDTH, seq), bf16),
            jax.ShapeDtypeStruct((batch, IDX_HEADS * IDX_DIM, seq), bf16),
            jax.ShapeDtypeStruct((m, IDX_DIM), bf16),
            jax.ShapeDtypeStruct((batch, IDX_HEADS, seq), f32),
        ],
        compiler_params=_cparams(("parallel",)),
        name="attn_prep",
    )(proj, proj, proj, proj, tail, q_gains.reshape(ne, 1, HEAD_DIM), k_gains.reshape(ne, 1, HEAD_DIM))


BAND_ROWS = 3 * KT
FAR_BUCKET = N_BUCKETS // 2 - 1


def _band_kernel(tab_ref, o_ref):
    jj = lax.broadcasted_iota(i32, (BAND_ROWS, QB), 0)
    qi = lax.broadcasted_iota(i32, (BAND_ROWS, QB), 1)
    rel = jnp.where(jj < KT, -2 * KT, jj - 2 * KT - qi)
    n = jnp.abs(rel)
    large = jnp.full(rel.shape, 8, i32)
    for thr in (12, 16, 23, 32, 46, 64, 91):
        large = large + (n >= thr).astype(i32)
    bucket = jnp.where(rel > 0, N_BUCKETS // 2, 0) + jnp.where(n < 8, n, large)

    def per_head(h, carry):
        acc = jnp.zeros((BAND_ROWS, QB), f32)
        for b in range(N_BUCKETS):
            acc = jnp.where(bucket == b, tab_ref[b, h], acc)
        o_ref[h] = (acc - tab_ref[FAR_BUCKET, h]) * LOG2E
        return carry

    lax.fori_loop(0, ATT_HEADS, per_head, 0)


def _band(rel_bias):
    return pl.pallas_call(
        _band_kernel,
        in_specs=[pl.BlockSpec(memory_space=pltpu.SMEM)],
        out_specs=pl.BlockSpec(memory_space=pltpu.VMEM),
        out_shape=jax.ShapeDtypeStruct((ATT_HEADS, BAND_ROWS, QB), f32),
        name="bias_band",
    )(rel_bias)


def _dsa_kernel(qT_ref, k_ref, vT_ref, ik_ref, iqT_ref, iwT_ref, gb_ref, band_ref, o_ref,
                key_ref, madd_ref, eye_ref, bound_ref, *state_refs, topk):
    m_refs = state_refs[:KV_HEADS]
    acc_refs = state_refs[KV_HEADS:]
    i = pl.program_id(1)
    nb = i // TILES_PER_BLOCK + 1
    t0 = i * QB
    qpos = t0 + lax.broadcasted_iota(i32, (1, QB), 1)
    limit = (qpos // CHUNK + 1) * CHUNK
    sub_iota = lax.broadcasted_iota(i32, (KB, QB), 0)
    idx_scale = (IDX_DIM ** -0.5) * (IDX_HEADS ** -0.5)
    int_min = jnp.int32(-2 ** 31)

    def score_block(j, carry):
        r0 = pl.multiple_of(j * KB, KB)
        ks = ik_ref[0, pl.ds(r0, KB), :]
        acc = jnp.zeros((KB, QB), f32)
        for hp in range(IDX_HEADS // 2):
            h0, h1 = 2 * hp, 2 * hp + 1
            w = jnp.concatenate([iqT_ref[0, h0 * IDX_DIM:(h0 + 1) * IDX_DIM, :],
                                 iqT_ref[0, h1 * IDX_DIM:(h1 + 1) * IDX_DIM, :]], axis=1)
            s = jnp.dot(ks, w, preferred_element_type=f32)
            acc = acc + jnp.maximum(s[:, :QB], 0.0) * iwT_ref[0, h0:h0 + 1, :]
            acc = acc + jnp.maximum(s[:, QB:], 0.0) * iwT_ref[0, h1:h1 + 1, :]
        score = acc * idx_scale
        score = jnp.where(score == 0.0, 0.0, score)
        score = jnp.where(r0 + sub_iota < limit, score, -jnp.inf)
        bits = pltpu.bitcast(score, i32)
        key_ref[pl.ds(r0, KB), :] = jnp.where(bits >= 0, bits, bits ^ jnp.int32(0x7FFFFFFF))
        return carry

    lax.fori_loop(0, nb, score_block, 0)

    def count_rows(pred_fn):
        def body(j, cnt8):
            r0 = pl.multiple_of(j * KB, KB)
            hit = pred_fn(key_ref[pl.ds(r0, KB), :], r0).astype(i32)
            return cnt8 + jnp.sum(hit.reshape(KB // SUBLANES, SUBLANES, QB), axis=0)
        cnt8 = lax.fori_loop(0, nb, body, jnp.zeros((SUBLANES, QB), i32))
        return jnp.sum(cnt8, axis=0, keepdims=True)

    n_nonneg = count_rows(lambda kt, r0: kt >= 0)
    nonneg = n_nonneg >= topk
    prefix0 = jnp.where(nonneg, jnp.int32(0), int_min)
    cur0 = jnp.where(nonneg, n_nonneg, nb * KB)
    n_bits = 31
    bits_per_check = 4

    def bits_cond(state):
        b, _, _, more = state
        return jnp.logical_and(b < n_bits, more > 0)

    def bits_body(state):
        b, prefix, cur, _ = state
        for u in range(bits_per_check):
            shift = jnp.maximum(n_bits - 1 - (b + u), 0)
            bit = jnp.where(b + u < n_bits, lax.shift_left(jnp.int32(1), shift), 0)
            cand = prefix | bit
            cnt = count_rows(lambda kt, r0, cand=cand: kt >= cand)
            take = cnt >= topk
            prefix = jnp.where(take, cand, prefix)
            cur = jnp.where(take, cnt, cur)
        more = jnp.max((cur > topk).astype(i32))
        return b + bits_per_check, prefix, cur, more

    more0 = jnp.max((cur0 > topk).astype(i32))
    _, tau, cur, _ = lax.while_loop(bits_cond, bits_body, (jnp.int32(0), prefix0, cur0, more0))

    neg_inf_key = jnp.int32(-2 ** 31 + 0x7FFFFF)
    excess = jnp.logical_and(cur > topk, tau > neg_inf_key)
    any_excess = jnp.max(excess.astype(i32)) > 0
    n_gt = lax.cond(any_excess, lambda: count_rows(lambda kt, r0: kt > tau), lambda: jnp.zeros((1, QB), i32))
    room = topk - n_gt
    n_iter = jnp.where(any_excess, 13, 0)

    def cut_step(b, cut):
        cand = cut | lax.shift_left(jnp.int32(1), 12 - b)
        cnt = count_rows(lambda kt, r0: jnp.logical_and(kt == tau, r0 + sub_iota < cand))
        return jnp.where(cnt <= room, cand, cut)

    cut = lax.fori_loop(0, n_iter, cut_step, jnp.zeros((1, QB), i32))
    cut = jnp.where(excess, cut, jnp.int32(2 ** 30))

    def mask_block(j, carry):
        r0 = pl.multiple_of(j * KB, KB)
        kt = key_ref[pl.ds(r0, KB), :]
        pos = r0 + sub_iota
        sel = jnp.logical_or(kt > tau, jnp.logical_and(kt == tau, pos < cut))
        sel = jnp.logical_and(sel, pos < limit)
        madd_ref[pl.ds(r0, KB), :] = jnp.where(sel, 0.0, NEG_BIG).astype(bf16)
        return carry

    lax.fori_loop(0, nb, mask_block, 0)

    @pl.when(i == 0)
    def _():
        rowi = lax.broadcasted_iota(i32, (QB, REP * QB), 0)
        coli = lax.broadcasted_iota(i32, (QB, REP * QB), 1)
        eye_ref[...] = ((coli & (QB - 1)) == rowi).astype(bf16)
        for g in range(KV_HEADS):
            def kmax_body(j, mx, g=g):
                r0 = pl.multiple_of(j * KB, KB)
                kf = k_ref[0, pl.ds(r0, KB), g * HEAD_DIM:(g + 1) * HEAD_DIM].astype(f32)
                n2 = jnp.sum(kf * kf, axis=-1, keepdims=True)
                return jnp.maximum(mx, jnp.max(n2, axis=0, keepdims=True))
            k2 = lax.fori_loop(0, k_ref.shape[1] // KB, kmax_body, jnp.zeros((1, 1), f32))
            bound_ref[g:g + 1, :] = jnp.broadcast_to(k2, (1, QB))
        bmax = jnp.max(jnp.abs(band_ref[...]).reshape(ATT_HEADS * BAND_ROWS, QB), axis=0, keepdims=True)
        room = FAST_LOGIT_LIMIT - jnp.max(bmax, axis=1, keepdims=True)
        bound_ref[KV_HEADS:KV_HEADS + 1, :] = jnp.broadcast_to(jnp.where(room > 0.0, room * room, -1.0), (1, QB))

    over = jnp.zeros((1, QB), i32)
    for g in range(KV_HEADS):
        q2 = jnp.zeros((1, QB), f32)
        for h in range(g * REP, (g + 1) * REP):
            qh = qT_ref[0, h * HEAD_DIM:(h + 1) * HEAD_DIM, :].astype(f32)
            q2 = jnp.maximum(q2, jnp.sum(qh * qh, axis=0, keepdims=True))
        over = over | (q2 * bound_ref[g:g + 1, :] * 1.002 > bound_ref[KV_HEADS:KV_HEADS + 1, :]).astype(i32)
    needs_shift = jnp.max(over) > 0

    for m_ref, acc_ref in zip(m_refs, acc_refs):
        m_ref[...] = jnp.full(m_ref.shape, NEG_BIG, f32)
        acc_ref[...] = jnp.zeros(acc_ref.shape, f32)

    ones_rows = jnp.ones((SUM_ROWS, KB), bf16)

    def att_block(j, carry, near, online):
        r0 = pl.multiple_of(j * KB, KB)

        def qk(g):
            qg = jnp.concatenate([qT_ref[0, h * HEAD_DIM:(h + 1) * HEAD_DIM, :]
                                  for h in range(g * REP, (g + 1) * REP)], axis=1)
            rhs = jnp.concatenate([qg, eye_ref[...]], axis=0)
            piece = KB // QK_PIECES
            parts = []
            for c in range(QK_PIECES):
                rows = pl.ds(r0 + c * piece, piece)
                lhs = jnp.concatenate([k_ref[0, rows, g * HEAD_DIM:(g + 1) * HEAD_DIM], madd_ref[rows, :]],
                                      axis=1)
                parts.append(jnp.dot(lhs, rhs, preferred_element_type=f32))
            return jnp.concatenate(parts, axis=0)

        ts = {0: qk(0)}
        for g in range(KV_HEADS):
            heads = range(g * REP, (g + 1) * REP)
            if g + 1 < KV_HEADS:
                ts[g + 1] = qk(g + 1)
            t = ts.pop(g)
            if near:
                rows = []
                for u in range(TILES_PER_BLOCK):
                    band_row = pl.multiple_of(jnp.clip(j * TILES_PER_BLOCK + u - i + 2, 0, 2) * KT, KT)
                    rows.append(jnp.concatenate([band_ref[h, pl.ds(band_row, KT), :] for h in heads], axis=1))
                t = t + jnp.concatenate(rows, axis=0)
            vt = jnp.concatenate([vT_ref[0, g * HEAD_DIM:(g + 1) * HEAD_DIM, pl.ds(r0, KB)], ones_rows], axis=0)
            if online:
                m_old = m_refs[g][...]
                m_new = jnp.maximum(m_old, jnp.max(t, axis=0, keepdims=True))
                alpha = jnp.exp2(m_old - m_new)
                p = jnp.exp2(t - m_new)
                m_refs[g][...] = m_new
                acc_refs[g][...] = (acc_refs[g][...] * alpha
                                    + jnp.dot(vt, p.astype(bf16), preferred_element_type=f32))
            else:
                p = jnp.exp2(t)
                acc_refs[g][...] = acc_refs[g][...] + jnp.dot(vt, p.astype(bf16), preferred_element_type=f32)
        return carry

    n_far = jnp.maximum(i - 1, 0) // TILES_PER_BLOCK

    def run_blocks(online):
        lax.fori_loop(0, n_far, functools.partial(att_block, near=False, online=online), 0)
        lax.fori_loop(n_far, nb, functools.partial(att_block, near=True, online=online), 0)

    lax.cond(needs_shift, functools.partial(run_blocks, True), functools.partial(run_blocks, False))

    for g in range(KV_HEADS):
        heads = range(g * REP, (g + 1) * REP)
        oT = acc_refs[g][0:HEAD_DIM, :] / acc_refs[g][HEAD_DIM:HEAD_DIM + 1, :]
        for r, h in enumerate(heads):
            gate = gb_ref[:, h * HEAD_DIM:(h + 1) * HEAD_DIM]
            o = oT[:, r * QB:(r + 1) * QB].T
            o_ref[:, h * HEAD_DIM:(h + 1) * HEAD_DIM] = (o * _silu(gate)).astype(o_ref.dtype)


def _dsa(qT, kn, vT, ik, iqT, iwT, proj, band, *, batch, seq, topk):
    m = proj.shape[0]
    nq = seq // QB
    kn3 = kn.reshape(batch, seq, KV_WIDTH)
    ik3 = ik.reshape(batch, seq, IDX_DIM)
    return pl.pallas_call(
        functools.partial(_dsa_kernel, topk=topk),
        grid=(batch, nq),
        in_specs=[
            pl.BlockSpec((1, ATT_WIDTH, QB), lambda b, i: (b, 0, i)),
            pl.BlockSpec((1, seq, KV_WIDTH), lambda b, i: (b, 0, 0)),
            pl.BlockSpec((1, KV_WIDTH, seq), lambda b, i: (b, 0, 0)),
            pl.BlockSpec((1, seq, IDX_DIM), lambda b, i: (b, 0, 0)),
            pl.BlockSpec((1, IDX_HEADS * IDX_DIM, QB), lambda b, i: (b, 0, i)),
            pl.BlockSpec((1, IDX_HEADS, QB), lambda b, i: (b, 0, i)),
            pl.BlockSpec((QB, ATT_WIDTH), lambda b, i: (b * nq + i, OFF_GB // ATT_WIDTH)),
            pl.BlockSpec((ATT_HEADS, BAND_ROWS, QB), lambda b, i: (0, 0, 0)),
        ],
        out_specs=pl.BlockSpec((QB, ATT_WIDTH), lambda b, i: (b * nq + i, 0)),
        out_shape=jax.ShapeDtypeStruct((m, ATT_WIDTH), bf16),
        scratch_shapes=[
            pltpu.VMEM((seq, QB), i32),
            pltpu.VMEM((seq, QB), bf16),
            pltpu.VMEM((QB, REP * QB), bf16),
            pltpu.VMEM((SUBLANES, QB), f32),
        ] + [pltpu.VMEM((1, REP * QB), f32)] * KV_HEADS
          + [pltpu.VMEM((HEAD_DIM + SUM_ROWS, REP * QB), f32)] * KV_HEADS,
        compiler_params=_cparams(("parallel", "arbitrary")),
        name="dsa",
    )(qT, kn3, vT, ik3, iqT, iwT, proj, band)


def _gelu(x):
    return 0.5 * x * (1.0 + jnp.tanh(math.sqrt(2.0 / math.pi) * (x + 0.044715 * (x * x * x))))


def _sgu_kernel(u_ref, v_ref, g_ref, lng_ref, lnb_ref, ws_ref, bsg_ref, o_ref, wsm_ref, bs_ref):
    @pl.when(pl.program_id(0) == 0)
    def _():
        tpos = lax.broadcasted_iota(i32, (SG_CHUNK, SG_CHUNK), 0) // CHUNK
        spos = lax.broadcasted_iota(i32, (SG_CHUNK, SG_CHUNK), 1) // CHUNK
        for g in range(SG_GROUPS):
            wsm_ref[g] = jnp.where(tpos >= spos, ws_ref[g], 0.0).astype(bf16)
        bs_ref[...] = bsg_ref[...].T

    v = _gelu(v_ref[...])
    mu = jnp.mean(v, axis=-1, keepdims=True)
    vc = v - mu
    var = jnp.mean(vc * vc, axis=-1, keepdims=True)
    vn = (vc * lax.rsqrt(var + EPS) * lng_ref[...] + lnb_ref[...]).astype(bf16)
    for g in range(SG_GROUPS):
        sl = slice(g * SG_GROUP, (g + 1) * SG_GROUP)
        mixed = jnp.dot(wsm_ref[g], vn[:, sl], preferred_element_type=f32) + bs_ref[:, g:g + 1]
        gate = g_ref[:, sl]
        o_ref[:, sl] = (_gelu(u_ref[:, sl]) * mixed * _silu(gate)).astype(o_ref.dtype)


def _sgu(proj, ln_g, ln_b, w_s, b_s, lj):
    m = proj.shape[0]
    no = ln_g.shape[0]
    return pl.pallas_call(
        _sgu_kernel,
        grid=(m // SG_CHUNK,),
        in_specs=[
            pl.BlockSpec((SG_CHUNK, SG_WIDTH), lambda i: (i, 0)),
            pl.BlockSpec((SG_CHUNK, SG_WIDTH), lambda i: (i, 1)),
            pl.BlockSpec((SG_CHUNK, SG_WIDTH), lambda i: (i, 2)),
            pl.BlockSpec((None, 1, SG_WIDTH), lambda i: (lj, 0, 0)),
            pl.BlockSpec((None, 1, SG_WIDTH), lambda i: (lj, 0, 0)),
            pl.BlockSpec((None, SG_GROUPS, SG_CHUNK, SG_CHUNK), lambda i: (lj, 0, 0, 0)),
            pl.BlockSpec((None, SG_GROUPS, SG_CHUNK), lambda i: (lj, 0, 0)),
        ],
        out_specs=pl.BlockSpec((SG_CHUNK, SG_WIDTH), lambda i: (i, 0)),
        out_shape=jax.ShapeDtypeStruct((m, SG_WIDTH), bf16),
        scratch_shapes=[pltpu.VMEM((SG_GROUPS, SG_CHUNK, SG_CHUNK), bf16), pltpu.VMEM((SG_CHUNK, SG_GROUPS), f32)],
        compiler_params=_cparams(("arbitrary",)),
        name="sgu",
    )(proj, proj, proj, ln_g.reshape(no, 1, SG_WIDTH), ln_b.reshape(no, 1, SG_WIDTH), w_s, b_s)


PROJ_TN = 1024
EVEN_COL_BLOCKS = (0, 1, 2, 3, 4, 5, 7, 8, 6, 9)
ODD_COL_BLOCKS = tuple(range(3 * SG_WIDTH // PROJ_TN))
TAIL_START = 10240


def kernel(x, p, norm_gain, rel_bias, even_w_in, conv_w, conv_b, lru_w_r, lru_b_r, lru_w_i, lru_b_i,
           lru_lambda, q_norm, k_norm, even_w_out, odd_w_in, sg_ln_g, sg_ln_b, sg_w_s, sg_b_s, odd_w_out,
           pe_w, pe_gate_norm, pe_w_gate):
    batch, seq, d = x.shape
    depth = p.shape[0]
    topk = min(TOPK_MAX, seq // 4)
    m = batch * seq
    h = x.reshape(m, d)
    band = _band(rel_bias)
    hn = _norm(h, norm_gain, 0)
    even_w_in_t = jnp.swapaxes(even_w_in, 1, 2)
    even_wo_b, odd_wo_b = _cast_bf16(even_w_out), _cast_bf16(odd_w_out)
    wg_b, pew_b = _cast_bf16(pe_w_gate), _cast_bf16(pe_w)
    for layer in range(depth):
        j = layer // 2
        if layer % 2 == 0:
            proj = _proj(hn, even_w_in_t, j, EVEN_COL_BLOCKS, transposed=True, tn=PROJ_TN)
            tail = _proj_tail(hn, even_w_in_t, j, TAIL_START)
            ya = _lru(proj, conv_w, conv_b, lru_w_r, lru_b_r, lru_w_i, lru_b_i, lru_lambda, j,
                      batch=batch, seq=seq)
            qT, kn, vT, iqT, ik, iwT = _prep(proj, tail, q_norm, k_norm, j, batch=batch, seq=seq)
            yb = _dsa(qT, kn, vT, ik, iqT, iwT, proj, band, batch=batch, seq=seq, topk=topk)
            ys, wo_b = [ya, yb], even_wo_b
        else:
            proj = _proj(hn, odd_w_in, j, ODD_COL_BLOCKS, tn=PROJ_TN)
            ys, wo_b = [_sgu(proj, sg_ln_g, sg_ln_b, sg_w_s, sg_b_s, j)], odd_wo_b
        nxt = (norm_gain, layer + 1) if layer + 1 < depth else (None, None)
        h, hn = _tail(h, ys, wo_b, j, pe_gate_norm, wg_b, p, pew_b, layer, *nxt)
    return h.reshape(batch, seq, d)
```

```python
import functools
import math

import jax
import jax.numpy as jnp
from jax import lax
from jax.experimental import pallas as pl
from jax.experimental.pallas import tpu as pltpu

f32 = jnp.float32
bf16 = jnp.bfloat16
i32 = jnp.int32

D_MODEL = 2048
CHUNK = 64
PE_DIM = 256
EPS = 1e-6
RNN_WIDTH = 2048
RNN_BLOCK = 128
CONV_WIDTH = 4
LRU_C = 8.0
ATT_HEADS = 16
HEAD_DIM = 128
KV_HEADS = 4
REP = ATT_HEADS // KV_HEADS
ATT_WIDTH = 2048
KV_WIDTH = 512
IDX_HEADS = 16
IDX_DIM = 64
TOPK_MAX = 256
N_BUCKETS = 32
SG_CHUNK = 128
SG_GROUPS = 16
SG_GROUP = 128
SG_WIDTH = 2048

LANES = 128
SUBLANES = 8
QB = 128
KT = 128
KB = 512
TILES_PER_BLOCK = KB // KT
QK_PIECES = 4
SUM_ROWS = 16
LOG2E = math.log2(math.e)
ATT_C2 = HEAD_DIM ** -0.5 * LOG2E
VMEM_LIMIT = 52 * 1024 * 1024
NEG_BIG = -1e30
FAST_LOGIT_LIMIT = 60.0

OFF_XA, OFF_GA, OFF_GB = 0, 2048, 4096
OFF_Q, OFF_K, OFF_V, OFF_IQ = 0, 2048, 2560, 3072


def _cparams(sem):
    return pltpu.CompilerParams(dimension_semantics=sem, vmem_limit_bytes=VMEM_LIMIT)


def _sigmoid(x):
    return 0.5 * jnp.tanh(0.5 * x) + 0.5


def _silu(x):
    return x * _sigmoid(x)


def _block_lookup(table):
    def f(j):
        out = jnp.int32(table[-1])
        for idx in range(len(table) - 2, -1, -1):
            out = jnp.where(j == idx, table[idx], out)
        return out
    return f


def _rms(x, gain):
    ms = jnp.mean(x * x, axis=-1, keepdims=True)
    return x * lax.rsqrt(ms + EPS) * gain


def _norm_kernel(h_ref, g_ref, o_ref):
    o_ref[...] = _rms(h_ref[...], g_ref[...]).astype(o_ref.dtype)


def _norm(h, gains, layer, *, tm=512):
    m, d = h.shape
    return pl.pallas_call(
        _norm_kernel,
        grid=(m // tm,),
        in_specs=[pl.BlockSpec((tm, d), lambda i: (i, 0)),
                  pl.BlockSpec((None, 1, d), lambda i: (layer, 0, 0))],
        out_specs=pl.BlockSpec((tm, d), lambda i: (i, 0)),
        out_shape=jax.ShapeDtypeStruct((m, d), bf16),
        compiler_params=_cparams(("parallel",)),
        name="norm",
    )(h, gains.reshape(gains.shape[0], 1, d))


_NT = (((1,), (1,)), ((), ()))


def _proj_kernel(hn_ref, w_ref, o_ref, wb_ref, *, transposed):
    @pl.when(pl.program_id(1) == 0)
    def _():
        wb_ref[...] = w_ref[...].astype(bf16)

    if transposed:
        acc = lax.dot_general(hn_ref[...], wb_ref[...], _NT, preferred_element_type=f32)
    else:
        acc = jnp.dot(hn_ref[...], wb_ref[...], preferred_element_type=f32)
    o_ref[...] = acc.astype(o_ref.dtype)


def _proj(hn, ws, widx, col_blocks, *, transposed=False, tm=1024, tn=1024, out_dtype=f32):
    m, d = hn.shape
    n = len(col_blocks) * tn
    colmap = _block_lookup(col_blocks)
    if transposed:
        w_spec = pl.BlockSpec((None, tn, d), lambda j, i: (widx, colmap(j), 0))
        wb_shape = (tn, d)
    else:
        w_spec = pl.BlockSpec((None, d, tn), lambda j, i: (widx, 0, colmap(j)))
        wb_shape = (d, tn)
    return pl.pallas_call(
        functools.partial(_proj_kernel, transposed=transposed),
        grid=(n // tn, m // tm),
        in_specs=[pl.BlockSpec((tm, d), lambda j, i: (i, 0)), w_spec],
        out_specs=pl.BlockSpec((tm, tn), lambda j, i: (i, j)),
        out_shape=jax.ShapeDtypeStruct((m, n), out_dtype),
        scratch_shapes=[pltpu.VMEM(wb_shape, bf16)],
        compiler_params=_cparams(("parallel", "arbitrary")),
        name="proj",
    )(hn, ws)


def _proj_tail_kernel(hn_ref, wt_ref, o_ref, *, tail_rows):
    row = lax.broadcasted_iota(i32, wt_ref.shape, 0)
    wt = jnp.where(row < tail_rows, wt_ref[...], 0.0).astype(bf16)
    o_ref[...] = lax.dot_general(hn_ref[...], wt, _NT, preferred_element_type=f32)


def _proj_tail(hn, wts, widx, tail_start, *, tm=1024):
    m, d = hn.shape
    tail_rows = wts.shape[1] - tail_start
    assert 0 < tail_rows < LANES and tail_start % LANES == 0
    return pl.pallas_call(
        functools.partial(_proj_tail_kernel, tail_rows=tail_rows),
        grid=(m // tm,),
        in_specs=[
            pl.BlockSpec((tm, d), lambda i: (i, 0)),
            pl.BlockSpec((None, LANES, d), lambda i: (widx, tail_start // LANES, 0)),
        ],
        out_specs=pl.BlockSpec((tm, LANES), lambda i: (i, 0)),
        out_shape=jax.ShapeDtypeStruct((m, LANES), f32),
        compiler_params=_cparams(("parallel",)),
        name="proj_tail",
    )(hn, wts)


def _cast_kernel(w_ref, o_ref):
    o_ref[...] = w_ref[...].astype(o_ref.dtype)


def _cast_bf16(w, *, tr=512):
    lead, rows, cols = w.shape
    tr = min(tr, rows)
    return pl.pallas_call(
        _cast_kernel,
        grid=(lead, rows // tr),
        in_specs=[pl.BlockSpec((None, tr, cols), lambda a, r: (a, r, 0))],
        out_specs=pl.BlockSpec((None, tr, cols), lambda a, r: (a, r, 0)),
        out_shape=jax.ShapeDtypeStruct(w.shape, bf16),
        compiler_params=_cparams(("parallel", "parallel")),
        name="cast_bf16",
    )(w)


def _outproj_kernel(*refs, n_y):
    h_ref = refs[0]
    y_refs = refs[1:1 + n_y]
    w_refs = refs[1 + n_y:1 + 2 * n_y]
    o_ref = refs[1 + 2 * n_y]
    wb_refs = refs[2 + 2 * n_y:]

    @pl.when(pl.program_id(1) == 0)
    def _():
        for w_ref, wb_ref in zip(w_refs, wb_refs):
            wb_ref[...] = w_ref[...].astype(bf16)

    acc = h_ref[...]
    for y_ref, wb_ref in zip(y_refs, wb_refs):
        acc = acc + jnp.dot(y_ref[...], wb_ref[...], preferred_element_type=f32)
    o_ref[...] = acc


def _outproj(h, ys, ws, widx, *, tm=1024, tn=512):
    m, d = h.shape
    n_y = len(ys)
    kw = ys[0].shape[1]
    in_specs = [pl.BlockSpec((tm, tn), lambda j, i: (i, j))]
    in_specs += [pl.BlockSpec((tm, kw), lambda j, i: (i, 0)) for _ in ys]
    in_specs += [pl.BlockSpec((None, kw, tn), lambda j, i, c=c: (widx, c, j)) for c in range(n_y)]
    return pl.pallas_call(
        functools.partial(_outproj_kernel, n_y=n_y),
        grid=(d // tn, m // tm),
        in_specs=in_specs,
        out_specs=pl.BlockSpec((tm, tn), lambda j, i: (i, j)),
        out_shape=jax.ShapeDtypeStruct((m, d), f32),
        scratch_shapes=[pltpu.VMEM((kw, tn), bf16)] * n_y,
        compiler_params=_cparams(("parallel", "arbitrary")),
        name="outproj",
    )(h, *ys, *([ws] * n_y))


def _pe_kernel(h_ref, gn_ref, wg_ref, p_ref, pew_ref, *rest, tn, emit_next):
    if emit_next:
        gnext_ref, o_ref, hnext_ref, hn_ref, pb_ref = rest
    else:
        o_ref, hn_ref, pb_ref = rest
    j = pl.program_id(1)

    @pl.when(j == 0)
    def _():
        hn_ref[...] = _rms(h_ref[...], gn_ref[...]).astype(bf16)
        pb_ref[...] = p_ref[...].astype(bf16)

    cols = pl.ds(pl.multiple_of(j * tn, tn), tn)
    gate = _sigmoid(jnp.dot(hn_ref[...], wg_ref[...], preferred_element_type=f32))
    e = jnp.dot(pb_ref[...], pew_ref[...], preferred_element_type=f32)
    o_ref[:, cols] = h_ref[:, cols] + gate * e

    if emit_next:
        @pl.when(j == pl.num_programs(1) - 1)
        def _():
            hnext_ref[...] = _rms(o_ref[...], gnext_ref[...]).astype(bf16)


def _pe(h, gns, wgs, p, pews, layer, next_gains=None, next_layer=None, *, tm=512, tn=512):
    m, d = h.shape
    depth = p.shape[0]
    emit_next = next_gains is not None
    in_specs = [
        pl.BlockSpec((tm, d), lambda i, j: (i, 0)),
        pl.BlockSpec((None, 1, d), lambda i, j: (layer, 0, 0)),
        pl.BlockSpec((None, d, tn), lambda i, j: (layer, 0, j)),
        pl.BlockSpec((None, tm, PE_DIM), lambda i, j: (layer, i, 0)),
        pl.BlockSpec((None, PE_DIM, tn), lambda i, j: (layer, 0, j)),
    ]
    args = [h, gns.reshape(depth, 1, d), wgs, p.reshape(depth, m, PE_DIM), pews]
    out_specs = [pl.BlockSpec((tm, d), lambda i, j: (i, 0))]
    out_shape = [jax.ShapeDtypeStruct((m, d), f32)]
    if emit_next:
        in_specs.append(pl.BlockSpec((None, 1, d), lambda i, j: (next_layer, 0, 0)))
        args.append(next_gains.reshape(next_gains.shape[0], 1, d))
        out_specs.append(pl.BlockSpec((tm, d), lambda i, j: (i, 0)))
        out_shape.append(jax.ShapeDtypeStruct((m, d), bf16))
    res = pl.pallas_call(
        functools.partial(_pe_kernel, tn=tn, emit_next=emit_next),
        grid=(m // tm, d // tn),
        in_specs=in_specs,
        out_specs=out_specs,
        out_shape=out_shape,
        scratch_shapes=[pltpu.VMEM((tm, d), bf16), pltpu.VMEM((tm, PE_DIM), bf16)],
        compiler_params=_cparams(("parallel", "arbitrary")),
        name="pe",
    )(*args)
    return res if emit_next else (res[0], None)


LRU_TT = 512
LRU_CB = 512


def _lru_kernel(x_ref, ga_ref, cw_ref, cb_ref, wr_ref, br_ref, wi_ref, bi_ref, lam_ref, o_ref,
                xext_ref, hcar_ref, a_ref, b_ref):
    tstep = pl.program_id(2)
    pad = SUBLANES

    @pl.when(tstep == 0)
    def _():
        xext_ref[0:pad, :] = jnp.zeros((pad, LRU_CB), f32)
        hcar_ref[...] = jnp.zeros_like(hcar_ref)

    @pl.when(tstep > 0)
    def _():
        xext_ref[0:pad, :] = xext_ref[LRU_TT:LRU_TT + pad, :]

    xext_ref[pad:, :] = x_ref[...]

    xc = cb_ref[...] + cw_ref[CONV_WIDTH - 1:CONV_WIDTH, :] * x_ref[...]
    for j in range(CONV_WIDTH - 1):
        sh = CONV_WIDTH - 1 - j
        xc = xc + cw_ref[j:j + 1, :] * xext_ref[pad - sh:pad - sh + LRU_TT, :]

    xcb = xc.astype(bf16)
    sp = jax.nn.softplus(-lam_ref[...])
    nb = LRU_CB // RNN_BLOCK
    for g in range(nb):
        sl = slice(g * RNN_BLOCK, (g + 1) * RNN_BLOCK)
        xg = xcb[:, sl]
        r = _sigmoid(jnp.dot(xg, wr_ref[g].astype(bf16), preferred_element_type=f32) + br_ref[:, sl])
        ig = _sigmoid(jnp.dot(xg, wi_ref[g].astype(bf16), preferred_element_type=f32) + bi_ref[:, sl])
        log_a = (-LRU_C) * r * sp[:, sl]
        a = jnp.exp(log_a)
        z = jnp.tanh(-log_a) * (1.0 + a * a)
        mult = jnp.where(z > 0.0, z * lax.rsqrt(z), 0.0)
        a_ref[:, sl] = a
        b_ref[:, sl] = mult * ig * xc[:, sl]

    nchunk = LRU_TT // SUBLANES
    a3 = a_ref[...].reshape(nchunk, SUBLANES, LRU_CB)
    b3 = b_ref[...].reshape(nchunk, SUBLANES, LRU_CB)
    row = lax.broadcasted_iota(i32, a3.shape, 1)
    d = 1
    while d < SUBLANES:
        keep = row >= d
        a_sh = jnp.where(keep, pltpu.roll(a3, d, axis=1), 1.0)
        b_sh = jnp.where(keep, pltpu.roll(b3, d, axis=1), 0.0)
        b3 = a3 * b_sh + b3
        a3 = a3 * a_sh
        d *= 2
    a_ref[...] = a3.reshape(LRU_TT, LRU_CB)
    b_ref[...] = b3.reshape(LRU_TT, LRU_CB)

    def carry(c, hprev):
        r0 = pl.multiple_of(c * SUBLANES, SUBLANES)
        h8 = a_ref[pl.ds(r0, SUBLANES), :] * hprev + b_ref[pl.ds(r0, SUBLANES), :]
        gate = ga_ref[pl.ds(r0, SUBLANES), :]
        o_ref[pl.ds(r0, SUBLANES), :] = (h8 * _silu(gate)).astype(o_ref.dtype)
        return jnp.broadcast_to(h8[SUBLANES - 1:SUBLANES, :], (SUBLANES, LRU_CB))

    hcar_ref[...] = lax.fori_loop(0, nchunk, carry, hcar_ref[...], unroll=8)


def _lru(proj, cw, cb, wr, br, wi, bi, lam, lj, *, batch, seq):
    m = proj.shape[0]
    ne = cw.shape[0]
    nt = seq // LRU_TT
    ncb = RNN_WIDTH // LRU_CB
    gpb = LRU_CB // RNN_BLOCK
    ga_off = OFF_GA // LRU_CB
    row = lambda b, c, t: b * nt + t
    vec = pl.BlockSpec((None, 1, LRU_CB), lambda b, c, t: (lj, 0, c))
    gate_w = pl.BlockSpec((None, gpb, RNN_BLOCK, RNN_BLOCK), lambda b, c, t: (lj, c, 0, 0))
    as_vec = lambda a: a.reshape(ne, 1, RNN_WIDTH)
    return pl.pallas_call(
        _lru_kernel,
        grid=(batch, ncb, nt),
        in_specs=[
            pl.BlockSpec((LRU_TT, LRU_CB), lambda b, c, t: (row(b, c, t), c)),
            pl.BlockSpec((LRU_TT, LRU_CB), lambda b, c, t: (row(b, c, t), ga_off + c)),
            pl.BlockSpec((None, CONV_WIDTH, LRU_CB), lambda b, c, t: (lj, 0, c)),
            vec, gate_w, vec, gate_w, vec, vec,
        ],
        out_specs=pl.BlockSpec((LRU_TT, LRU_CB), lambda b, c, t: (row(b, c, t), c)),
        out_shape=jax.ShapeDtypeStruct((m, RNN_WIDTH), bf16),
        scratch_shapes=[
            pltpu.VMEM((LRU_TT + SUBLANES, LRU_CB), f32),
            pltpu.VMEM((SUBLANES, LRU_CB), f32),
            pltpu.VMEM((LRU_TT, LRU_CB), f32),
            pltpu.VMEM((LRU_TT, LRU_CB), f32),
        ],
        compiler_params=_cparams(("parallel", "parallel", "arbitrary")),
        name="lru",
    )(proj, proj, cw, as_vec(cb), wr, as_vec(br), wi, as_vec(bi), as_vec(lam))


PREP_T = 512


def _head_rms(x, gain):
    ms = jnp.mean(x * x, axis=-1, keepdims=True)
    return x * lax.rsqrt(ms + EPS) * gain


def _prep_kernel(q_ref, k_ref, v_ref, iq_ref, tail_ref, qg_ref, kg_ref,
                 qT_ref, kn_ref, vT_ref, iqT_ref, ik_ref, iwT_ref):
    for h in range(ATT_HEADS):
        sl = slice(h * HEAD_DIM, (h + 1) * HEAD_DIM)
        qT_ref[0, sl, :] = (_head_rms(q_ref[:, sl].astype(f32), qg_ref[...]) * ATT_C2).T.astype(bf16)
    for g in range(KV_HEADS):
        sl = slice(g * HEAD_DIM, (g + 1) * HEAD_DIM)
        kn_ref[:, sl] = _head_rms(k_ref[:, sl].astype(f32), kg_ref[...]).astype(bf16)
        vT_ref[0, sl, :] = v_ref[:, sl].T
    for c in range(IDX_HEADS * IDX_DIM // LANES):
        sl = slice(c * LANES, (c + 1) * LANES)
        iqT_ref[0, sl, :] = iq_ref[:, sl].T
    tail_t = tail_ref[...].T
    ik_ref[...] = tail_ref[:, 0:IDX_DIM].astype(bf16)
    iwT_ref[0] = tail_t[IDX_DIM:IDX_DIM + IDX_HEADS, :]


def _prep(proj, tail, q_gains, k_gains, lj, *, batch, seq):
    m = proj.shape[0]
    ne = q_gains.shape[0]
    nt = seq // PREP_T
    bt = lambda i: (i // nt, 0, i % nt)
    return pl.pallas_call(
        _prep_kernel,
        grid=(m // PREP_T,),
        in_specs=[
            pl.BlockSpec((PREP_T, ATT_WIDTH), lambda i: (i, OFF_Q // ATT_WIDTH)),
            pl.BlockSpec((PREP_T, KV_WIDTH), lambda i: (i, OFF_K // KV_WIDTH)),
            pl.BlockSpec((PREP_T, KV_WIDTH), lambda i: (i, OFF_V // KV_WIDTH)),
            pl.BlockSpec((PREP_T, IDX_HEADS * IDX_DIM), lambda i: (i, OFF_IQ // (IDX_HEADS * IDX_DIM))),
            pl.BlockSpec((PREP_T, LANES), lambda i: (i, 0)),
            pl.BlockSpec((None, 1, HEAD_DIM), lambda i: (lj, 0, 0)),
            pl.BlockSpec((None, 1, HEAD_DIM), lambda i: (lj, 0, 0)),
        ],
        out_specs=[
            pl.BlockSpec((1, ATT_WIDTH, PREP_T), bt),
            pl.BlockSpec((PREP_T, KV_WIDTH), lambda i: (i, 0)),
            pl.BlockSpec((1, KV_WIDTH, PREP_T), bt),
            pl.BlockSpec((1, IDX_HEADS * IDX_DIM, PREP_T), bt),
            pl.BlockSpec((PREP_T, IDX_DIM), lambda i: (i, 0)),
            pl.BlockSpec((1, IDX_HEADS, PREP_T), bt),
        ],
        out_shape=[
            jax.ShapeDtypeStruct((batch, ATT_WIDTH, seq), bf16),
            jax.ShapeDtypeStruct((m, KV_WIDTH), bf16),
            jax.ShapeDtypeStruct((batch, KV_WIDTH, seq), bf16),
            jax.ShapeDtypeStruct((batch, IDX_HEADS * IDX_DIM, seq), bf16),
            jax.ShapeDtypeStruct((m, IDX_DIM), bf16),
            jax.ShapeDtypeStruct((batch, IDX_HEADS, seq), f32),
        ],
        compiler_params=_cparams(("parallel",)),
        name="attn_prep",
    )(proj, proj, proj, proj, tail, q_gains.reshape(ne, 1, HEAD_DIM), k_gains.reshape(ne, 1, HEAD_DIM))


BAND_ROWS = 3 * KT
FAR_BUCKET = N_BUCKETS // 2 - 1


def _band_kernel(tab_ref, o_ref):
    jj = lax.broadcasted_iota(i32, (BAND_ROWS, QB), 0)
    qi = lax.broadcasted_iota(i32, (BAND_ROWS, QB), 1)
    rel = jnp.where(jj < KT, -2 * KT, jj - 2 * KT - qi)
    n = jnp.abs(rel)
    large = jnp.full(rel.shape, 8, i32)
    for thr in (12, 16, 23, 32, 46, 64, 91):
        large = large + (n >= thr).astype(i32)
    bucket = jnp.where(rel > 0, N_BUCKETS // 2, 0) + jnp.where(n < 8, n, large)

    def per_head(h, carry):
        acc = jnp.zeros((BAND_ROWS, QB), f32)
        for b in range(N_BUCKETS):
            acc = jnp.where(bucket == b, tab_ref[b, h], acc)
        o_ref[h] = (acc - tab_ref[FAR_BUCKET, h]) * LOG2E
        return carry

    lax.fori_loop(0, ATT_HEADS, per_head, 0)


def _band(rel_bias):
    return pl.pallas_call(
        _band_kernel,
        in_specs=[pl.BlockSpec(memory_space=pltpu.SMEM)],
        out_specs=pl.BlockSpec(memory_space=pltpu.VMEM),
        out_shape=jax.ShapeDtypeStruct((ATT_HEADS, BAND_ROWS, QB), f32),
        name="bias_band",
    )(rel_bias)


def _dsa_kernel(qT_ref, k_ref, vT_ref, ik_ref, iqT_ref, iwT_ref, gb_ref, band_ref, o_ref,
                key_ref, madd_ref, eye_ref, bound_ref, *state_refs, topk):
    m_refs = state_refs[:KV_HEADS]
    acc_refs = state_refs[KV_HEADS:]
    i = pl.program_id(1)
    nb = i // TILES_PER_BLOCK + 1
    t0 = i * QB
    qpos = t0 + lax.broadcasted_iota(i32, (1, QB), 1)
    limit = (qpos // CHUNK + 1) * CHUNK
    sub_iota = lax.broadcasted_iota(i32, (KB, QB), 0)
    idx_scale = (IDX_DIM ** -0.5) * (IDX_HEADS ** -0.5)
    int_min = jnp.int32(-2 ** 31)

    def score_block(j, carry):
        r0 = pl.multiple_of(j * KB, KB)
        ks = ik_ref[0, pl.ds(r0, KB), :]
        acc = jnp.zeros((KB, QB), f32)
        for hp in range(IDX_HEADS // 2):
            h0, h1 = 2 * hp, 2 * hp + 1
            w = jnp.concatenate([iqT_ref[0, h0 * IDX_DIM:(h0 + 1) * IDX_DIM, :],
                                 iqT_ref[0, h1 * IDX_DIM:(h1 + 1) * IDX_DIM, :]], axis=1)
            s = jnp.dot(ks, w, preferred_element_type=f32)
            acc = acc + jnp.maximum(s[:, :QB], 0.0) * iwT_ref[0, h0:h0 + 1, :]
            acc = acc + jnp.maximum(s[:, QB:], 0.0) * iwT_ref[0, h1:h1 + 1, :]
        score = acc * idx_scale
        score = jnp.where(score == 0.0, 0.0, score)
        score = jnp.where(r0 + sub_iota < limit, score, -jnp.inf)
        bits = pltpu.bitcast(score, i32)
        key_ref[pl.ds(r0, KB), :] = jnp.where(bits >= 0, bits, bits ^ jnp.int32(0x7FFFFFFF))
        return carry

    lax.fori_loop(0, nb, score_block, 0)

    def count_rows(pred_fn):
        def body(j, cnt8):
            r0 = pl.multiple_of(j * KB, KB)
            hit = pred_fn(key_ref[pl.ds(r0, KB), :], r0).astype(i32)
            return cnt8 + jnp.sum(hit.reshape(KB // SUBLANES, SUBLANES, QB), axis=0)
        cnt8 = lax.fori_loop(0, nb, body, jnp.zeros((SUBLANES, QB), i32))
        return jnp.sum(cnt8, axis=0, keepdims=True)

    n_nonneg = count_rows(lambda kt, r0: kt >= 0)
    nonneg = n_nonneg >= topk
    prefix0 = jnp.where(nonneg, jnp.int32(0), int_min)
    cur0 = jnp.where(nonneg, n_nonneg, nb * KB)
    n_bits = 31
    bits_per_check = 4

    def bits_cond(state):
        b, _, _, more = state
        return jnp.logical_and(b < n_bits, more > 0)

    def bits_body(state):
        b, prefix, cur, _ = state
        for u in range(bits_per_check):
            shift = jnp.maximum(n_bits - 1 - (b + u), 0)
            bit = jnp.where(b + u < n_bits, lax.shift_left(jnp.int32(1), shift), 0)
            cand = prefix | bit
            cnt = count_rows(lambda kt, r0, cand=cand: kt >= cand)
            take = cnt >= topk
            prefix = jnp.where(take, cand, prefix)
            cur = jnp.where(take, cnt, cur)
        more = jnp.max((cur > topk).astype(i32))
        return b + bits_per_check, prefix, cur, more

    more0 = jnp.max((cur0 > topk).astype(i32))
    _, tau, cur, _ = lax.while_loop(bits_cond, bits_body, (jnp.int32(0), prefix0, cur0, more0))

    neg_inf_key = jnp.int32(-2 ** 31 + 0x7FFFFF)
    excess = jnp.logical_and(cur > topk, tau > neg_inf_key)
    any_excess = jnp.max(excess.astype(i32)) > 0
    n_gt = lax.cond(any_excess, lambda: count_rows(lambda kt, r0: kt > tau), lambda: jnp.zeros((1, QB), i32))
    room = topk - n_gt
    n_iter = jnp.where(any_excess, 13, 0)

    def cut_step(b, cut):
        cand = cut | lax.shift_left(jnp.int32(1), 12 - b)
        cnt = count_rows(lambda kt, r0: jnp.logical_and(kt == tau, r0 + sub_iota < cand))
        return jnp.where(cnt <= room, cand, cut)

    cut = lax.fori_loop(0, n_iter, cut_step, jnp.zeros((1, QB), i32))
    cut = jnp.where(excess, cut, jnp.int32(2 ** 30))

    def mask_block(j, carry):
        r0 = pl.multiple_of(j * KB, KB)
        kt = key_ref[pl.ds(r0, KB), :]
        pos = r0 + sub_iota
        sel = jnp.logical_or(kt > tau, jnp.logical_and(kt == tau, pos < cut))
        sel = jnp.logical_and(sel, pos < limit)
        madd_ref[pl.ds(r0, KB), :] = jnp.where(sel, 0.0, NEG_BIG).astype(bf16)
        return carry

    lax.fori_loop(0, nb, mask_block, 0)

    @pl.when(i == 0)
    def _():
        rowi = lax.broadcasted_iota(i32, (QB, REP * QB), 0)
        coli = lax.broadcasted_iota(i32, (QB, REP * QB), 1)
        eye_ref[...] = ((coli & (QB - 1)) == rowi).astype(bf16)
        for g in range(KV_HEADS):
            def kmax_body(j, mx, g=g):
                r0 = pl.multiple_of(j * KB, KB)
                kf = k_ref[0, pl.ds(r0, KB), g * HEAD_DIM:(g + 1) * HEAD_DIM].astype(f32)
                n2 = jnp.sum(kf * kf, axis=-1, keepdims=True)
                return jnp.maximum(mx, jnp.max(n2, axis=0, keepdims=True))
            k2 = lax.fori_loop(0, k_ref.shape[1] // KB, kmax_body, jnp.zeros((1, 1), f32))
            bound_ref[g:g + 1, :] = jnp.broadcast_to(k2, (1, QB))
        bmax = jnp.max(jnp.abs(band_ref[...]).reshape(ATT_HEADS * BAND_ROWS, QB), axis=0, keepdims=True)
        room = FAST_LOGIT_LIMIT - jnp.max(bmax, axis=1, keepdims=True)
        bound_ref[KV_HEADS:KV_HEADS + 1, :] = jnp.broadcast_to(jnp.where(room > 0.0, room * room, -1.0), (1, QB))

    over = jnp.zeros((1, QB), i32)
    for g in range(KV_HEADS):
        q2 = jnp.zeros((1, QB), f32)
        for h in range(g * REP, (g + 1) * REP):
            qh = qT_ref[0, h * HEAD_DIM:(h + 1) * HEAD_DIM, :].astype(f32)
            q2 = jnp.maximum(q2, jnp.sum(qh * qh, axis=0, keepdims=True))
        over = over | (q2 * bound_ref[g:g + 1, :] * 1.002 > bound_ref[KV_HEADS:KV_HEADS + 1, :]).astype(i32)
    needs_shift = jnp.max(over) > 0

    for m_ref, acc_ref in zip(m_refs, acc_refs):
        m_ref[...] = jnp.full(m_ref.shape, NEG_BIG, f32)
        acc_ref[...] = jnp.zeros(acc_ref.shape, f32)

    ones_rows = jnp.ones((SUM_ROWS, KB), bf16)

    def att_block(j, carry, near, online):
        r0 = pl.multiple_of(j * KB, KB)

        def qk(g):
            qg = jnp.concatenate([qT_ref[0, h * HEAD_DIM:(h + 1) * HEAD_DIM, :]
                                  for h in range(g * REP, (g + 1) * REP)], axis=1)
            rhs = jnp.concatenate([qg, eye_ref[...]], axis=0)
            piece = KB // QK_PIECES
            parts = []
            for c in range(QK_PIECES):
                rows = pl.ds(r0 + c * piece, piece)
                lhs = jnp.concatenate([k_ref[0, rows, g * HEAD_DIM:(g + 1) * HEAD_DIM], madd_ref[rows, :]],
                                      axis=1)
                parts.append(jnp.dot(lhs, rhs, preferred_element_type=f32))
            return jnp.concatenate(parts, axis=0)

        ts = {0: qk(0)}
        for g in range(KV_HEADS):
            heads = range(g * REP, (g + 1) * REP)
            if g + 1 < KV_HEADS:
                ts[g + 1] = qk(g + 1)
            t = ts.pop(g)
            if near:
                rows = []
                for u in range(TILES_PER_BLOCK):
                    band_row = pl.multiple_of(jnp.clip(j * TILES_PER_BLOCK + u - i + 2, 0, 2) * KT, KT)
                    rows.append(jnp.concatenate([band_ref[h, pl.ds(band_row, KT), :] for h in heads], axis=1))
                t = t + jnp.concatenate(rows, axis=0)
            vt = jnp.concatenate([vT_ref[0, g * HEAD_DIM:(g + 1) * HEAD_DIM, pl.ds(r0, KB)], ones_rows], axis=0)
            if online:
                m_old = m_refs[g][...]
                m_new = jnp.maximum(m_old, jnp.max(t, axis=0, keepdims=True))
                alpha = jnp.exp2(m_old - m_new)
                p = jnp.exp2(t - m_new)
                m_refs[g][...] = m_new
                acc_refs[g][...] = (acc_refs[g][...] * alpha
                                    + jnp.dot(vt, p.astype(bf16), preferred_element_type=f32))
            else:
                p = jnp.exp2(t)
                acc_refs[g][...] = acc_refs[g][...] + jnp.dot(vt, p.astype(bf16), preferred_element_type=f32)
        return carry

    n_far = jnp.maximum(i - 1, 0) // TILES_PER_BLOCK

    def run_blocks(online):
        lax.fori_loop(0, n_far, functools.partial(att_block, near=False, online=online), 0)
        lax.fori_loop(n_far, nb, functools.partial(att_block, near=True, online=online), 0)

    lax.cond(needs_shift, functools.partial(run_blocks, True), functools.partial(run_blocks, False))

    for g in range(KV_HEADS):
        heads = range(g * REP, (g + 1) * REP)
        oT = acc_refs[g][0:HEAD_DIM, :] / acc_refs[g][HEAD_DIM:HEAD_DIM + 1, :]
        for r, h in enumerate(heads):
            gate = gb_ref[:, h * HEAD_DIM:(h + 1) * HEAD_DIM]
            o = oT[:, r * QB:(r + 1) * QB].T
            o_ref[:, h * HEAD_DIM:(h + 1) * HEAD_DIM] = (o * _silu(gate)).astype(o_ref.dtype)


def _dsa(qT, kn, vT, ik, iqT, iwT, proj, band, *, batch, seq, topk):
    m = proj.shape[0]
    nq = seq // QB
    kn3 = kn.reshape(batch, seq, KV_WIDTH)
    ik3 = ik.reshape(batch, seq, IDX_DIM)
    return pl.pallas_call(
        functools.partial(_dsa_kernel, topk=topk),
        grid=(batch, nq),
        in_specs=[
            pl.BlockSpec((1, ATT_WIDTH, QB), lambda b, i: (b, 0, i)),
            pl.BlockSpec((1, seq, KV_WIDTH), lambda b, i: (b, 0, 0)),
            pl.BlockSpec((1, KV_WIDTH, seq), lambda b, i: (b, 0, 0)),
            pl.BlockSpec((1, seq, IDX_DIM), lambda b, i: (b, 0, 0)),
            pl.BlockSpec((1, IDX_HEADS * IDX_DIM, QB), lambda b, i: (b, 0, i)),
            pl.BlockSpec((1, IDX_HEADS, QB), lambda b, i: (b, 0, i)),
            pl.BlockSpec((QB, ATT_WIDTH), lambda b, i: (b * nq + i, OFF_GB // ATT_WIDTH)),
            pl.BlockSpec((ATT_HEADS, BAND_ROWS, QB), lambda b, i: (0, 0, 0)),
        ],
        out_specs=pl.BlockSpec((QB, ATT_WIDTH), lambda b, i: (b * nq + i, 0)),
        out_shape=jax.ShapeDtypeStruct((m, ATT_WIDTH), bf16),
        scratch_shapes=[
            pltpu.VMEM((seq, QB), i32),
            pltpu.VMEM((seq, QB), bf16),
            pltpu.VMEM((QB, REP * QB), bf16),
            pltpu.VMEM((SUBLANES, QB), f32),
        ] + [pltpu.VMEM((1, REP * QB), f32)] * KV_HEADS
          + [pltpu.VMEM((HEAD_DIM + SUM_ROWS, REP * QB), f32)] * KV_HEADS,
        compiler_params=_cparams(("parallel", "arbitrary")),
        name="dsa",
    )(qT, kn3, vT, ik3, iqT, iwT, proj, band)


def _gelu(x):
    return 0.5 * x * (1.0 + jnp.tanh(math.sqrt(2.0 / math.pi) * (x + 0.044715 * (x * x * x))))


def _sgu_kernel(u_ref, v_ref, g_ref, lng_ref, lnb_ref, ws_ref, bsg_ref, o_ref, wsm_ref, bs_ref):
    @pl.when(pl.program_id(0) == 0)
    def _():
        tpos = lax.broadcasted_iota(i32, (SG_CHUNK, SG_CHUNK), 0) // CHUNK
        spos = lax.broadcasted_iota(i32, (SG_CHUNK, SG_CHUNK), 1) // CHUNK
        for g in range(SG_GROUPS):
            wsm_ref[g] = jnp.where(tpos >= spos, ws_ref[g], 0.0).astype(bf16)
        bs_ref[...] = bsg_ref[...].T

    v = _gelu(v_ref[...].astype(f32))
    mu = jnp.mean(v, axis=-1, keepdims=True)
    vc = v - mu
    var = jnp.mean(vc * vc, axis=-1, keepdims=True)
    vn = (vc * lax.rsqrt(var + EPS) * lng_ref[...] + lnb_ref[...]).astype(bf16)
    for g in range(SG_GROUPS):
        sl = slice(g * SG_GROUP, (g + 1) * SG_GROUP)
        mixed = jnp.dot(wsm_ref[g], vn[:, sl], preferred_element_type=f32) + bs_ref[:, g:g + 1]
        gate = g_ref[:, sl].astype(f32)
        o_ref[:, sl] = (_gelu(u_ref[:, sl].astype(f32)) * mixed * _silu(gate)).astype(o_ref.dtype)


def _sgu(proj, ln_g, ln_b, w_s, b_s, lj):
    m = proj.shape[0]
    no = ln_g.shape[0]
    return pl.pallas_call(
        _sgu_kernel,
        grid=(m // SG_CHUNK,),
        in_specs=[
            pl.BlockSpec((SG_CHUNK, SG_WIDTH), lambda i: (i, 0)),
            pl.BlockSpec((SG_CHUNK, SG_WIDTH), lambda i: (i, 1)),
            pl.BlockSpec((SG_CHUNK, SG_WIDTH), lambda i: (i, 2)),
            pl.BlockSpec((None, 1, SG_WIDTH), lambda i: (lj, 0, 0)),
            pl.BlockSpec((None, 1, SG_WIDTH), lambda i: (lj, 0, 0)),
            pl.BlockSpec((None, SG_GROUPS, SG_CHUNK, SG_CHUNK), lambda i: (lj, 0, 0, 0)),
            pl.BlockSpec((None, SG_GROUPS, SG_CHUNK), lambda i: (lj, 0, 0)),
        ],
        out_specs=pl.BlockSpec((SG_CHUNK, SG_WIDTH), lambda i: (i, 0)),
        out_shape=jax.ShapeDtypeStruct((m, SG_WIDTH), bf16),
        scratch_shapes=[pltpu.VMEM((SG_GROUPS, SG_CHUNK, SG_CHUNK), bf16), pltpu.VMEM((SG_CHUNK, SG_GROUPS), f32)],
        compiler_params=_cparams(("arbitrary",)),
        name="sgu",
    )(proj, proj, proj, ln_g.reshape(no, 1, SG_WIDTH), ln_b.reshape(no, 1, SG_WIDTH), w_s, b_s)


PROJ_TN = 1024
EVEN_GATE_BLOCKS = (0, 1, 2, 3, 7, 8)
EVEN_ATT_BLOCKS = (4, 5, 6, 9)
ODD_COL_BLOCKS = tuple(range(3 * SG_WIDTH // PROJ_TN))
TAIL_START = 10240


def kernel(x, p, norm_gain, rel_bias, even_w_in, conv_w, conv_b, lru_w_r, lru_b_r, lru_w_i, lru_b_i,
           lru_lambda, q_norm, k_norm, even_w_out, odd_w_in, sg_ln_g, sg_ln_b, sg_w_s, sg_b_s, odd_w_out,
           pe_w, pe_gate_norm, pe_w_gate):
    batch, seq, d = x.shape
    depth = p.shape[0]
    topk = min(TOPK_MAX, seq // 4)
    m = batch * seq
    h = x.reshape(m, d)
    band = _band(rel_bias)
    hn = _norm(h, norm_gain, 0)
    even_w_in_t = jnp.swapaxes(even_w_in, 1, 2)
    wg_b, pew_b = _cast_bf16(pe_w_gate), _cast_bf16(pe_w)
    for layer in range(depth):
        j = layer // 2
        if layer % 2 == 0:
            proj = _proj(hn, even_w_in_t, j, EVEN_GATE_BLOCKS, transposed=True, tn=PROJ_TN)
            proj_att = _proj(hn, even_w_in_t, j, EVEN_ATT_BLOCKS, transposed=True, tn=PROJ_TN, out_dtype=bf16)
            tail = _proj_tail(hn, even_w_in_t, j, TAIL_START)
            ya = _lru(proj, conv_w, conv_b, lru_w_r, lru_b_r, lru_w_i, lru_b_i, lru_lambda, j,
                      batch=batch, seq=seq)
            qT, kn, vT, iqT, ik, iwT = _prep(proj_att, tail, q_norm, k_norm, j, batch=batch, seq=seq)
            yb = _dsa(qT, kn, vT, ik, iqT, iwT, proj, band, batch=batch, seq=seq, topk=topk)
            h = _outproj(h, [ya, yb], even_w_out, j)
        else:
            proj = _proj(hn, odd_w_in, j, ODD_COL_BLOCKS, tn=PROJ_TN, out_dtype=bf16)
            h = _outproj(h, [_sgu(proj, sg_ln_g, sg_ln_b, sg_w_s, sg_b_s, j)], odd_w_out, j)
        nxt = (norm_gain, layer + 1) if layer + 1 < depth else (None, None)
        h, hn = _pe(h, pe_gate_norm, wg_b, p, pew_b, layer, *nxt)
    return h.reshape(batch, seq, d)
```

```python
import functools
import math

import jax
import jax.numpy as jnp
from jax import lax
from jax.experimental import pallas as pl
from jax.experimental.pallas import tpu as pltpu

f32 = jnp.float32
bf16 = jnp.bfloat16
i32 = jnp.int32

D_MODEL = 2048
CHUNK = 64
PE_DIM = 256
EPS = 1e-6
RNN_WIDTH = 2048
RNN_BLOCK = 128
CONV_WIDTH = 4
LRU_C = 8.0
ATT_HEADS = 16
HEAD_DIM = 128
KV_HEADS = 4
REP = ATT_HEADS // KV_HEADS
ATT_WIDTH = 2048
KV_WIDTH = 512
IDX_HEADS = 16
IDX_DIM = 64
TOPK_MAX = 256
N_BUCKETS = 32
SG_CHUNK = 128
SG_GROUPS = 16
SG_GROUP = 128
SG_WIDTH = 2048

LANES = 128
SUBLANES = 8
QB = 128
KT = 128
KB = 512
TILES_PER_BLOCK = KB // KT
QK_PIECES = 4
SUM_ROWS = 16
LOG2E = math.log2(math.e)
ATT_C2 = HEAD_DIM ** -0.5 * LOG2E
VMEM_LIMIT = 52 * 1024 * 1024
NEG_BIG = -1e30
FAST_LOGIT_LIMIT = 60.0

OFF_XA, OFF_GA, OFF_GB = 0, 2048, 4096
OFF_Q, OFF_K, OFF_V, OFF_IQ = 0, 2048, 2560, 3072


def _cparams(sem):
    return pltpu.CompilerParams(dimension_semantics=sem, vmem_limit_bytes=VMEM_LIMIT)


def _sigmoid(x):
    return 0.5 * jnp.tanh(0.5 * x) + 0.5


def _silu(x):
    return x * _sigmoid(x)


def _block_lookup(table):
    def f(j):
        out = jnp.int32(table[-1])
        for idx in range(len(table) - 2, -1, -1):
            out = jnp.where(j == idx, table[idx], out)
        return out
    return f


def _rms(x, gain):
    ms = jnp.mean(x * x, axis=-1, keepdims=True)
    return x * lax.rsqrt(ms + EPS) * gain


def _norm_kernel(h_ref, g_ref, o_ref):
    o_ref[...] = _rms(h_ref[...], g_ref[...]).astype(o_ref.dtype)


def _norm(h, gains, layer, *, tm=512):
    m, d = h.shape
    return pl.pallas_call(
        _norm_kernel,
        grid=(m // tm,),
        in_specs=[pl.BlockSpec((tm, d), lambda i: (i, 0)),
                  pl.BlockSpec((None, 1, d), lambda i: (layer, 0, 0))],
        out_specs=pl.BlockSpec((tm, d), lambda i: (i, 0)),
        out_shape=jax.ShapeDtypeStruct((m, d), bf16),
        compiler_params=_cparams(("parallel",)),
        name="norm",
    )(h, gains.reshape(gains.shape[0], 1, d))


_NT = (((1,), (1,)), ((), ()))


def _proj_kernel(hn_ref, w_ref, o_ref, wb_ref, *, transposed):
    @pl.when(pl.program_id(1) == 0)
    def _():
        wb_ref[...] = w_ref[...].astype(bf16)

    if transposed:
        acc = lax.dot_general(hn_ref[...], wb_ref[...], _NT, preferred_element_type=f32)
    else:
        acc = jnp.dot(hn_ref[...], wb_ref[...], preferred_element_type=f32)
    o_ref[...] = acc.astype(o_ref.dtype)


def _proj(hn, ws, widx, col_blocks, *, transposed=False, tm=1024, tn=1024, out_dtype=f32):
    m, d = hn.shape
    n = len(col_blocks) * tn
    colmap = _block_lookup(col_blocks)
    if transposed:
        w_spec = pl.BlockSpec((None, tn, d), lambda j, i: (widx, colmap(j), 0))
        wb_shape = (tn, d)
    else:
        w_spec = pl.BlockSpec((None, d, tn), lambda j, i: (widx, 0, colmap(j)))
        wb_shape = (d, tn)
    return pl.pallas_call(
        functools.partial(_proj_kernel, transposed=transposed),
        grid=(n // tn, m // tm),
        in_specs=[pl.BlockSpec((tm, d), lambda j, i: (i, 0)), w_spec],
        out_specs=pl.BlockSpec((tm, tn), lambda j, i: (i, j)),
        out_shape=jax.ShapeDtypeStruct((m, n), out_dtype),
        scratch_shapes=[pltpu.VMEM(wb_shape, bf16)],
        compiler_params=_cparams(("parallel", "arbitrary")),
        name="proj",
    )(hn, ws)


def _proj_tail_kernel(hn_ref, wt_ref, o_ref, *, tail_rows):
    row = lax.broadcasted_iota(i32, wt_ref.shape, 0)
    wt = jnp.where(row < tail_rows, wt_ref[...], 0.0).astype(bf16)
    o_ref[...] = lax.dot_general(hn_ref[...], wt, _NT, preferred_element_type=f32)


def _proj_tail(hn, wts, widx, tail_start, *, tm=1024):
    m, d = hn.shape
    tail_rows = wts.shape[1] - tail_start
    assert 0 < tail_rows < LANES and tail_start % LANES == 0
    return pl.pallas_call(
        functools.partial(_proj_tail_kernel, tail_rows=tail_rows),
        grid=(m // tm,),
        in_specs=[
            pl.BlockSpec((tm, d), lambda i: (i, 0)),
            pl.BlockSpec((None, LANES, d), lambda i: (widx, tail_start // LANES, 0)),
        ],
        out_specs=pl.BlockSpec((tm, LANES), lambda i: (i, 0)),
        out_shape=jax.ShapeDtypeStruct((m, LANES), f32),
        compiler_params=_cparams(("parallel",)),
        name="proj_tail",
    )(hn, wts)


def _cast_kernel(w_ref, o_ref):
    o_ref[...] = w_ref[...].astype(o_ref.dtype)


def _cast_bf16(w, *, tr=512):
    lead, rows, cols = w.shape
    tr = min(tr, rows)
    return pl.pallas_call(
        _cast_kernel,
        grid=(lead, rows // tr),
        in_specs=[pl.BlockSpec((None, tr, cols), lambda a, r: (a, r, 0))],
        out_specs=pl.BlockSpec((None, tr, cols), lambda a, r: (a, r, 0)),
        out_shape=jax.ShapeDtypeStruct(w.shape, bf16),
        compiler_params=_cparams(("parallel", "parallel")),
        name="cast_bf16",
    )(w)


def _outproj_kernel(*refs, n_y):
    h_ref = refs[0]
    y_refs = refs[1:1 + n_y]
    w_refs = refs[1 + n_y:1 + 2 * n_y]
    o_ref = refs[1 + 2 * n_y]
    wb_refs = refs[2 + 2 * n_y:]

    @pl.when(pl.program_id(1) == 0)
    def _():
        for w_ref, wb_ref in zip(w_refs, wb_refs):
            wb_ref[...] = w_ref[...].astype(bf16)

    acc = h_ref[...]
    for y_ref, wb_ref in zip(y_refs, wb_refs):
        acc = acc + jnp.dot(y_ref[...], wb_ref[...], preferred_element_type=f32)
    o_ref[...] = acc


def _outproj(h, ys, ws, widx, *, tm=1024, tn=512):
    m, d = h.shape
    n_y = len(ys)
    kw = ys[0].shape[1]
    in_specs = [pl.BlockSpec((tm, tn), lambda j, i: (i, j))]
    in_specs += [pl.BlockSpec((tm, kw), lambda j, i: (i, 0)) for _ in ys]
    in_specs += [pl.BlockSpec((None, kw, tn), lambda j, i, c=c: (widx, c, j)) for c in range(n_y)]
    return pl.pallas_call(
        functools.partial(_outproj_kernel, n_y=n_y),
        grid=(d // tn, m // tm),
        in_specs=in_specs,
        out_specs=pl.BlockSpec((tm, tn), lambda j, i: (i, j)),
        out_shape=jax.ShapeDtypeStruct((m, d), f32),
        scratch_shapes=[pltpu.VMEM((kw, tn), bf16)] * n_y,
        compiler_params=_cparams(("parallel", "arbitrary")),
        name="outproj",
    )(h, *ys, *([ws] * n_y))


def _pe_kernel(h_ref, gn_ref, wg_ref, p_ref, pew_ref, *rest, tn, emit_next):
    if emit_next:
        gnext_ref, o_ref, hnext_ref, hn_ref, pb_ref, orow_ref = rest
    else:
        o_ref, hn_ref, pb_ref = rest
    j = pl.program_id(1)

    @pl.when(j == 0)
    def _():
        hn_ref[...] = _rms(h_ref[...], gn_ref[...]).astype(bf16)
        pb_ref[...] = p_ref[...].astype(bf16)

    cols = pl.ds(pl.multiple_of(j * tn, tn), tn)
    gate = _sigmoid(jnp.dot(hn_ref[...], wg_ref[...], preferred_element_type=f32))
    e = jnp.dot(pb_ref[...], pew_ref[...], preferred_element_type=f32)
    out = h_ref[:, cols] + gate * e
    o_ref[...] = out

    if emit_next:
        orow_ref[:, cols] = out

        @pl.when(j == pl.num_programs(1) - 1)
        def _():
            hnext_ref[...] = _rms(orow_ref[...], gnext_ref[...]).astype(bf16)


def _pe(h, gns, wgs, p, pews, layer, next_gains=None, next_layer=None, *, tm=1024, tn=512):
    m, d = h.shape
    depth = p.shape[0]
    emit_next = next_gains is not None
    in_specs = [
        pl.BlockSpec((tm, d), lambda i, j: (i, 0)),
        pl.BlockSpec((None, 1, d), lambda i, j: (layer, 0, 0)),
        pl.BlockSpec((None, d, tn), lambda i, j: (layer, 0, j)),
        pl.BlockSpec((None, tm, PE_DIM), lambda i, j: (layer, i, 0)),
        pl.BlockSpec((None, PE_DIM, tn), lambda i, j: (layer, 0, j)),
    ]
    args = [h, gns.reshape(depth, 1, d), wgs, p.reshape(depth, m, PE_DIM), pews]
    out_specs = [pl.BlockSpec((tm, tn), lambda i, j: (i, j))]
    out_shape = [jax.ShapeDtypeStruct((m, d), f32)]
    scratch = [pltpu.VMEM((tm, d), bf16), pltpu.VMEM((tm, PE_DIM), bf16)]
    if emit_next:
        in_specs.append(pl.BlockSpec((None, 1, d), lambda i, j: (next_layer, 0, 0)))
        args.append(next_gains.reshape(next_gains.shape[0], 1, d))
        out_specs.append(pl.BlockSpec((tm, d), lambda i, j: (i, 0)))
        out_shape.append(jax.ShapeDtypeStruct((m, d), bf16))
        scratch.append(pltpu.VMEM((tm, d), f32))
    res = pl.pallas_call(
        functools.partial(_pe_kernel, tn=tn, emit_next=emit_next),
        grid=(m // tm, d // tn),
        in_specs=in_specs,
        out_specs=out_specs,
        out_shape=out_shape,
        scratch_shapes=scratch,
        compiler_params=_cparams(("parallel", "arbitrary")),
        name="pe",
    )(*args)
    return res if emit_next else (res[0], None)


LRU_TT = 512
LRU_CB = 512


def _lru_kernel(x_ref, ga_ref, cw_ref, cb_ref, wr_ref, br_ref, wi_ref, bi_ref, lam_ref, o_ref,
                xext_ref, hcar_ref, a_ref, b_ref):
    tstep = pl.program_id(2)
    pad = SUBLANES

    @pl.when(tstep == 0)
    def _():
        xext_ref[0:pad, :] = jnp.zeros((pad, LRU_CB), f32)
        hcar_ref[...] = jnp.zeros_like(hcar_ref)

    @pl.when(tstep > 0)
    def _():
        xext_ref[0:pad, :] = xext_ref[LRU_TT:LRU_TT + pad, :]

    xext_ref[pad:, :] = x_ref[...]

    xc = cb_ref[...] + cw_ref[CONV_WIDTH - 1:CONV_WIDTH, :] * x_ref[...]
    for j in range(CONV_WIDTH - 1):
        sh = CONV_WIDTH - 1 - j
        xc = xc + cw_ref[j:j + 1, :] * xext_ref[pad - sh:pad - sh + LRU_TT, :]

    xcb = xc.astype(bf16)
    sp = jax.nn.softplus(-lam_ref[...])
    nb = LRU_CB // RNN_BLOCK
    for g in range(nb):
        sl = slice(g * RNN_BLOCK, (g + 1) * RNN_BLOCK)
        xg = xcb[:, sl]
        r = _sigmoid(jnp.dot(xg, wr_ref[g].astype(bf16), preferred_element_type=f32) + br_ref[:, sl])
        ig = _sigmoid(jnp.dot(xg, wi_ref[g].astype(bf16), preferred_element_type=f32) + bi_ref[:, sl])
        log_a = (-LRU_C) * r * sp[:, sl]
        a = jnp.exp(log_a)
        z = jnp.tanh(-log_a) * (1.0 + a * a)
        mult = jnp.where(z > 0.0, z * lax.rsqrt(z), 0.0)
        a_ref[:, sl] = a
        b_ref[:, sl] = mult * ig * xc[:, sl]

    nchunk = LRU_TT // SUBLANES
    a3 = a_ref[...].reshape(nchunk, SUBLANES, LRU_CB)
    b3 = b_ref[...].reshape(nchunk, SUBLANES, LRU_CB)
    row = lax.broadcasted_iota(i32, a3.shape, 1)
    d = 1
    while d < SUBLANES:
        keep = row >= d
        a_sh = jnp.where(keep, pltpu.roll(a3, d, axis=1), 1.0)
        b_sh = jnp.where(keep, pltpu.roll(b3, d, axis=1), 0.0)
        b3 = a3 * b_sh + b3
        a3 = a3 * a_sh
        d *= 2
    a_ref[...] = a3.reshape(LRU_TT, LRU_CB)
    b_ref[...] = b3.reshape(LRU_TT, LRU_CB)

    def carry(c, hprev):
        r0 = pl.multiple_of(c * SUBLANES, SUBLANES)
        h8 = a_ref[pl.ds(r0, SUBLANES), :] * hprev + b_ref[pl.ds(r0, SUBLANES), :]
        gate = ga_ref[pl.ds(r0, SUBLANES), :]
        o_ref[pl.ds(r0, SUBLANES), :] = (h8 * _silu(gate)).astype(o_ref.dtype)
        return jnp.broadcast_to(h8[SUBLANES - 1:SUBLANES, :], (SUBLANES, LRU_CB))

    hcar_ref[...] = lax.fori_loop(0, nchunk, carry, hcar_ref[...], unroll=8)


def _lru(proj, cw, cb, wr, br, wi, bi, lam, lj, *, batch, seq):
    m = proj.shape[0]
    ne = cw.shape[0]
    nt = seq // LRU_TT
    ncb = RNN_WIDTH // LRU_CB
    gpb = LRU_CB // RNN_BLOCK
    ga_off = OFF_GA // LRU_CB
    row = lambda b, c, t: b * nt + t
    vec = pl.BlockSpec((None, 1, LRU_CB), lambda b, c, t: (lj, 0, c))
    gate_w = pl.BlockSpec((None, gpb, RNN_BLOCK, RNN_BLOCK), lambda b, c, t: (lj, c, 0, 0))
    as_vec = lambda a: a.reshape(ne, 1, RNN_WIDTH)
    return pl.pallas_call(
        _lru_kernel,
        grid=(batch, ncb, nt),
        in_specs=[
            pl.BlockSpec((LRU_TT, LRU_CB), lambda b, c, t: (row(b, c, t), c)),
            pl.BlockSpec((LRU_TT, LRU_CB), lambda b, c, t: (row(b, c, t), ga_off + c)),
            pl.BlockSpec((None, CONV_WIDTH, LRU_CB), lambda b, c, t: (lj, 0, c)),
            vec, gate_w, vec, gate_w, vec, vec,
        ],
        out_specs=pl.BlockSpec((LRU_TT, LRU_CB), lambda b, c, t: (row(b, c, t), c)),
        out_shape=jax.ShapeDtypeStruct((m, RNN_WIDTH), bf16),
        scratch_shapes=[
            pltpu.VMEM((LRU_TT + SUBLANES, LRU_CB), f32),
            pltpu.VMEM((SUBLANES, LRU_CB), f32),
            pltpu.VMEM((LRU_TT, LRU_CB), f32),
            pltpu.VMEM((LRU_TT, LRU_CB), f32),
        ],
        compiler_params=_cparams(("parallel", "parallel", "arbitrary")),
        name="lru",
    )(proj, proj, cw, as_vec(cb), wr, as_vec(br), wi, as_vec(bi), as_vec(lam))


PREP_T = 512


def _head_rms(x, gain):
    ms = jnp.mean(x * x, axis=-1, keepdims=True)
    return x * lax.rsqrt(ms + EPS) * gain


def _prep_kernel(q_ref, k_ref, v_ref, iq_ref, tail_ref, qg_ref, kg_ref,
                 qT_ref, kn_ref, vT_ref, iqT_ref, ik_ref, iwT_ref):
    for h in range(ATT_HEADS):
        sl = slice(h * HEAD_DIM, (h + 1) * HEAD_DIM)
        qT_ref[0, sl, :] = (_head_rms(q_ref[:, sl].astype(f32), qg_ref[...]) * ATT_C2).T.astype(bf16)
    for g in range(KV_HEADS):
        sl = slice(g * HEAD_DIM, (g + 1) * HEAD_DIM)
        kn_ref[:, sl] = _head_rms(k_ref[:, sl].astype(f32), kg_ref[...]).astype(bf16)
        vT_ref[0, sl, :] = v_ref[:, sl].T
    for c in range(IDX_HEADS * IDX_DIM // LANES):
        sl = slice(c * LANES, (c + 1) * LANES)
        iqT_ref[0, sl, :] = iq_ref[:, sl].T
    tail_t = tail_ref[...].T
    ik_ref[...] = tail_ref[:, 0:IDX_DIM].astype(bf16)
    iwT_ref[0] = tail_t[IDX_DIM:IDX_DIM + IDX_HEADS, :]


def _prep(proj, tail, q_gains, k_gains, lj, *, batch, seq):
    m = proj.shape[0]
    ne = q_gains.shape[0]
    nt = seq // PREP_T
    bt = lambda i: (i // nt, 0, i % nt)
    return pl.pallas_call(
        _prep_kernel,
        grid=(m // PREP_T,),
        in_specs=[
            pl.BlockSpec((PREP_T, ATT_WIDTH), lambda i: (i, OFF_Q // ATT_WIDTH)),
            pl.BlockSpec((PREP_T, KV_WIDTH), lambda i: (i, OFF_K // KV_WIDTH)),
            pl.BlockSpec((PREP_T, KV_WIDTH), lambda i: (i, OFF_V // KV_WIDTH)),
            pl.BlockSpec((PREP_T, IDX_HEADS * IDX_DIM), lambda i: (i, OFF_IQ // (IDX_HEADS * IDX_DIM))),
            pl.BlockSpec((PREP_T, LANES), lambda i: (i, 0)),
            pl.BlockSpec((None, 1, HEAD_DIM), lambda i: (lj, 0, 0)),
            pl.BlockSpec((None, 1, HEAD_DIM), lambda i: (lj, 0, 0)),
        ],
        out_specs=[
            pl.BlockSpec((1, ATT_WIDTH, PREP_T), bt),
            pl.BlockSpec((PREP_T, KV_WIDTH), lambda i: (i, 0)),
            pl.BlockSpec((1, KV_WIDTH, PREP_T), bt),
            pl.BlockSpec((1, IDX_HEADS * IDX_DIM, PREP_T), bt),
            pl.BlockSpec((PREP_T, IDX_DIM), lambda i: (i, 0)),
            pl.BlockSpec((1, IDX_HEADS, PREP_T), bt),
        ],
        out_shape=[
            jax.ShapeDtypeStruct((batch, ATT_WIDTH, seq), bf16),
            jax.ShapeDtypeStruct((m, KV_WIDTH), bf16),
            jax.ShapeDtypeStruct((batch, KV_WIDTH, seq), bf16),
            jax.ShapeDtypeStruct((batch, IDX_HEADS * IDX_DIM, seq), bf16),
            jax.ShapeDtypeStruct((m, IDX_DIM), bf16),
            jax.ShapeDtypeStruct((batch, IDX_HEADS, seq), f32),
        ],
        compiler_params=_cparams(("parallel",)),
        name="attn_prep",
    )(proj, proj, proj, proj, tail, q_gains.reshape(ne, 1, HEAD_DIM), k_gains.reshape(ne, 1, HEAD_DIM))


BAND_ROWS = 3 * KT
FAR_BUCKET = N_BUCKETS // 2 - 1


def _band_kernel(tab_ref, o_ref):
    jj = lax.broadcasted_iota(i32, (BAND_ROWS, QB), 0)
    qi = lax.broadcasted_iota(i32, (BAND_ROWS, QB), 1)
    rel = jnp.where(jj < KT, -2 * KT, jj - 2 * KT - qi)
    n = jnp.abs(rel)
    large = jnp.full(rel.shape, 8, i32)
    for thr in (12, 16, 23, 32, 46, 64, 91):
        large = large + (n >= thr).astype(i32)
    bucket = jnp.where(rel > 0, N_BUCKETS // 2, 0) + jnp.where(n < 8, n, large)

    def per_head(h, carry):
        acc = jnp.zeros((BAND_ROWS, QB), f32)
        for b in range(N_BUCKETS):
            acc = jnp.where(bucket == b, tab_ref[b, h], acc)
        o_ref[h] = (acc - tab_ref[FAR_BUCKET, h]) * LOG2E
        return carry

    lax.fori_loop(0, ATT_HEADS, per_head, 0)


def _band(rel_bias):
    return pl.pallas_call(
        _band_kernel,
        in_specs=[pl.BlockSpec(memory_space=pltpu.SMEM)],
        out_specs=pl.BlockSpec(memory_space=pltpu.VMEM),
        out_shape=jax.ShapeDtypeStruct((ATT_HEADS, BAND_ROWS, QB), f32),
        name="bias_band",
    )(rel_bias)


def _dsa_kernel(qT_ref, k_ref, vT_ref, ik_ref, iqT_ref, iwT_ref, gb_ref, band_ref, o_ref,
                key_ref, madd_ref, eye_ref, bound_ref, *state_refs, topk):
    m_refs = state_refs[:KV_HEADS]
    acc_refs = state_refs[KV_HEADS:]
    i = pl.program_id(1)
    nb = i // TILES_PER_BLOCK + 1
    t0 = i * QB
    qpos = t0 + lax.broadcasted_iota(i32, (1, QB), 1)
    limit = (qpos // CHUNK + 1) * CHUNK
    sub_iota = lax.broadcasted_iota(i32, (KB, QB), 0)
    idx_scale = (IDX_DIM ** -0.5) * (IDX_HEADS ** -0.5)
    int_min = jnp.int32(-2 ** 31)

    def score_block(j, carry):
        r0 = pl.multiple_of(j * KB, KB)
        ks = ik_ref[0, pl.ds(r0, KB), :]
        acc = jnp.zeros((KB, QB), f32)
        for hp in range(IDX_HEADS // 2):
            h0, h1 = 2 * hp, 2 * hp + 1
            w = jnp.concatenate([iqT_ref[0, h0 * IDX_DIM:(h0 + 1) * IDX_DIM, :],
                                 iqT_ref[0, h1 * IDX_DIM:(h1 + 1) * IDX_DIM, :]], axis=1)
            s = jnp.dot(ks, w, preferred_element_type=f32)
            acc = acc + jnp.maximum(s[:, :QB], 0.0) * iwT_ref[0, h0:h0 + 1, :]
            acc = acc + jnp.maximum(s[:, QB:], 0.0) * iwT_ref[0, h1:h1 + 1, :]
        score = acc * idx_scale
        score = jnp.where(score == 0.0, 0.0, score)
        score = jnp.where(r0 + sub_iota < limit, score, -jnp.inf)
        bits = pltpu.bitcast(score, i32)
        key_ref[pl.ds(r0, KB), :] = jnp.where(bits >= 0, bits, bits ^ jnp.int32(0x7FFFFFFF))
        return carry

    lax.fori_loop(0, nb, score_block, 0)

    def count_rows(pred_fn):
        def body(j, cnt8):
            r0 = pl.multiple_of(j * KB, KB)
            hit = pred_fn(key_ref[pl.ds(r0, KB), :], r0).astype(i32)
            return cnt8 + jnp.sum(hit.reshape(KB // SUBLANES, SUBLANES, QB), axis=0)
        cnt8 = lax.fori_loop(0, nb, body, jnp.zeros((SUBLANES, QB), i32))
        return jnp.sum(cnt8, axis=0, keepdims=True)

    n_nonneg = count_rows(lambda kt, r0: kt >= 0)
    nonneg = n_nonneg >= topk
    prefix0 = jnp.where(nonneg, jnp.int32(0), int_min)
    cur0 = jnp.where(nonneg, n_nonneg, nb * KB)
    n_bits = 31
    bits_per_check = 4

    def bits_cond(state):
        b, _, _, more = state
        return jnp.logical_and(b < n_bits, more > 0)

    def bits_body(state):
        b, prefix, cur, _ = state
        for u in range(bits_per_check):
            shift = jnp.maximum(n_bits - 1 - (b + u), 0)
            bit = jnp.where(b + u < n_bits, lax.shift_left(jnp.int32(1), shift), 0)
            cand = prefix | bit
            cnt = count_rows(lambda kt, r0, cand=cand: kt >= cand)
            take = cnt >= topk
            prefix = jnp.where(take, cand, prefix)
            cur = jnp.where(take, cnt, cur)
        more = jnp.max((cur > topk).astype(i32))
        return b + bits_per_check, prefix, cur, more

    more0 = jnp.max((cur0 > topk).astype(i32))
    _, tau, cur, _ = lax.while_loop(bits_cond, bits_body, (jnp.int32(0), prefix0, cur0, more0))

    neg_inf_key = jnp.int32(-2 ** 31 + 0x7FFFFF)
    excess = jnp.logical_and(cur > topk, tau > neg_inf_key)
    any_excess = jnp.max(excess.astype(i32)) > 0
    n_gt = lax.cond(any_excess, lambda: count_rows(lambda kt, r0: kt > tau), lambda: jnp.zeros((1, QB), i32))
    room = topk - n_gt
    n_iter = jnp.where(any_excess, 13, 0)

    def cut_step(b, cut):
        cand = cut | lax.shift_left(jnp.int32(1), 12 - b)
        cnt = count_rows(lambda kt, r0: jnp.logical_and(kt == tau, r0 + sub_iota < cand))
        return jnp.where(cnt <= room, cand, cut)

    cut = lax.fori_loop(0, n_iter, cut_step, jnp.zeros((1, QB), i32))
    cut = jnp.where(excess, cut, jnp.int32(2 ** 30))

    def mask_block(j, carry):
        r0 = pl.multiple_of(j * KB, KB)
        kt = key_ref[pl.ds(r0, KB), :]
        pos = r0 + sub_iota
        sel = jnp.logical_or(kt > tau, jnp.logical_and(kt == tau, pos < cut))
        sel = jnp.logical_and(sel, pos < limit)
        madd_ref[pl.ds(r0, KB), :] = jnp.where(sel, 0.0, NEG_BIG).astype(bf16)
        return carry

    lax.fori_loop(0, nb, mask_block, 0)

    @pl.when(i == 0)
    def _():
        rowi = lax.broadcasted_iota(i32, (QB, REP * QB), 0)
        coli = lax.broadcasted_iota(i32, (QB, REP * QB), 1)
        eye_ref[...] = ((coli & (QB - 1)) == rowi).astype(bf16)
        for g in range(KV_HEADS):
            def kmax_body(j, mx, g=g):
                r0 = pl.multiple_of(j * KB, KB)
                kf = k_ref[0, pl.ds(r0, KB), g * HEAD_DIM:(g + 1) * HEAD_DIM].astype(f32)
                n2 = jnp.sum(kf * kf, axis=-1, keepdims=True)
                return jnp.maximum(mx, jnp.max(n2, axis=0, keepdims=True))
            k2 = lax.fori_loop(0, k_ref.shape[1] // KB, kmax_body, jnp.zeros((1, 1), f32))
            bound_ref[g:g + 1, :] = jnp.broadcast_to(k2, (1, QB))
        bmax = jnp.max(jnp.abs(band_ref[...]).reshape(ATT_HEADS * BAND_ROWS, QB), axis=0, keepdims=True)
        room = FAST_LOGIT_LIMIT - jnp.max(bmax, axis=1, keepdims=True)
        bound_ref[KV_HEADS:KV_HEADS + 1, :] = jnp.broadcast_to(jnp.where(room > 0.0, room * room, -1.0), (1, QB))

    over = jnp.zeros((1, QB), i32)
    for g in range(KV_HEADS):
        q2 = jnp.zeros((1, QB), f32)
        for h in range(g * REP, (g + 1) * REP):
            qh = qT_ref[0, h * HEAD_DIM:(h + 1) * HEAD_DIM, :].astype(f32)
            q2 = jnp.maximum(q2, jnp.sum(qh * qh, axis=0, keepdims=True))
        over = over | (q2 * bound_ref[g:g + 1, :] * 1.002 > bound_ref[KV_HEADS:KV_HEADS + 1, :]).astype(i32)
    needs_shift = jnp.max(over) > 0

    for m_ref, acc_ref in zip(m_refs, acc_refs):
        m_ref[...] = jnp.full(m_ref.shape, NEG_BIG, f32)
        acc_ref[...] = jnp.zeros(acc_ref.shape, f32)

    ones_rows = jnp.ones((SUM_ROWS, KB), bf16)

    def att_block(j, carry, near, online):
        r0 = pl.multiple_of(j * KB, KB)

        def qk(g):
            qg = jnp.concatenate([qT_ref[0, h * HEAD_DIM:(h + 1) * HEAD_DIM, :]
                                  for h in range(g * REP, (g + 1) * REP)], axis=1)
            rhs = jnp.concatenate([qg, eye_ref[...]], axis=0)
            piece = KB // QK_PIECES
            parts = []
            for c in range(QK_PIECES):
                rows = pl.ds(r0 + c * piece, piece)
                lhs = jnp.concatenate([k_ref[0, rows, g * HEAD_DIM:(g + 1) * HEAD_DIM], madd_ref[rows, :]],
                                      axis=1)
                parts.append(jnp.dot(lhs, rhs, preferred_element_type=f32))
            return jnp.concatenate(parts, axis=0)

        ts = {0: qk(0)}
        for g in range(KV_HEADS):
            heads = range(g * REP, (g + 1) * REP)
            if g + 1 < KV_HEADS:
                ts[g + 1] = qk(g + 1)
            t = ts.pop(g)
            if near:
                rows = []
                for u in range(TILES_PER_BLOCK):
                    band_row = pl.multiple_of(jnp.clip(j * TILES_PER_BLOCK + u - i + 2, 0, 2) * KT, KT)
                    rows.append(jnp.concatenate([band_ref[h, pl.ds(band_row, KT), :] for h in heads], axis=1))
                t = t + jnp.concatenate(rows, axis=0)
            vt = jnp.concatenate([vT_ref[0, g * HEAD_DIM:(g + 1) * HEAD_DIM, pl.ds(r0, KB)], ones_rows], axis=0)
            if online:
                m_old = m_refs[g][...]
                m_new = jnp.maximum(m_old, jnp.max(t, axis=0, keepdims=True))
                alpha = jnp.exp2(m_old - m_new)
                p = jnp.exp2(t - m_new)
                m_refs[g][...] = m_new
                acc_refs[g][...] = (acc_refs[g][...] * alpha
                                    + jnp.dot(vt, p.astype(bf16), preferred_element_type=f32))
            else:
                p = jnp.exp2(t)
                acc_refs[g][...] = acc_refs[g][...] + jnp.dot(vt, p.astype(bf16), preferred_element_type=f32)
        return carry

    n_far = jnp.maximum(i - 1, 0) // TILES_PER_BLOCK

    def run_blocks(online):
        lax.fori_loop(0, n_far, functools.partial(att_block, near=False, online=online), 0)
        lax.fori_loop(n_far, nb, functools.partial(att_block, near=True, online=online), 0)

    lax.cond(needs_shift, functools.partial(run_blocks, True), functools.partial(run_blocks, False))

    for g in range(KV_HEADS):
        heads = range(g * REP, (g + 1) * REP)
        oT = acc_refs[g][0:HEAD_DIM, :] / acc_refs[g][HEAD_DIM:HEAD_DIM + 1, :]
        for r, h in enumerate(heads):
            gate = gb_ref[:, h * HEAD_DIM:(h + 1) * HEAD_DIM]
            o = oT[:, r * QB:(r + 1) * QB].T
            o_ref[:, h * HEAD_DIM:(h + 1) * HEAD_DIM] = (o * _silu(gate)).astype(o_ref.dtype)


def _dsa(qT, kn, vT, ik, iqT, iwT, proj, band, *, batch, seq, topk):
    m = proj.shape[0]
    nq = seq // QB
    kn3 = kn.reshape(batch, seq, KV_WIDTH)
    ik3 = ik.reshape(batch, seq, IDX_DIM)
    return pl.pallas_call(
        functools.partial(_dsa_kernel, topk=topk),
        grid=(batch, nq),
        in_specs=[
            pl.BlockSpec((1, ATT_WIDTH, QB), lambda b, i: (b, 0, i)),
            pl.BlockSpec((1, seq, KV_WIDTH), lambda b, i: (b, 0, 0)),
            pl.BlockSpec((1, KV_WIDTH, seq), lambda b, i: (b, 0, 0)),
            pl.BlockSpec((1, seq, IDX_DIM), lambda b, i: (b, 0, 0)),
            pl.BlockSpec((1, IDX_HEADS * IDX_DIM, QB), lambda b, i: (b, 0, i)),
            pl.BlockSpec((1, IDX_HEADS, QB), lambda b, i: (b, 0, i)),
            pl.BlockSpec((QB, ATT_WIDTH), lambda b, i: (b * nq + i, OFF_GB // ATT_WIDTH)),
            pl.BlockSpec((ATT_HEADS, BAND_ROWS, QB), lambda b, i: (0, 0, 0)),
        ],
        out_specs=pl.BlockSpec((QB, ATT_WIDTH), lambda b, i: (b * nq + i, 0)),
        out_shape=jax.ShapeDtypeStruct((m, ATT_WIDTH), bf16),
        scratch_shapes=[
            pltpu.VMEM((seq, QB), i32),
            pltpu.VMEM((seq, QB), bf16),
            pltpu.VMEM((QB, REP * QB), bf16),
            pltpu.VMEM((SUBLANES, QB), f32),
        ] + [pltpu.VMEM((1, REP * QB), f32)] * KV_HEADS
          + [pltpu.VMEM((HEAD_DIM + SUM_ROWS, REP * QB), f32)] * KV_HEADS,
        compiler_params=_cparams(("parallel", "arbitrary")),
        name="dsa",
    )(qT, kn3, vT, ik3, iqT, iwT, proj, band)


def _gelu(x):
    c = math.sqrt(2.0 / math.pi)
    half = 0.5 * x
    return half + half * jnp.tanh(x * (c + (c * 0.044715) * (x * x)))


def _sgu_kernel(u_ref, v_ref, g_ref, lng_ref, lnb_ref, ws_ref, bsg_ref, o_ref, wsm_ref, bs_ref):
    @pl.when(pl.program_id(0) == 0)
    def _():
        tpos = lax.broadcasted_iota(i32, (SG_CHUNK, SG_CHUNK), 0) // CHUNK
        spos = lax.broadcasted_iota(i32, (SG_CHUNK, SG_CHUNK), 1) // CHUNK
        for g in range(SG_GROUPS):
            wsm_ref[g] = jnp.where(tpos >= spos, ws_ref[g], 0.0).astype(bf16)
        bs_ref[...] = bsg_ref[...].T

    v = _gelu(v_ref[...].astype(f32))
    mu = jnp.mean(v, axis=-1, keepdims=True)
    vc = v - mu
    var = jnp.mean(vc * vc, axis=-1, keepdims=True)
    vn = (vc * lax.rsqrt(var + EPS) * lng_ref[...] + lnb_ref[...]).astype(bf16)
    for g in range(SG_GROUPS):
        sl = slice(g * SG_GROUP, (g + 1) * SG_GROUP)
        mixed = jnp.dot(wsm_ref[g], vn[:, sl], preferred_element_type=f32) + bs_ref[:, g:g + 1]
        gate = g_ref[:, sl].astype(f32)
        o_ref[:, sl] = (_gelu(u_ref[:, sl].astype(f32)) * mixed * _silu(gate)).astype(o_ref.dtype)


def _sgu(proj, ln_g, ln_b, w_s, b_s, lj):
    m = proj.shape[0]
    no = ln_g.shape[0]
    return pl.pallas_call(
        _sgu_kernel,
        grid=(m // SG_CHUNK,),
        in_specs=[
            pl.BlockSpec((SG_CHUNK, SG_WIDTH), lambda i: (i, 0)),
            pl.BlockSpec((SG_CHUNK, SG_WIDTH), lambda i: (i, 1)),
            pl.BlockSpec((SG_CHUNK, SG_WIDTH), lambda i: (i, 2)),
            pl.BlockSpec((None, 1, SG_WIDTH), lambda i: (lj, 0, 0)),
            pl.BlockSpec((None, 1, SG_WIDTH), lambda i: (lj, 0, 0)),
            pl.BlockSpec((None, SG_GROUPS, SG_CHUNK, SG_CHUNK), lambda i: (lj, 0, 0, 0)),
            pl.BlockSpec((None, SG_GROUPS, SG_CHUNK), lambda i: (lj, 0, 0)),
        ],
        out_specs=pl.BlockSpec((SG_CHUNK, SG_WIDTH), lambda i: (i, 0)),
        out_shape=jax.ShapeDtypeStruct((m, SG_WIDTH), bf16),
        scratch_shapes=[pltpu.VMEM((SG_GROUPS, SG_CHUNK, SG_CHUNK), bf16), pltpu.VMEM((SG_CHUNK, SG_GROUPS), f32)],
        compiler_params=_cparams(("arbitrary",)),
        name="sgu",
    )(proj, proj, proj, ln_g.reshape(no, 1, SG_WIDTH), ln_b.reshape(no, 1, SG_WIDTH), w_s, b_s)


PROJ_TN = 1024
EVEN_GATE_BLOCKS = (0, 1, 2, 3, 7, 8)
EVEN_ATT_BLOCKS = (4, 5, 6, 9)
ODD_COL_BLOCKS = tuple(range(3 * SG_WIDTH // PROJ_TN))
TAIL_START = 10240


def kernel(x, p, norm_gain, rel_bias, even_w_in, conv_w, conv_b, lru_w_r, lru_b_r, lru_w_i, lru_b_i,
           lru_lambda, q_norm, k_norm, even_w_out, odd_w_in, sg_ln_g, sg_ln_b, sg_w_s, sg_b_s, odd_w_out,
           pe_w, pe_gate_norm, pe_w_gate):
    batch, seq, d = x.shape
    depth = p.shape[0]
    topk = min(TOPK_MAX, seq // 4)
    m = batch * seq
    h = x.reshape(m, d)
    band = _band(rel_bias)
    hn = _norm(h, norm_gain, 0)
    even_w_in_t = jnp.swapaxes(even_w_in, 1, 2)
    wg_b, pew_b = _cast_bf16(pe_w_gate), _cast_bf16(pe_w)
    for layer in range(depth):
        j = layer // 2
        if layer % 2 == 0:
            proj = _proj(hn, even_w_in_t, j, EVEN_GATE_BLOCKS, transposed=True, tn=PROJ_TN)
            proj_att = _proj(hn, even_w_in_t, j, EVEN_ATT_BLOCKS, transposed=True, tn=PROJ_TN, out_dtype=bf16)
            tail = _proj_tail(hn, even_w_in_t, j, TAIL_START)
            ya = _lru(proj, conv_w, conv_b, lru_w_r, lru_b_r, lru_w_i, lru_b_i, lru_lambda, j,
                      batch=batch, seq=seq)
            qT, kn, vT, iqT, ik, iwT = _prep(proj_att, tail, q_norm, k_norm, j, batch=batch, seq=seq)
            yb = _dsa(qT, kn, vT, ik, iqT, iwT, proj, band, batch=batch, seq=seq, topk=topk)
            h = _outproj(h, [ya, yb], even_w_out, j)
        else:
            proj = _proj(hn, odd_w_in, j, ODD_COL_BLOCKS, tn=PROJ_TN, out_dtype=bf16)
            h = _outproj(h, [_sgu(proj, sg_ln_g, sg_ln_b, sg_w_s, sg_b_s, j)], odd_w_out, j)
        nxt = (norm_gain, layer + 1) if layer + 1 < depth else (None, None)
        h, hn = _pe(h, pe_gate_norm, wg_b, p, pew_b, layer, *nxt)
    return h.reshape(batch, seq, d)
```

```python
import functools
import math

import jax
import jax.numpy as jnp
from jax import lax
from jax.experimental import pallas as pl
from jax.experimental.pallas import tpu as pltpu

f32 = jnp.float32
bf16 = jnp.bfloat16
i32 = jnp.int32

D_MODEL = 2048
CHUNK = 64
PE_DIM = 256
EPS = 1e-6
RNN_WIDTH = 2048
RNN_BLOCK = 128
CONV_WIDTH = 4
LRU_C = 8.0
ATT_HEADS = 16
HEAD_DIM = 128
KV_HEADS = 4
REP = ATT_HEADS // KV_HEADS
ATT_WIDTH = 2048
KV_WIDTH = 512
IDX_HEADS = 16
IDX_DIM = 64
TOPK_MAX = 256
N_BUCKETS = 32
SG_CHUNK = 128
SG_GROUPS = 16
SG_GROUP = 128
SG_WIDTH = 2048

LANES = 128
SUBLANES = 8
QB = 128
KT = 128
KB = 512
TILES_PER_BLOCK = KB // KT
QK_PIECES = 4
RADIX_FIXED_BITS = 24
SUM_ROWS = 16
LOG2E = math.log2(math.e)
ATT_C2 = HEAD_DIM ** -0.5 * LOG2E
VMEM_LIMIT = 52 * 1024 * 1024
NEG_BIG = -1e30
FAST_LOGIT_LIMIT = 60.0

OFF_XA, OFF_GA, OFF_GB = 0, 2048, 4096
OFF_Q, OFF_K, OFF_V, OFF_IQ = 0, 2048, 2560, 3072


def _cparams(sem):
    return pltpu.CompilerParams(dimension_semantics=sem, vmem_limit_bytes=VMEM_LIMIT)


def _sigmoid(x):
    return 0.5 * jnp.tanh(0.5 * x) + 0.5


def _silu(x):
    return x * _sigmoid(x)


def _block_lookup(table):
    def f(j):
        out = jnp.int32(table[-1])
        for idx in range(len(table) - 2, -1, -1):
            out = jnp.where(j == idx, table[idx], out)
        return out
    return f


def _rms(x, gain):
    ms = jnp.mean(x * x, axis=-1, keepdims=True)
    return x * lax.rsqrt(ms + EPS) * gain


def _norm_kernel(h_ref, g_ref, o_ref):
    o_ref[...] = _rms(h_ref[...], g_ref[...]).astype(o_ref.dtype)


def _norm(h, gains, layer, *, tm=512):
    m, d = h.shape
    return pl.pallas_call(
        _norm_kernel,
        grid=(m // tm,),
        in_specs=[pl.BlockSpec((tm, d), lambda i: (i, 0)),
                  pl.BlockSpec((None, 1, d), lambda i: (layer, 0, 0))],
        out_specs=pl.BlockSpec((tm, d), lambda i: (i, 0)),
        out_shape=jax.ShapeDtypeStruct((m, d), bf16),
        compiler_params=_cparams(("parallel",)),
        name="norm",
    )(h, gains.reshape(gains.shape[0], 1, d))


_NT = (((1,), (1,)), ((), ()))


def _proj_kernel(hn_ref, w_ref, o_ref, wb_ref, *, transposed):
    @pl.when(pl.program_id(1) == 0)
    def _():
        wb_ref[...] = w_ref[...].astype(bf16)

    if transposed:
        acc = lax.dot_general(hn_ref[...], wb_ref[...], _NT, preferred_element_type=f32)
    else:
        acc = jnp.dot(hn_ref[...], wb_ref[...], preferred_element_type=f32)
    o_ref[...] = acc.astype(o_ref.dtype)


def _proj(hn, ws, widx, col_blocks, *, transposed=False, tm=1024, tn=1024, out_dtype=f32):
    m, d = hn.shape
    n = len(col_blocks) * tn
    colmap = _block_lookup(col_blocks)
    if transposed:
        w_spec = pl.BlockSpec((None, tn, d), lambda j, i: (widx, colmap(j), 0))
        wb_shape = (tn, d)
    else:
        w_spec = pl.BlockSpec((None, d, tn), lambda j, i: (widx, 0, colmap(j)))
        wb_shape = (d, tn)
    return pl.pallas_call(
        functools.partial(_proj_kernel, transposed=transposed),
        grid=(n // tn, m // tm),
        in_specs=[pl.BlockSpec((tm, d), lambda j, i: (i, 0)), w_spec],
        out_specs=pl.BlockSpec((tm, tn), lambda j, i: (i, j)),
        out_shape=jax.ShapeDtypeStruct((m, n), out_dtype),
        scratch_shapes=[pltpu.VMEM(wb_shape, bf16)],
        compiler_params=_cparams(("parallel", "arbitrary")),
        name="proj",
    )(hn, ws)


def _proj_tail_kernel(hn_ref, wt_ref, o_ref, *, tail_rows):
    row = lax.broadcasted_iota(i32, wt_ref.shape, 0)
    wt = jnp.where(row < tail_rows, wt_ref[...], 0.0).astype(bf16)
    o_ref[...] = lax.dot_general(hn_ref[...], wt, _NT, preferred_element_type=f32)


def _proj_tail(hn, wts, widx, tail_start, *, tm=1024):
    m, d = hn.shape
    tail_rows = wts.shape[1] - tail_start
    assert 0 < tail_rows < LANES and tail_start % LANES == 0
    return pl.pallas_call(
        functools.partial(_proj_tail_kernel, tail_rows=tail_rows),
        grid=(m // tm,),
        in_specs=[
            pl.BlockSpec((tm, d), lambda i: (i, 0)),
            pl.BlockSpec((None, LANES, d), lambda i: (widx, tail_start // LANES, 0)),
        ],
        out_specs=pl.BlockSpec((tm, LANES), lambda i: (i, 0)),
        out_shape=jax.ShapeDtypeStruct((m, LANES), f32),
        compiler_params=_cparams(("parallel",)),
        name="proj_tail",
    )(hn, wts)


def _cast_kernel(w_ref, o_ref):
    o_ref[...] = w_ref[...].astype(o_ref.dtype)


def _cast_bf16(w, *, tr=512):
    lead, rows, cols = w.shape
    tr = min(tr, rows)
    return pl.pallas_call(
        _cast_kernel,
        grid=(lead, rows // tr),
        in_specs=[pl.BlockSpec((None, tr, cols), lambda a, r: (a, r, 0))],
        out_specs=pl.BlockSpec((None, tr, cols), lambda a, r: (a, r, 0)),
        out_shape=jax.ShapeDtypeStruct(w.shape, bf16),
        compiler_params=_cparams(("parallel", "parallel")),
        name="cast_bf16",
    )(w)


def _outproj_kernel(*refs, n_y):
    h_ref = refs[0]
    y_refs = refs[1:1 + n_y]
    w_refs = refs[1 + n_y:1 + 2 * n_y]
    o_ref = refs[1 + 2 * n_y]
    wb_refs = refs[2 + 2 * n_y:]

    @pl.when(pl.program_id(1) == 0)
    def _():
        for w_ref, wb_ref in zip(w_refs, wb_refs):
            wb_ref[...] = w_ref[...].astype(bf16)

    acc = h_ref[...]
    for y_ref, wb_ref in zip(y_refs, wb_refs):
        acc = acc + jnp.dot(y_ref[...], wb_ref[...], preferred_element_type=f32)
    o_ref[...] = acc


def _outproj(h, ys, ws, widx, *, tm=1024, tn=512):
    m, d = h.shape
    n_y = len(ys)
    kw = ys[0].shape[1]
    in_specs = [pl.BlockSpec((tm, tn), lambda j, i: (i, j))]
    in_specs += [pl.BlockSpec((tm, kw), lambda j, i: (i, 0)) for _ in ys]
    in_specs += [pl.BlockSpec((None, kw, tn), lambda j, i, c=c: (widx, c, j)) for c in range(n_y)]
    return pl.pallas_call(
        functools.partial(_outproj_kernel, n_y=n_y),
        grid=(d // tn, m // tm),
        in_specs=in_specs,
        out_specs=pl.BlockSpec((tm, tn), lambda j, i: (i, j)),
        out_shape=jax.ShapeDtypeStruct((m, d), f32),
        scratch_shapes=[pltpu.VMEM((kw, tn), bf16)] * n_y,
        compiler_params=_cparams(("parallel", "arbitrary")),
        name="outproj",
    )(h, *ys, *([ws] * n_y))


def _pe_kernel(h_ref, gn_ref, wg_ref, p_ref, pew_ref, *rest, tn, emit_next):
    if emit_next:
        gnext_ref, o_ref, hnext_ref, hn_ref, pb_ref, orow_ref = rest
    else:
        o_ref, hn_ref, pb_ref = rest
    j = pl.program_id(1)

    @pl.when(j == 0)
    def _():
        hn_ref[...] = _rms(h_ref[...], gn_ref[...]).astype(bf16)
        pb_ref[...] = p_ref[...].astype(bf16)

    cols = pl.ds(pl.multiple_of(j * tn, tn), tn)
    gate = _sigmoid(jnp.dot(hn_ref[...], wg_ref[...], preferred_element_type=f32))
    e = jnp.dot(pb_ref[...], pew_ref[...], preferred_element_type=f32)
    out = h_ref[:, cols] + gate * e
    o_ref[...] = out

    if emit_next:
        orow_ref[:, cols] = out

        @pl.when(j == pl.num_programs(1) - 1)
        def _():
            hnext_ref[...] = _rms(orow_ref[...], gnext_ref[...]).astype(bf16)


def _pe(h, gns, wgs, p, pews, layer, next_gains=None, next_layer=None, *, tm=1024, tn=512):
    m, d = h.shape
    depth = p.shape[0]
    emit_next = next_gains is not None
    in_specs = [
        pl.BlockSpec((tm, d), lambda i, j: (i, 0)),
        pl.BlockSpec((None, 1, d), lambda i, j: (layer, 0, 0)),
        pl.BlockSpec((None, d, tn), lambda i, j: (layer, 0, j)),
        pl.BlockSpec((None, tm, PE_DIM), lambda i, j: (layer, i, 0)),
        pl.BlockSpec((None, PE_DIM, tn), lambda i, j: (layer, 0, j)),
    ]
    args = [h, gns.reshape(depth, 1, d), wgs, p.reshape(depth, m, PE_DIM), pews]
    out_specs = [pl.BlockSpec((tm, tn), lambda i, j: (i, j))]
    out_shape = [jax.ShapeDtypeStruct((m, d), f32)]
    scratch = [pltpu.VMEM((tm, d), bf16), pltpu.VMEM((tm, PE_DIM), bf16)]
    if emit_next:
        in_specs.append(pl.BlockSpec((None, 1, d), lambda i, j: (next_layer, 0, 0)))
        args.append(next_gains.reshape(next_gains.shape[0], 1, d))
        out_specs.append(pl.BlockSpec((tm, d), lambda i, j: (i, 0)))
        out_shape.append(jax.ShapeDtypeStruct((m, d), bf16))
        scratch.append(pltpu.VMEM((tm, d), f32))
    res = pl.pallas_call(
        functools.partial(_pe_kernel, tn=tn, emit_next=emit_next),
        grid=(m // tm, d // tn),
        in_specs=in_specs,
        out_specs=out_specs,
        out_shape=out_shape,
        scratch_shapes=scratch,
        compiler_params=_cparams(("parallel", "arbitrary")),
        name="pe",
    )(*args)
    return res if emit_next else (res[0], None)


LRU_TT = 1024
LRU_CB = 512


def _lru_kernel(x_ref, ga_ref, cw_ref, cb_ref, wr_ref, br_ref, wi_ref, bi_ref, lam_ref, o_ref,
                xext_ref, hcar_ref, a_ref, b_ref):
    tstep = pl.program_id(2)
    pad = SUBLANES

    @pl.when(tstep == 0)
    def _():
        xext_ref[0:pad, :] = jnp.zeros((pad, LRU_CB), f32)
        hcar_ref[...] = jnp.zeros_like(hcar_ref)

    @pl.when(tstep > 0)
    def _():
        xext_ref[0:pad, :] = xext_ref[LRU_TT:LRU_TT + pad, :]

    xext_ref[pad:, :] = x_ref[...]

    xc = cb_ref[...] + cw_ref[CONV_WIDTH - 1:CONV_WIDTH, :] * x_ref[...]
    for j in range(CONV_WIDTH - 1):
        sh = CONV_WIDTH - 1 - j
        xc = xc + cw_ref[j:j + 1, :] * xext_ref[pad - sh:pad - sh + LRU_TT, :]

    xcb = xc.astype(bf16)
    sp = jax.nn.softplus(-lam_ref[...])
    nb = LRU_CB // RNN_BLOCK
    for g in range(nb):
        sl = slice(g * RNN_BLOCK, (g + 1) * RNN_BLOCK)
        xg = xcb[:, sl]
        r = _sigmoid(jnp.dot(xg, wr_ref[g].astype(bf16), preferred_element_type=f32) + br_ref[:, sl])
        ig = _sigmoid(jnp.dot(xg, wi_ref[g].astype(bf16), preferred_element_type=f32) + bi_ref[:, sl])
        log_a = (-LRU_C) * r * sp[:, sl]
        a = jnp.exp(log_a)
        z = jnp.tanh(-log_a) * (1.0 + a * a)
        mult = jnp.where(z > 0.0, z * lax.rsqrt(z), 0.0)
        a_ref[:, sl] = a
        b_ref[:, sl] = mult * ig * xc[:, sl]

    nchunk = LRU_TT // SUBLANES
    a3 = a_ref[...].reshape(nchunk, SUBLANES, LRU_CB)
    b3 = b_ref[...].reshape(nchunk, SUBLANES, LRU_CB)
    row = lax.broadcasted_iota(i32, a3.shape, 1)
    d = 1
    while d < SUBLANES:
        keep = row >= d
        a_sh = jnp.where(keep, pltpu.roll(a3, d, axis=1), 1.0)
        b_sh = jnp.where(keep, pltpu.roll(b3, d, axis=1), 0.0)
        b3 = a3 * b_sh + b3
        a3 = a3 * a_sh
        d *= 2
    a_ref[...] = a3.reshape(LRU_TT, LRU_CB)
    b_ref[...] = b3.reshape(LRU_TT, LRU_CB)

    def carry(c, hprev):
        r0 = pl.multiple_of(c * SUBLANES, SUBLANES)
        h8 = a_ref[pl.ds(r0, SUBLANES), :] * hprev + b_ref[pl.ds(r0, SUBLANES), :]
        gate = ga_ref[pl.ds(r0, SUBLANES), :]
        o_ref[pl.ds(r0, SUBLANES), :] = (h8 * _silu(gate)).astype(o_ref.dtype)
        return jnp.broadcast_to(h8[SUBLANES - 1:SUBLANES, :], (SUBLANES, LRU_CB))

    hcar_ref[...] = lax.fori_loop(0, nchunk, carry, hcar_ref[...], unroll=8)


def _lru(proj, cw, cb, wr, br, wi, bi, lam, lj, *, batch, seq):
    m = proj.shape[0]
    ne = cw.shape[0]
    nt = seq // LRU_TT
    ncb = RNN_WIDTH // LRU_CB
    gpb = LRU_CB // RNN_BLOCK
    ga_off = OFF_GA // LRU_CB
    row = lambda b, c, t: b * nt + t
    vec = pl.BlockSpec((None, 1, LRU_CB), lambda b, c, t: (lj, 0, c))
    gate_w = pl.BlockSpec((None, gpb, RNN_BLOCK, RNN_BLOCK), lambda b, c, t: (lj, c, 0, 0))
    as_vec = lambda a: a.reshape(ne, 1, RNN_WIDTH)
    return pl.pallas_call(
        _lru_kernel,
        grid=(batch, ncb, nt),
        in_specs=[
            pl.BlockSpec((LRU_TT, LRU_CB), lambda b, c, t: (row(b, c, t), c)),
            pl.BlockSpec((LRU_TT, LRU_CB), lambda b, c, t: (row(b, c, t), ga_off + c)),
            pl.BlockSpec((None, CONV_WIDTH, LRU_CB), lambda b, c, t: (lj, 0, c)),
            vec, gate_w, vec, gate_w, vec, vec,
        ],
        out_specs=pl.BlockSpec((LRU_TT, LRU_CB), lambda b, c, t: (row(b, c, t), c)),
        out_shape=jax.ShapeDtypeStruct((m, RNN_WIDTH), bf16),
        scratch_shapes=[
            pltpu.VMEM((LRU_TT + SUBLANES, LRU_CB), f32),
            pltpu.VMEM((SUBLANES, LRU_CB), f32),
            pltpu.VMEM((LRU_TT, LRU_CB), f32),
            pltpu.VMEM((LRU_TT, LRU_CB), f32),
        ],
        compiler_params=_cparams(("parallel", "parallel", "arbitrary")),
        name="lru",
    )(proj, proj, cw, as_vec(cb), wr, as_vec(br), wi, as_vec(bi), as_vec(lam))


PREP_T = 512


def _head_rms(x, gain):
    ms = jnp.mean(x * x, axis=-1, keepdims=True)
    return x * lax.rsqrt(ms + EPS) * gain


def _prep_kernel(q_ref, k_ref, v_ref, iq_ref, tail_ref, qg_ref, kg_ref,
                 qT_ref, kn_ref, vT_ref, iqT_ref, ik_ref, iwT_ref):
    for h in range(ATT_HEADS):
        sl = slice(h * HEAD_DIM, (h + 1) * HEAD_DIM)
        qT_ref[0, sl, :] = (_head_rms(q_ref[:, sl].astype(f32), qg_ref[...]) * ATT_C2).T.astype(bf16)
    for g in range(KV_HEADS):
        sl = slice(g * HEAD_DIM, (g + 1) * HEAD_DIM)
        kn_ref[:, sl] = _head_rms(k_ref[:, sl].astype(f32), kg_ref[...]).astype(bf16)
        vT_ref[0, sl, :] = v_ref[:, sl].T
    for c in range(IDX_HEADS * IDX_DIM // LANES):
        sl = slice(c * LANES, (c + 1) * LANES)
        iqT_ref[0, sl, :] = iq_ref[:, sl].T
    tail_t = tail_ref[...].T
    ik_ref[...] = tail_ref[:, 0:IDX_DIM].astype(bf16)
    iwT_ref[0] = tail_t[IDX_DIM:IDX_DIM + IDX_HEADS, :]


def _prep(proj, tail, q_gains, k_gains, lj, *, batch, seq):
    m = proj.shape[0]
    ne = q_gains.shape[0]
    nt = seq // PREP_T
    bt = lambda i: (i // nt, 0, i % nt)
    return pl.pallas_call(
        _prep_kernel,
        grid=(m // PREP_T,),
        in_specs=[
            pl.BlockSpec((PREP_T, ATT_WIDTH), lambda i: (i, OFF_Q // ATT_WIDTH)),
            pl.BlockSpec((PREP_T, KV_WIDTH), lambda i: (i, OFF_K // KV_WIDTH)),
            pl.BlockSpec((PREP_T, KV_WIDTH), lambda i: (i, OFF_V // KV_WIDTH)),
            pl.BlockSpec((PREP_T, IDX_HEADS * IDX_DIM), lambda i: (i, OFF_IQ // (IDX_HEADS * IDX_DIM))),
            pl.BlockSpec((PREP_T, LANES), lambda i: (i, 0)),
            pl.BlockSpec((None, 1, HEAD_DIM), lambda i: (lj, 0, 0)),
            pl.BlockSpec((None, 1, HEAD_DIM), lambda i: (lj, 0, 0)),
        ],
        out_specs=[
            pl.BlockSpec((1, ATT_WIDTH, PREP_T), bt),
            pl.BlockSpec((PREP_T, KV_WIDTH), lambda i: (i, 0)),
            pl.BlockSpec((1, KV_WIDTH, PREP_T), bt),
            pl.BlockSpec((1, IDX_HEADS * IDX_DIM, PREP_T), bt),
            pl.BlockSpec((PREP_T, IDX_DIM), lambda i: (i, 0)),
            pl.BlockSpec((1, IDX_HEADS, PREP_T), bt),
        ],
        out_shape=[
            jax.ShapeDtypeStruct((batch, ATT_WIDTH, seq), bf16),
            jax.ShapeDtypeStruct((m, KV_WIDTH), bf16),
            jax.ShapeDtypeStruct((batch, KV_WIDTH, seq), bf16),
            jax.ShapeDtypeStruct((batch, IDX_HEADS * IDX_DIM, seq), bf16),
            jax.ShapeDtypeStruct((m, IDX_DIM), bf16),
            jax.ShapeDtypeStruct((batch, IDX_HEADS, seq), f32),
        ],
        compiler_params=_cparams(("parallel",)),
        name="attn_prep",
    )(proj, proj, proj, proj, tail, q_gains.reshape(ne, 1, HEAD_DIM), k_gains.reshape(ne, 1, HEAD_DIM))


BAND_ROWS = 3 * KT
FAR_BUCKET = N_BUCKETS // 2 - 1


def _band_kernel(tab_ref, o_ref):
    jj = lax.broadcasted_iota(i32, (BAND_ROWS, QB), 0)
    qi = lax.broadcasted_iota(i32, (BAND_ROWS, QB), 1)
    rel = jnp.where(jj < KT, -2 * KT, jj - 2 * KT - qi)
    n = jnp.abs(rel)
    large = jnp.full(rel.shape, 8, i32)
    for thr in (12, 16, 23, 32, 46, 64, 91):
        large = large + (n >= thr).astype(i32)
    bucket = jnp.where(rel > 0, N_BUCKETS // 2, 0) + jnp.where(n < 8, n, large)

    def per_head(h, carry):
        acc = jnp.zeros((BAND_ROWS, QB), f32)
        for b in range(N_BUCKETS):
            acc = jnp.where(bucket == b, tab_ref[b, h], acc)
        o_ref[h] = (acc - tab_ref[FAR_BUCKET, h]) * LOG2E
        return carry

    lax.fori_loop(0, ATT_HEADS, per_head, 0)


def _band(rel_bias):
    return pl.pallas_call(
        _band_kernel,
        in_specs=[pl.BlockSpec(memory_space=pltpu.SMEM)],
        out_specs=pl.BlockSpec(memory_space=pltpu.VMEM),
        out_shape=jax.ShapeDtypeStruct((ATT_HEADS, BAND_ROWS, QB), f32),
        name="bias_band",
    )(rel_bias)


def _dsa_kernel(qT_ref, k_ref, vT_ref, ik_ref, iqT_ref, iwT_ref, gb_ref, band_ref, o_ref,
                key_ref, madd_ref, eye_ref, bound_ref, *state_refs, topk):
    m_refs = state_refs[:KV_HEADS]
    acc_refs = state_refs[KV_HEADS:]
    i = pl.program_id(1)
    nb = i // TILES_PER_BLOCK + 1
    t0 = i * QB
    qpos = t0 + lax.broadcasted_iota(i32, (1, QB), 1)
    limit = (qpos // CHUNK + 1) * CHUNK
    sub_iota = lax.broadcasted_iota(i32, (KB, QB), 0)
    idx_scale = (IDX_DIM ** -0.5) * (IDX_HEADS ** -0.5)
    int_min = jnp.int32(-2 ** 31)

    def score_block(j, carry):
        r0 = pl.multiple_of(j * KB, KB)
        ks = ik_ref[0, pl.ds(r0, KB), :]
        acc = jnp.zeros((KB, QB), f32)
        for hp in range(IDX_HEADS // 2):
            h0, h1 = 2 * hp, 2 * hp + 1
            w = jnp.concatenate([iqT_ref[0, h0 * IDX_DIM:(h0 + 1) * IDX_DIM, :],
                                 iqT_ref[0, h1 * IDX_DIM:(h1 + 1) * IDX_DIM, :]], axis=1)
            s = jnp.dot(ks, w, preferred_element_type=f32)
            acc = acc + jnp.maximum(s[:, :QB], 0.0) * iwT_ref[0, h0:h0 + 1, :]
            acc = acc + jnp.maximum(s[:, QB:], 0.0) * iwT_ref[0, h1:h1 + 1, :]
        score = acc * idx_scale
        score = jnp.where(score == 0.0, 0.0, score)
        score = jnp.where(r0 + sub_iota < limit, score, -jnp.inf)
        bits = pltpu.bitcast(score, i32)
        key_ref[pl.ds(r0, KB), :] = jnp.where(bits >= 0, bits, bits ^ jnp.int32(0x7FFFFFFF))
        return carry

    lax.fori_loop(0, nb, score_block, 0)

    def count_rows(pred_fn):
        def body(j, cnt8):
            r0 = pl.multiple_of(j * KB, KB)
            hit = pred_fn(key_ref[pl.ds(r0, KB), :], r0).astype(i32)
            return cnt8 + jnp.sum(hit.reshape(KB // SUBLANES, SUBLANES, QB), axis=0)
        cnt8 = lax.fori_loop(0, nb, body, jnp.zeros((SUBLANES, QB), i32))
        return jnp.sum(cnt8, axis=0, keepdims=True)

    n_nonneg = count_rows(lambda kt, r0: kt >= 0)
    nonneg = n_nonneg >= topk
    prefix0 = jnp.where(nonneg, jnp.int32(0), int_min)
    cur0 = jnp.where(nonneg, n_nonneg, nb * KB)
    n_bits = 31
    bits_per_check = 4

    def bit_pass(b, prefix, cur):
        shift = jnp.maximum(n_bits - 1 - b, 0)
        bit = jnp.where(b < n_bits, lax.shift_left(jnp.int32(1), shift), 0)
        cand = prefix | bit
        cnt = count_rows(lambda kt, r0: kt >= cand)
        take = cnt >= topk
        return jnp.where(take, cand, prefix), jnp.where(take, cnt, cur)

    prefix1, cur1 = lax.fori_loop(0, RADIX_FIXED_BITS, lambda b, pc: bit_pass(b, *pc), (prefix0, cur0))

    def bits_cond(state):
        b, _, _, more = state
        return jnp.logical_and(b < n_bits, more > 0)

    def bits_body(state):
        b, prefix, cur, _ = state
        for u in range(bits_per_check):
            prefix, cur = bit_pass(b + u, prefix, cur)
        more = jnp.max((cur > topk).astype(i32))
        return b + bits_per_check, prefix, cur, more

    more1 = jnp.max((cur1 > topk).astype(i32))
    _, tau, cur, _ = lax.while_loop(bits_cond, bits_body, (jnp.int32(RADIX_FIXED_BITS), prefix1, cur1, more1))

    neg_inf_key = jnp.int32(-2 ** 31 + 0x7FFFFF)
    excess = jnp.logical_and(cur > topk, tau > neg_inf_key)
    any_excess = jnp.max(excess.astype(i32)) > 0
    n_gt = lax.cond(any_excess, lambda: count_rows(lambda kt, r0: kt > tau), lambda: jnp.zeros((1, QB), i32))
    room = topk - n_gt
    n_iter = jnp.where(any_excess, 13, 0)

    def cut_step(b, cut):
        cand = cut | lax.shift_left(jnp.int32(1), 12 - b)
        cnt = count_rows(lambda kt, r0: jnp.logical_and(kt == tau, r0 + sub_iota < cand))
        return jnp.where(cnt <= room, cand, cut)

    cut = lax.fori_loop(0, n_iter, cut_step, jnp.zeros((1, QB), i32))
    cut = jnp.where(excess, cut, jnp.int32(2 ** 30))

    def mask_block(j, carry):
        r0 = pl.multiple_of(j * KB, KB)
        kt = key_ref[pl.ds(r0, KB), :]
        pos = r0 + sub_iota
        sel = jnp.logical_or(kt > tau, jnp.logical_and(kt == tau, pos < cut))
        sel = jnp.logical_and(sel, pos < limit)
        madd_ref[pl.ds(r0, KB), :] = jnp.where(sel, 0.0, NEG_BIG).astype(bf16)
        return carry

    lax.fori_loop(0, nb, mask_block, 0)

    @pl.when(i == 0)
    def _():
        rowi = lax.broadcasted_iota(i32, (QB, REP * QB), 0)
        coli = lax.broadcasted_iota(i32, (QB, REP * QB), 1)
        eye_ref[...] = ((coli & (QB - 1)) == rowi).astype(bf16)
        for g in range(KV_HEADS):
            def kmax_body(j, mx, g=g):
                r0 = pl.multiple_of(j * KB, KB)
                kf = k_ref[0, pl.ds(r0, KB), g * HEAD_DIM:(g + 1) * HEAD_DIM].astype(f32)
                n2 = jnp.sum(kf * kf, axis=-1, keepdims=True)
                return jnp.maximum(mx, jnp.max(n2, axis=0, keepdims=True))
            k2 = lax.fori_loop(0, k_ref.shape[1] // KB, kmax_body, jnp.zeros((1, 1), f32))
            bound_ref[g:g + 1, :] = jnp.broadcast_to(k2, (1, QB))
        bmax = jnp.max(jnp.abs(band_ref[...]).reshape(ATT_HEADS * BAND_ROWS, QB), axis=0, keepdims=True)
        room = FAST_LOGIT_LIMIT - jnp.max(bmax, axis=1, keepdims=True)
        bound_ref[KV_HEADS:KV_HEADS + 1, :] = jnp.broadcast_to(jnp.where(room > 0.0, room * room, -1.0), (1, QB))

    over = jnp.zeros((1, QB), i32)
    for g in range(KV_HEADS):
        q2 = jnp.zeros((1, QB), f32)
        for h in range(g * REP, (g + 1) * REP):
            qh = qT_ref[0, h * HEAD_DIM:(h + 1) * HEAD_DIM, :].astype(f32)
            q2 = jnp.maximum(q2, jnp.sum(qh * qh, axis=0, keepdims=True))
        over = over | (q2 * bound_ref[g:g + 1, :] * 1.002 > bound_ref[KV_HEADS:KV_HEADS + 1, :]).astype(i32)
    needs_shift = jnp.max(over) > 0

    for m_ref, acc_ref in zip(m_refs, acc_refs):
        m_ref[...] = jnp.full(m_ref.shape, NEG_BIG, f32)
        acc_ref[...] = jnp.zeros(acc_ref.shape, f32)

    ones_rows = jnp.ones((SUM_ROWS, KB), bf16)

    def att_block(j, carry, near, online):
        r0 = pl.multiple_of(j * KB, KB)

        def qk(g):
            qg = jnp.concatenate([qT_ref[0, h * HEAD_DIM:(h + 1) * HEAD_DIM, :]
                                  for h in range(g * REP, (g + 1) * REP)], axis=1)
            rhs = jnp.concatenate([qg, eye_ref[...]], axis=0)
            piece = KB // QK_PIECES
            parts = []
            for c in range(QK_PIECES):
                rows = pl.ds(r0 + c * piece, piece)
                lhs = jnp.concatenate([k_ref[0, rows, g * HEAD_DIM:(g + 1) * HEAD_DIM], madd_ref[rows, :]],
                                      axis=1)
                parts.append(jnp.dot(lhs, rhs, preferred_element_type=f32))
            return jnp.concatenate(parts, axis=0)

        ts = {0: qk(0)}
        for g in range(KV_HEADS):
            heads = range(g * REP, (g + 1) * REP)
            if g + 1 < KV_HEADS:
                ts[g + 1] = qk(g + 1)
            t = ts.pop(g)
            if near:
                rows = []
                for u in range(TILES_PER_BLOCK):
                    band_row = pl.multiple_of(jnp.clip(j * TILES_PER_BLOCK + u - i + 2, 0, 2) * KT, KT)
                    rows.append(jnp.concatenate([band_ref[h, pl.ds(band_row, KT), :] for h in heads], axis=1))
                t = t + jnp.concatenate(rows, axis=0)
            vt = jnp.concatenate([vT_ref[0, g * HEAD_DIM:(g + 1) * HEAD_DIM, pl.ds(r0, KB)], ones_rows], axis=0)
            if online:
                m_old = m_refs[g][...]
                m_new = jnp.maximum(m_old, jnp.max(t, axis=0, keepdims=True))
                alpha = jnp.exp2(m_old - m_new)
                p = jnp.exp2(t - m_new)
                m_refs[g][...] = m_new
                acc_refs[g][...] = (acc_refs[g][...] * alpha
                                    + jnp.dot(vt, p.astype(bf16), preferred_element_type=f32))
            else:
                p = jnp.exp2(t)
                acc_refs[g][...] = acc_refs[g][...] + jnp.dot(vt, p.astype(bf16), preferred_element_type=f32)
        return carry

    n_far = jnp.maximum(i - 1, 0) // TILES_PER_BLOCK

    def run_blocks(online):
        lax.fori_loop(0, n_far, functools.partial(att_block, near=False, online=online), 0)
        lax.fori_loop(n_far, nb, functools.partial(att_block, near=True, online=online), 0)

    lax.cond(needs_shift, functools.partial(run_blocks, True), functools.partial(run_blocks, False))

    for g in range(KV_HEADS):
        heads = range(g * REP, (g + 1) * REP)
        oT = acc_refs[g][0:HEAD_DIM, :] / acc_refs[g][HEAD_DIM:HEAD_DIM + 1, :]
        for r, h in enumerate(heads):
            gate = gb_ref[:, h * HEAD_DIM:(h + 1) * HEAD_DIM]
            o = oT[:, r * QB:(r + 1) * QB].T
            o_ref[:, h * HEAD_DIM:(h + 1) * HEAD_DIM] = (o * _silu(gate)).astype(o_ref.dtype)


def _dsa(qT, kn, vT, ik, iqT, iwT, proj, band, *, batch, seq, topk):
    m = proj.shape[0]
    nq = seq // QB
    kn3 = kn.reshape(batch, seq, KV_WIDTH)
    ik3 = ik.reshape(batch, seq, IDX_DIM)
    return pl.pallas_call(
        functools.partial(_dsa_kernel, topk=topk),
        grid=(batch, nq),
        in_specs=[
            pl.BlockSpec((1, ATT_WIDTH, QB), lambda b, i: (b, 0, i)),
            pl.BlockSpec((1, seq, KV_WIDTH), lambda b, i: (b, 0, 0)),
            pl.BlockSpec((1, KV_WIDTH, seq), lambda b, i: (b, 0, 0)),
            pl.BlockSpec((1, seq, IDX_DIM), lambda b, i: (b, 0, 0)),
            pl.BlockSpec((1, IDX_HEADS * IDX_DIM, QB), lambda b, i: (b, 0, i)),
            pl.BlockSpec((1, IDX_HEADS, QB), lambda b, i: (b, 0, i)),
            pl.BlockSpec((QB, ATT_WIDTH), lambda b, i: (b * nq + i, OFF_GB // ATT_WIDTH)),
            pl.BlockSpec((ATT_HEADS, BAND_ROWS, QB), lambda b, i: (0, 0, 0)),
        ],
        out_specs=pl.BlockSpec((QB, ATT_WIDTH), lambda b, i: (b * nq + i, 0)),
        out_shape=jax.ShapeDtypeStruct((m, ATT_WIDTH), bf16),
        scratch_shapes=[
            pltpu.VMEM((seq, QB), i32),
            pltpu.VMEM((seq, QB), bf16),
            pltpu.VMEM((QB, REP * QB), bf16),
            pltpu.VMEM((SUBLANES, QB), f32),
        ] + [pltpu.VMEM((1, REP * QB), f32)] * KV_HEADS
          + [pltpu.VMEM((HEAD_DIM + SUM_ROWS, REP * QB), f32)] * KV_HEADS,
        compiler_params=_cparams(("parallel", "arbitrary")),
        name="dsa",
    )(qT, kn3, vT, ik3, iqT, iwT, proj, band)


def _gelu(x):
    c = math.sqrt(2.0 / math.pi)
    half = 0.5 * x
    return half + half * jnp.tanh(x * (c + (c * 0.044715) * (x * x)))


def _sgu_kernel(u_ref, v_ref, g_ref, lng_ref, lnb_ref, ws_ref, bsg_ref, o_ref, wsm_ref, bs_ref):
    @pl.when(pl.program_id(0) == 0)
    def _():
        tpos = lax.broadcasted_iota(i32, (SG_CHUNK, SG_CHUNK), 0) // CHUNK
        spos = lax.broadcasted_iota(i32, (SG_CHUNK, SG_CHUNK), 1) // CHUNK
        for g in range(SG_GROUPS):
            wsm_ref[g] = jnp.where(tpos >= spos, ws_ref[g], 0.0).astype(bf16)
        bs_ref[...] = bsg_ref[...].T

    v = _gelu(v_ref[...].astype(f32))
    mu = jnp.mean(v, axis=-1, keepdims=True)
    vc = v - mu
    var = jnp.mean(vc * vc, axis=-1, keepdims=True)
    vn = (vc * lax.rsqrt(var + EPS) * lng_ref[...] + lnb_ref[...]).astype(bf16)
    for g in range(SG_GROUPS):
        sl = slice(g * SG_GROUP, (g + 1) * SG_GROUP)
        mixed = jnp.dot(wsm_ref[g], vn[:, sl], preferred_element_type=f32) + bs_ref[:, g:g + 1]
        gate = g_ref[:, sl].astype(f32)
        o_ref[:, sl] = (_gelu(u_ref[:, sl].astype(f32)) * mixed * _silu(gate)).astype(o_ref.dtype)


def _sgu(proj, ln_g, ln_b, w_s, b_s, lj):
    m = proj.shape[0]
    no = ln_g.shape[0]
    return pl.pallas_call(
        _sgu_kernel,
        grid=(m // SG_CHUNK,),
        in_specs=[
            pl.BlockSpec((SG_CHUNK, SG_WIDTH), lambda i: (i, 0)),
            pl.BlockSpec((SG_CHUNK, SG_WIDTH), lambda i: (i, 1)),
            pl.BlockSpec((SG_CHUNK, SG_WIDTH), lambda i: (i, 2)),
            pl.BlockSpec((None, 1, SG_WIDTH), lambda i: (lj, 0, 0)),
            pl.BlockSpec((None, 1, SG_WIDTH), lambda i: (lj, 0, 0)),
            pl.BlockSpec((None, SG_GROUPS, SG_CHUNK, SG_CHUNK), lambda i: (lj, 0, 0, 0)),
            pl.BlockSpec((None, SG_GROUPS, SG_CHUNK), lambda i: (lj, 0, 0)),
        ],
        out_specs=pl.BlockSpec((SG_CHUNK, SG_WIDTH), lambda i: (i, 0)),
        out_shape=jax.ShapeDtypeStruct((m, SG_WIDTH), bf16),
        scratch_shapes=[pltpu.VMEM((SG_GROUPS, SG_CHUNK, SG_CHUNK), bf16), pltpu.VMEM((SG_CHUNK, SG_GROUPS), f32)],
        compiler_params=_cparams(("arbitrary",)),
        name="sgu",
    )(proj, proj, proj, ln_g.reshape(no, 1, SG_WIDTH), ln_b.reshape(no, 1, SG_WIDTH), w_s, b_s)


PROJ_TN = 1024
EVEN_GATE_BLOCKS = (0, 1, 2, 3, 7, 8)
EVEN_ATT_BLOCKS = (4, 5, 6, 9)
ODD_COL_BLOCKS = tuple(range(3 * SG_WIDTH // PROJ_TN))
TAIL_START = 10240


def kernel(x, p, norm_gain, rel_bias, even_w_in, conv_w, conv_b, lru_w_r, lru_b_r, lru_w_i, lru_b_i,
           lru_lambda, q_norm, k_norm, even_w_out, odd_w_in, sg_ln_g, sg_ln_b, sg_w_s, sg_b_s, odd_w_out,
           pe_w, pe_gate_norm, pe_w_gate):
    batch, seq, d = x.shape
    depth = p.shape[0]
    topk = min(TOPK_MAX, seq // 4)
    m = batch * seq
    h = x.reshape(m, d)
    band = _band(rel_bias)
    hn = _norm(h, norm_gain, 0)
    even_w_in_t = jnp.swapaxes(even_w_in, 1, 2)
    wg_b, pew_b = _cast_bf16(pe_w_gate), _cast_bf16(pe_w)
    for layer in range(depth):
        j = layer // 2
        if layer % 2 == 0:
            proj = _proj(hn, even_w_in_t, j, EVEN_GATE_BLOCKS, transposed=True, tn=PROJ_TN)
            proj_att = _proj(hn, even_w_in_t, j, EVEN_ATT_BLOCKS, transposed=True, tn=PROJ_TN, out_dtype=bf16)
            tail = _proj_tail(hn, even_w_in_t, j, TAIL_START)
            ya = _lru(proj, conv_w, conv_b, lru_w_r, lru_b_r, lru_w_i, lru_b_i, lru_lambda, j,
                      batch=batch, seq=seq)
            qT, kn, vT, iqT, ik, iwT = _prep(proj_att, tail, q_norm, k_norm, j, batch=batch, seq=seq)
            yb = _dsa(qT, kn, vT, ik, iqT, iwT, proj, band, batch=batch, seq=seq, topk=topk)
            h = _outproj(h, [ya, yb], even_w_out, j)
        else:
            proj = _proj(hn, odd_w_in, j, ODD_COL_BLOCKS, tn=PROJ_TN, out_dtype=bf16)
            h = _outproj(h, [_sgu(proj, sg_ln_g, sg_ln_b, sg_w_s, sg_b_s, j)], odd_w_out, j, tn=1024)
        nxt = (norm_gain, layer + 1) if layer + 1 < depth else (None, None)
        h, hn = _pe(h, pe_gate_norm, wg_b, p, pew_b, layer, *nxt)
    return h.reshape(batch, seq, d)
```

```python
import functools
import math

import jax
import jax.numpy as jnp
from jax import lax
from jax.experimental import pallas as pl
from jax.experimental.pallas import tpu as pltpu

f32 = jnp.float32
bf16 = jnp.bfloat16
i32 = jnp.int32

D_MODEL = 2048
CHUNK = 64
PE_DIM = 256
EPS = 1e-6
RNN_WIDTH = 2048
RNN_BLOCK = 128
CONV_WIDTH = 4
LRU_C = 8.0
ATT_HEADS = 16
HEAD_DIM = 128
KV_HEADS = 4
REP = ATT_HEADS // KV_HEADS
ATT_WIDTH = 2048
KV_WIDTH = 512
IDX_HEADS = 16
IDX_DIM = 64
TOPK_MAX = 256
N_BUCKETS = 32
SG_CHUNK = 128
SG_GROUPS = 16
SG_GROUP = 128
SG_WIDTH = 2048

LANES = 128
SUBLANES = 8
QB = 128
KT = 128
KB = 512
TILES_PER_BLOCK = KB // KT
QK_PIECES = 4
SUM_ROWS = 16
LOG2E = math.log2(math.e)
ATT_C2 = HEAD_DIM ** -0.5 * LOG2E
VMEM_LIMIT = 52 * 1024 * 1024
NEG_BIG = -1e30
FAST_LOGIT_LIMIT = 60.0

OFF_XA, OFF_GA, OFF_GB = 0, 2048, 4096
OFF_Q, OFF_K, OFF_V, OFF_IQ = 0, 2048, 2560, 3072


def _cparams(sem):
    return pltpu.CompilerParams(dimension_semantics=sem, vmem_limit_bytes=VMEM_LIMIT)


def _sigmoid(x):
    return 0.5 * jnp.tanh(0.5 * x) + 0.5


def _silu(x):
    return x * _sigmoid(x)


def _block_lookup(table):
    def f(j):
        out = jnp.int32(table[-1])
        for idx in range(len(table) - 2, -1, -1):
            out = jnp.where(j == idx, table[idx], out)
        return out
    return f


def _rms(x, gain):
    ms = jnp.mean(x * x, axis=-1, keepdims=True)
    return x * lax.rsqrt(ms + EPS) * gain


def _norm_kernel(h_ref, g_ref, o_ref):
    o_ref[...] = _rms(h_ref[...], g_ref[...]).astype(o_ref.dtype)


def _norm(h, gains, layer, *, tm=512):
    m, d = h.shape
    return pl.pallas_call(
        _norm_kernel,
        grid=(m // tm,),
        in_specs=[pl.BlockSpec((tm, d), lambda i: (i, 0)),
                  pl.BlockSpec((None, 1, d), lambda i: (layer, 0, 0))],
        out_specs=pl.BlockSpec((tm, d), lambda i: (i, 0)),
        out_shape=jax.ShapeDtypeStruct((m, d), bf16),
        compiler_params=_cparams(("parallel",)),
        name="norm",
    )(h, gains.reshape(gains.shape[0], 1, d))


_NT = (((1,), (1,)), ((), ()))


def _proj_kernel(hn_ref, w_ref, o_ref, wb_ref, *, transposed):
    @pl.when(pl.program_id(1) == 0)
    def _():
        wb_ref[...] = w_ref[...].astype(bf16)

    if transposed:
        acc = lax.dot_general(hn_ref[...], wb_ref[...], _NT, preferred_element_type=f32)
    else:
        acc = jnp.dot(hn_ref[...], wb_ref[...], preferred_element_type=f32)
    o_ref[...] = acc.astype(o_ref.dtype)


def _proj(hn, ws, widx, col_blocks, *, transposed=False, tm=1024, tn=1024, out_dtype=f32):
    m, d = hn.shape
    n = len(col_blocks) * tn
    colmap = _block_lookup(col_blocks)
    if transposed:
        w_spec = pl.BlockSpec((None, tn, d), lambda j, i: (widx, colmap(j), 0))
        wb_shape = (tn, d)
    else:
        w_spec = pl.BlockSpec((None, d, tn), lambda j, i: (widx, 0, colmap(j)))
        wb_shape = (d, tn)
    return pl.pallas_call(
        functools.partial(_proj_kernel, transposed=transposed),
        grid=(n // tn, m // tm),
        in_specs=[pl.BlockSpec((tm, d), lambda j, i: (i, 0)), w_spec],
        out_specs=pl.BlockSpec((tm, tn), lambda j, i: (i, j)),
        out_shape=jax.ShapeDtypeStruct((m, n), out_dtype),
        scratch_shapes=[pltpu.VMEM(wb_shape, bf16)],
        compiler_params=_cparams(("parallel", "arbitrary")),
        name="proj",
    )(hn, ws)


def _proj_tail_kernel(hn_ref, wt_ref, o_ref, *, tail_rows):
    row = lax.broadcasted_iota(i32, wt_ref.shape, 0)
    wt = jnp.where(row < tail_rows, wt_ref[...], 0.0).astype(bf16)
    o_ref[...] = lax.dot_general(hn_ref[...], wt, _NT, preferred_element_type=f32)


def _proj_tail(hn, wts, widx, tail_start, *, tm=1024):
    m, d = hn.shape
    tail_rows = wts.shape[1] - tail_start
    assert 0 < tail_rows < LANES and tail_start % LANES == 0
    return pl.pallas_call(
        functools.partial(_proj_tail_kernel, tail_rows=tail_rows),
        grid=(m // tm,),
        in_specs=[
            pl.BlockSpec((tm, d), lambda i: (i, 0)),
            pl.BlockSpec((None, LANES, d), lambda i: (widx, tail_start // LANES, 0)),
        ],
        out_specs=pl.BlockSpec((tm, LANES), lambda i: (i, 0)),
        out_shape=jax.ShapeDtypeStruct((m, LANES), f32),
        compiler_params=_cparams(("parallel",)),
        name="proj_tail",
    )(hn, wts)


def _cast_kernel(w_ref, o_ref):
    o_ref[...] = w_ref[...].astype(o_ref.dtype)


def _cast_bf16(w, *, tr=512):
    lead, rows, cols = w.shape
    tr = min(tr, rows)
    return pl.pallas_call(
        _cast_kernel,
        grid=(lead, rows // tr),
        in_specs=[pl.BlockSpec((None, tr, cols), lambda a, r: (a, r, 0))],
        out_specs=pl.BlockSpec((None, tr, cols), lambda a, r: (a, r, 0)),
        out_shape=jax.ShapeDtypeStruct(w.shape, bf16),
        compiler_params=_cparams(("parallel", "parallel")),
        name="cast_bf16",
    )(w)


def _outproj_kernel(*refs, n_y):
    h_ref = refs[0]
    y_refs = refs[1:1 + n_y]
    w_refs = refs[1 + n_y:1 + 2 * n_y]
    o_ref = refs[1 + 2 * n_y]
    wb_refs = refs[2 + 2 * n_y:]

    @pl.when(pl.program_id(1) == 0)
    def _():
        for w_ref, wb_ref in zip(w_refs, wb_refs):
            wb_ref[...] = w_ref[...].astype(bf16)

    acc = h_ref[...]
    for y_ref, wb_ref in zip(y_refs, wb_refs):
        acc = acc + jnp.dot(y_ref[...], wb_ref[...], preferred_element_type=f32)
    o_ref[...] = acc


def _outproj(h, ys, ws, widx, *, tm=1024, tn=512):
    m, d = h.shape
    n_y = len(ys)
    kw = ys[0].shape[1]
    in_specs = [pl.BlockSpec((tm, tn), lambda j, i: (i, j))]
    in_specs += [pl.BlockSpec((tm, kw), lambda j, i: (i, 0)) for _ in ys]
    in_specs += [pl.BlockSpec((None, kw, tn), lambda j, i, c=c: (widx, c, j)) for c in range(n_y)]
    return pl.pallas_call(
        functools.partial(_outproj_kernel, n_y=n_y),
        grid=(d // tn, m // tm),
        in_specs=in_specs,
        out_specs=pl.BlockSpec((tm, tn), lambda j, i: (i, j)),
        out_shape=jax.ShapeDtypeStruct((m, d), f32),
        scratch_shapes=[pltpu.VMEM((kw, tn), bf16)] * n_y,
        compiler_params=_cparams(("parallel", "arbitrary")),
        name="outproj",
    )(h, *ys, *([ws] * n_y))


def _pe_kernel(h_ref, gn_ref, wg_ref, p_ref, pew_ref, *rest, tn, emit_next):
    if emit_next:
        gnext_ref, o_ref, hnext_ref, hn_ref, pb_ref, orow_ref = rest
    else:
        o_ref, hn_ref, pb_ref = rest
    j = pl.program_id(1)

    @pl.when(j == 0)
    def _():
        hn_ref[...] = _rms(h_ref[...], gn_ref[...]).astype(bf16)
        pb_ref[...] = p_ref[...].astype(bf16)

    cols = pl.ds(pl.multiple_of(j * tn, tn), tn)
    gate = _sigmoid(jnp.dot(hn_ref[...], wg_ref[...], preferred_element_type=f32))
    e = jnp.dot(pb_ref[...], pew_ref[...], preferred_element_type=f32)
    out = h_ref[:, cols] + gate * e
    o_ref[...] = out

    if emit_next:
        orow_ref[:, cols] = out

        @pl.when(j == pl.num_programs(1) - 1)
        def _():
            hnext_ref[...] = _rms(orow_ref[...], gnext_ref[...]).astype(bf16)


def _pe(h, gns, wgs, p, pews, layer, next_gains=None, next_layer=None, *, tm=1024, tn=512):
    m, d = h.shape
    depth = p.shape[0]
    emit_next = next_gains is not None
    in_specs = [
        pl.BlockSpec((tm, d), lambda i, j: (i, 0)),
        pl.BlockSpec((None, 1, d), lambda i, j: (layer, 0, 0)),
        pl.BlockSpec((None, d, tn), lambda i, j: (layer, 0, j)),
        pl.BlockSpec((None, tm, PE_DIM), lambda i, j: (layer, i, 0)),
        pl.BlockSpec((None, PE_DIM, tn), lambda i, j: (layer, 0, j)),
    ]
    args = [h, gns.reshape(depth, 1, d), wgs, p.reshape(depth, m, PE_DIM), pews]
    out_specs = [pl.BlockSpec((tm, tn), lambda i, j: (i, j))]
    out_shape = [jax.ShapeDtypeStruct((m, d), f32)]
    scratch = [pltpu.VMEM((tm, d), bf16), pltpu.VMEM((tm, PE_DIM), bf16)]
    if emit_next:
        in_specs.append(pl.BlockSpec((None, 1, d), lambda i, j: (next_layer, 0, 0)))
        args.append(next_gains.reshape(next_gains.shape[0], 1, d))
        out_specs.append(pl.BlockSpec((tm, d), lambda i, j: (i, 0)))
        out_shape.append(jax.ShapeDtypeStruct((m, d), bf16))
        scratch.append(pltpu.VMEM((tm, d), f32))
    res = pl.pallas_call(
        functools.partial(_pe_kernel, tn=tn, emit_next=emit_next),
        grid=(m // tm, d // tn),
        in_specs=in_specs,
        out_specs=out_specs,
        out_shape=out_shape,
        scratch_shapes=scratch,
        compiler_params=_cparams(("parallel", "arbitrary")),
        name="pe",
    )(*args)
    return res if emit_next else (res[0], None)


LRU_TT = 1024
LRU_CB = 512


def _lru_kernel(x_ref, ga_ref, cw_ref, cb_ref, wr_ref, br_ref, wi_ref, bi_ref, lam_ref, o_ref,
                xext_ref, hcar_ref, a_ref, b_ref):
    tstep = pl.program_id(2)
    pad = SUBLANES

    @pl.when(tstep == 0)
    def _():
        xext_ref[0:pad, :] = jnp.zeros((pad, LRU_CB), f32)
        hcar_ref[...] = jnp.zeros_like(hcar_ref)

    @pl.when(tstep > 0)
    def _():
        xext_ref[0:pad, :] = xext_ref[LRU_TT:LRU_TT + pad, :]

    xext_ref[pad:, :] = x_ref[...]

    xc = cb_ref[...] + cw_ref[CONV_WIDTH - 1:CONV_WIDTH, :] * x_ref[...]
    for j in range(CONV_WIDTH - 1):
        sh = CONV_WIDTH - 1 - j
        xc = xc + cw_ref[j:j + 1, :] * xext_ref[pad - sh:pad - sh + LRU_TT, :]

    xcb = xc.astype(bf16)
    sp = jax.nn.softplus(-lam_ref[...])
    nb = LRU_CB // RNN_BLOCK
    for g in range(nb):
        sl = slice(g * RNN_BLOCK, (g + 1) * RNN_BLOCK)
        xg = xcb[:, sl]
        r = _sigmoid(jnp.dot(xg, wr_ref[g].astype(bf16), preferred_element_type=f32) + br_ref[:, sl])
        ig = _sigmoid(jnp.dot(xg, wi_ref[g].astype(bf16), preferred_element_type=f32) + bi_ref[:, sl])
        log_a = (-LRU_C) * r * sp[:, sl]
        a = jnp.exp(log_a)
        z = jnp.tanh(-log_a) * (1.0 + a * a)
        mult = jnp.where(z > 0.0, z * lax.rsqrt(z), 0.0)
        a_ref[:, sl] = a
        b_ref[:, sl] = mult * ig * xc[:, sl]

    nchunk = LRU_TT // SUBLANES
    a3 = a_ref[...].reshape(nchunk, SUBLANES, LRU_CB)
    b3 = b_ref[...].reshape(nchunk, SUBLANES, LRU_CB)
    row = lax.broadcasted_iota(i32, a3.shape, 1)
    d = 1
    while d < SUBLANES:
        keep = row >= d
        a_sh = jnp.where(keep, pltpu.roll(a3, d, axis=1), 1.0)
        b_sh = jnp.where(keep, pltpu.roll(b3, d, axis=1), 0.0)
        b3 = a3 * b_sh + b3
        a3 = a3 * a_sh
        d *= 2
    a_ref[...] = a3.reshape(LRU_TT, LRU_CB)
    b_ref[...] = b3.reshape(LRU_TT, LRU_CB)

    def carry(c, hprev):
        r0 = pl.multiple_of(c * SUBLANES, SUBLANES)
        h8 = a_ref[pl.ds(r0, SUBLANES), :] * hprev + b_ref[pl.ds(r0, SUBLANES), :]
        gate = ga_ref[pl.ds(r0, SUBLANES), :]
        o_ref[pl.ds(r0, SUBLANES), :] = (h8 * _silu(gate)).astype(o_ref.dtype)
        return jnp.broadcast_to(h8[SUBLANES - 1:SUBLANES, :], (SUBLANES, LRU_CB))

    hcar_ref[...] = lax.fori_loop(0, nchunk, carry, hcar_ref[...], unroll=8)


def _lru(proj, cw, cb, wr, br, wi, bi, lam, lj, *, batch, seq):
    m = proj.shape[0]
    ne = cw.shape[0]
    nt = seq // LRU_TT
    ncb = RNN_WIDTH // LRU_CB
    gpb = LRU_CB // RNN_BLOCK
    ga_off = OFF_GA // LRU_CB
    row = lambda b, c, t: b * nt + t
    vec = pl.BlockSpec((None, 1, LRU_CB), lambda b, c, t: (lj, 0, c))
    gate_w = pl.BlockSpec((None, gpb, RNN_BLOCK, RNN_BLOCK), lambda b, c, t: (lj, c, 0, 0))
    as_vec = lambda a: a.reshape(ne, 1, RNN_WIDTH)
    return pl.pallas_call(
        _lru_kernel,
        grid=(batch, ncb, nt),
        in_specs=[
            pl.BlockSpec((LRU_TT, LRU_CB), lambda b, c, t: (row(b, c, t), c)),
            pl.BlockSpec((LRU_TT, LRU_CB), lambda b, c, t: (row(b, c, t), ga_off + c)),
            pl.BlockSpec((None, CONV_WIDTH, LRU_CB), lambda b, c, t: (lj, 0, c)),
            vec, gate_w, vec, gate_w, vec, vec,
        ],
        out_specs=pl.BlockSpec((LRU_TT, LRU_CB), lambda b, c, t: (row(b, c, t), c)),
        out_shape=jax.ShapeDtypeStruct((m, RNN_WIDTH), bf16),
        scratch_shapes=[
            pltpu.VMEM((LRU_TT + SUBLANES, LRU_CB), f32),
            pltpu.VMEM((SUBLANES, LRU_CB), f32),
            pltpu.VMEM((LRU_TT, LRU_CB), f32),
            pltpu.VMEM((LRU_TT, LRU_CB), f32),
        ],
        compiler_params=_cparams(("parallel", "parallel", "arbitrary")),
        name="lru",
    )(proj, proj, cw, as_vec(cb), wr, as_vec(br), wi, as_vec(bi), as_vec(lam))


PREP_T = 512


def _head_rms(x, gain):
    ms = jnp.mean(x * x, axis=-1, keepdims=True)
    return x * lax.rsqrt(ms + EPS) * gain


def _prep_kernel(q_ref, k_ref, v_ref, iq_ref, tail_ref, qg_ref, kg_ref,
                 qT_ref, kn_ref, vT_ref, iqT_ref, ik_ref, iwT_ref, qn2_ref):
    for h in range(ATT_HEADS):
        sl = slice(h * HEAD_DIM, (h + 1) * HEAD_DIM)
        qt = (_head_rms(q_ref[:, sl].astype(f32), qg_ref[...]) * ATT_C2).T.astype(bf16)
        qT_ref[0, sl, :] = qt
        qf = qt.astype(f32)
        qn2_ref[0, h:h + 1, :] = jnp.sum(qf * qf, axis=0, keepdims=True)
    for g in range(KV_HEADS):
        sl = slice(g * HEAD_DIM, (g + 1) * HEAD_DIM)
        kn_ref[:, sl] = _head_rms(k_ref[:, sl].astype(f32), kg_ref[...]).astype(bf16)
        vT_ref[0, sl, :] = v_ref[:, sl].T
    for c in range(IDX_HEADS * IDX_DIM // LANES):
        sl = slice(c * LANES, (c + 1) * LANES)
        iqT_ref[0, sl, :] = iq_ref[:, sl].T
    tail_t = tail_ref[...].T
    ik_ref[...] = tail_ref[:, 0:IDX_DIM].astype(bf16)
    iwT_ref[0] = tail_t[IDX_DIM:IDX_DIM + IDX_HEADS, :]


def _prep(proj, tail, q_gains, k_gains, lj, *, batch, seq):
    m = proj.shape[0]
    ne = q_gains.shape[0]
    nt = seq // PREP_T
    bt = lambda i: (i // nt, 0, i % nt)
    return pl.pallas_call(
        _prep_kernel,
        grid=(m // PREP_T,),
        in_specs=[
            pl.BlockSpec((PREP_T, ATT_WIDTH), lambda i: (i, OFF_Q // ATT_WIDTH)),
            pl.BlockSpec((PREP_T, KV_WIDTH), lambda i: (i, OFF_K // KV_WIDTH)),
            pl.BlockSpec((PREP_T, KV_WIDTH), lambda i: (i, OFF_V // KV_WIDTH)),
            pl.BlockSpec((PREP_T, IDX_HEADS * IDX_DIM), lambda i: (i, OFF_IQ // (IDX_HEADS * IDX_DIM))),
            pl.BlockSpec((PREP_T, LANES), lambda i: (i, 0)),
            pl.BlockSpec((None, 1, HEAD_DIM), lambda i: (lj, 0, 0)),
            pl.BlockSpec((None, 1, HEAD_DIM), lambda i: (lj, 0, 0)),
        ],
        out_specs=[
            pl.BlockSpec((1, ATT_WIDTH, PREP_T), bt),
            pl.BlockSpec((PREP_T, KV_WIDTH), lambda i: (i, 0)),
            pl.BlockSpec((1, KV_WIDTH, PREP_T), bt),
            pl.BlockSpec((1, IDX_HEADS * IDX_DIM, PREP_T), bt),
            pl.BlockSpec((PREP_T, IDX_DIM), lambda i: (i, 0)),
            pl.BlockSpec((1, IDX_HEADS, PREP_T), bt),
            pl.BlockSpec((1, ATT_HEADS, PREP_T), bt),
        ],
        out_shape=[
            jax.ShapeDtypeStruct((batch, ATT_WIDTH, seq), bf16),
            jax.ShapeDtypeStruct((m, KV_WIDTH), bf16),
            jax.ShapeDtypeStruct((batch, KV_WIDTH, seq), bf16),
            jax.ShapeDtypeStruct((batch, IDX_HEADS * IDX_DIM, seq), bf16),
            jax.ShapeDtypeStruct((m, IDX_DIM), bf16),
            jax.ShapeDtypeStruct((batch, IDX_HEADS, seq), f32),
            jax.ShapeDtypeStruct((batch, ATT_HEADS, seq), f32),
        ],
        compiler_params=_cparams(("parallel",)),
        name="attn_prep",
    )(proj, proj, proj, proj, tail, q_gains.reshape(ne, 1, HEAD_DIM), k_gains.reshape(ne, 1, HEAD_DIM))


BAND_ROWS = 3 * KT
FAR_BUCKET = N_BUCKETS // 2 - 1


def _band_kernel(tab_ref, o_ref):
    jj = lax.broadcasted_iota(i32, (BAND_ROWS, QB), 0)
    qi = lax.broadcasted_iota(i32, (BAND_ROWS, QB), 1)
    rel = jnp.where(jj < KT, -2 * KT, jj - 2 * KT - qi)
    n = jnp.abs(rel)
    large = jnp.full(rel.shape, 8, i32)
    for thr in (12, 16, 23, 32, 46, 64, 91):
        large = large + (n >= thr).astype(i32)
    bucket = jnp.where(rel > 0, N_BUCKETS // 2, 0) + jnp.where(n < 8, n, large)

    def per_head(h, carry):
        acc = jnp.zeros((BAND_ROWS, QB), f32)
        for b in range(N_BUCKETS):
            acc = jnp.where(bucket == b, tab_ref[b, h], acc)
        o_ref[h] = (acc - tab_ref[FAR_BUCKET, h]) * LOG2E
        return carry

    lax.fori_loop(0, ATT_HEADS, per_head, 0)


def _band(rel_bias):
    return pl.pallas_call(
        _band_kernel,
        in_specs=[pl.BlockSpec(memory_space=pltpu.SMEM)],
        out_specs=pl.BlockSpec(memory_space=pltpu.VMEM),
        out_shape=jax.ShapeDtypeStruct((ATT_HEADS, BAND_ROWS, QB), f32),
        name="bias_band",
    )(rel_bias)


def _dsa_kernel(qT_ref, k_ref, vT_ref, ik_ref, iqT_ref, iwT_ref, gb_ref, band_ref, qn2_ref, o_ref,
                key_ref, madd_ref, eye_ref, shift_ref, *state_refs, topk):
    m_refs = state_refs[:KV_HEADS]
    acc_refs = state_refs[KV_HEADS:]
    i = pl.program_id(1)
    nb = i // TILES_PER_BLOCK + 1
    t0 = i * QB
    qpos = t0 + lax.broadcasted_iota(i32, (1, QB), 1)
    limit = (qpos // CHUNK + 1) * CHUNK
    sub_iota = lax.broadcasted_iota(i32, (KB, QB), 0)
    idx_scale = (IDX_DIM ** -0.5) * (IDX_HEADS ** -0.5)
    int_min = jnp.int32(-2 ** 31)

    def score_block(j, carry):
        r0 = pl.multiple_of(j * KB, KB)
        ks = ik_ref[0, pl.ds(r0, KB), :]
        acc = jnp.zeros((KB, QB), f32)
        for hp in range(IDX_HEADS // 2):
            h0, h1 = 2 * hp, 2 * hp + 1
            w = jnp.concatenate([iqT_ref[0, h0 * IDX_DIM:(h0 + 1) * IDX_DIM, :],
                                 iqT_ref[0, h1 * IDX_DIM:(h1 + 1) * IDX_DIM, :]], axis=1)
            s = jnp.dot(ks, w, preferred_element_type=f32)
            acc = acc + jnp.maximum(s[:, :QB], 0.0) * iwT_ref[0, h0:h0 + 1, :]
            acc = acc + jnp.maximum(s[:, QB:], 0.0) * iwT_ref[0, h1:h1 + 1, :]
        score = acc * idx_scale
        score = jnp.where(score == 0.0, 0.0, score)
        score = jnp.where(r0 + sub_iota < limit, score, -jnp.inf)
        bits = pltpu.bitcast(score, i32)
        key_ref[pl.ds(r0, KB), :] = jnp.where(bits >= 0, bits, bits ^ jnp.int32(0x7FFFFFFF))
        return carry

    lax.fori_loop(0, nb, score_block, 0)

    def count_rows(pred_fn):
        def body(j, cnt8):
            r0 = pl.multiple_of(j * KB, KB)
            hit = pred_fn(key_ref[pl.ds(r0, KB), :], r0).astype(i32)
            return cnt8 + jnp.sum(hit.reshape(KB // SUBLANES, SUBLANES, QB), axis=0)
        cnt8 = lax.fori_loop(0, nb, body, jnp.zeros((SUBLANES, QB), i32))
        return jnp.sum(cnt8, axis=0, keepdims=True)

    n_nonneg = count_rows(lambda kt, r0: kt >= 0)
    nonneg = n_nonneg >= topk
    prefix0 = jnp.where(nonneg, jnp.int32(0), int_min)
    cur0 = jnp.where(nonneg, n_nonneg, nb * KB)
    n_bits = 31

    def bit_pass(b, state):
        prefix, cur = state
        cand = prefix | lax.shift_left(jnp.int32(1), n_bits - 1 - b)
        cnt = count_rows(lambda kt, r0: kt >= cand)
        take = cnt >= topk
        return jnp.where(take, cand, prefix), jnp.where(take, cnt, cur)

    tau, cur = lax.fori_loop(0, n_bits, bit_pass, (prefix0, cur0))

    neg_inf_key = jnp.int32(-2 ** 31 + 0x7FFFFF)
    excess = jnp.logical_and(cur > topk, tau > neg_inf_key)
    any_excess = jnp.max(excess.astype(i32)) > 0
    n_gt = lax.cond(any_excess, lambda: count_rows(lambda kt, r0: kt > tau), lambda: jnp.zeros((1, QB), i32))
    room = topk - n_gt
    n_iter = jnp.where(any_excess, 13, 0)

    def cut_step(b, cut):
        cand = cut | lax.shift_left(jnp.int32(1), 12 - b)
        cnt = count_rows(lambda kt, r0: jnp.logical_and(kt == tau, r0 + sub_iota < cand))
        return jnp.where(cnt <= room, cand, cut)

    cut = lax.fori_loop(0, n_iter, cut_step, jnp.zeros((1, QB), i32))
    cut = jnp.where(excess, cut, jnp.int32(2 ** 30))

    def mask_block(j, carry):
        r0 = pl.multiple_of(j * KB, KB)
        kt = key_ref[pl.ds(r0, KB), :]
        pos = r0 + sub_iota
        sel = jnp.logical_or(kt > tau, jnp.logical_and(kt == tau, pos < cut))
        sel = jnp.logical_and(sel, pos < limit)
        madd_ref[pl.ds(r0, KB), :] = jnp.where(sel, 0.0, NEG_BIG).astype(bf16)
        return carry

    lax.fori_loop(0, nb, mask_block, 0)

    @pl.when(i == 0)
    def _():
        rowi = lax.broadcasted_iota(i32, (QB, REP * QB), 0)
        coli = lax.broadcasted_iota(i32, (QB, REP * QB), 1)
        eye_ref[...] = ((coli & (QB - 1)) == rowi).astype(bf16)
        bmax = jnp.max(jnp.abs(band_ref[...]).reshape(ATT_HEADS * BAND_ROWS, QB), axis=0, keepdims=True)
        room = FAST_LOGIT_LIMIT - jnp.max(bmax, axis=1, keepdims=True)
        room2 = jnp.where(room > 0.0, room * room, -1.0)
        over = jnp.zeros((1, qn2_ref.shape[2]), i32)
        for g in range(KV_HEADS):
            def kmax_body(j, mx, g=g):
                r0 = pl.multiple_of(j * KB, KB)
                kf = k_ref[0, pl.ds(r0, KB), g * HEAD_DIM:(g + 1) * HEAD_DIM].astype(f32)
                n2 = jnp.sum(kf * kf, axis=-1, keepdims=True)
                return jnp.maximum(mx, jnp.max(n2, axis=0, keepdims=True))
            k2 = lax.fori_loop(0, k_ref.shape[1] // KB, kmax_body, jnp.zeros((1, 1), f32))
            q2 = jnp.max(qn2_ref[0, g * REP:(g + 1) * REP, :], axis=0, keepdims=True)
            over = over | (q2 * k2 * 1.002 > room2).astype(i32)
        shift_ref[0] = jnp.max(over)

    needs_shift = shift_ref[0] > 0

    for m_ref, acc_ref in zip(m_refs, acc_refs):
        m_ref[...] = jnp.full(m_ref.shape, NEG_BIG, f32)
        acc_ref[...] = jnp.zeros(acc_ref.shape, f32)

    ones_rows = jnp.ones((SUM_ROWS, KB), bf16)

    def att_block(j, carry, near, online):
        r0 = pl.multiple_of(j * KB, KB)

        def qk(g):
            qg = jnp.concatenate([qT_ref[0, h * HEAD_DIM:(h + 1) * HEAD_DIM, :]
                                  for h in range(g * REP, (g + 1) * REP)], axis=1)
            rhs = jnp.concatenate([qg, eye_ref[...]], axis=0)
            piece = KB // QK_PIECES
            parts = []
            for c in range(QK_PIECES):
                rows = pl.ds(r0 + c * piece, piece)
                lhs = jnp.concatenate([k_ref[0, rows, g * HEAD_DIM:(g + 1) * HEAD_DIM], madd_ref[rows, :]],
                                      axis=1)
                parts.append(jnp.dot(lhs, rhs, preferred_element_type=f32))
            return jnp.concatenate(parts, axis=0)

        ts = {0: qk(0)}
        for g in range(KV_HEADS):
            heads = range(g * REP, (g + 1) * REP)
            if g + 1 < KV_HEADS:
                ts[g + 1] = qk(g + 1)
            t = ts.pop(g)
            if near:
                rows = []
                for u in range(TILES_PER_BLOCK):
                    band_row = pl.multiple_of(jnp.clip(j * TILES_PER_BLOCK + u - i + 2, 0, 2) * KT, KT)
                    rows.append(jnp.concatenate([band_ref[h, pl.ds(band_row, KT), :] for h in heads], axis=1))
                t = t + jnp.concatenate(rows, axis=0)
            vt = jnp.concatenate([vT_ref[0, g * HEAD_DIM:(g + 1) * HEAD_DIM, pl.ds(r0, KB)], ones_rows], axis=0)
            if online:
                m_old = m_refs[g][...]
                m_new = jnp.maximum(m_old, jnp.max(t, axis=0, keepdims=True))
                alpha = jnp.exp2(m_old - m_new)
                p = jnp.exp2(t - m_new)
                m_refs[g][...] = m_new
                acc_refs[g][...] = (acc_refs[g][...] * alpha
                                    + jnp.dot(vt, p.astype(bf16), preferred_element_type=f32))
            else:
                p = jnp.exp2(t)
                acc_refs[g][...] = acc_refs[g][...] + jnp.dot(vt, p.astype(bf16), preferred_element_type=f32)
        return carry

    n_far = jnp.maximum(i - 1, 0) // TILES_PER_BLOCK

    def run_blocks(online):
        lax.fori_loop(0, n_far, functools.partial(att_block, near=False, online=online), 0)
        lax.fori_loop(n_far, nb, functools.partial(att_block, near=True, online=online), 0)

    lax.cond(needs_shift, functools.partial(run_blocks, True), functools.partial(run_blocks, False))

    for g in range(KV_HEADS):
        heads = range(g * REP, (g + 1) * REP)
        oT = acc_refs[g][0:HEAD_DIM, :] / acc_refs[g][HEAD_DIM:HEAD_DIM + 1, :]
        for r, h in enumerate(heads):
            gate = gb_ref[:, h * HEAD_DIM:(h + 1) * HEAD_DIM]
            o = oT[:, r * QB:(r + 1) * QB].T
            o_ref[:, h * HEAD_DIM:(h + 1) * HEAD_DIM] = (o * _silu(gate)).astype(o_ref.dtype)


def _dsa(qT, kn, vT, ik, iqT, iwT, proj, band, qn2, *, batch, seq, topk):
    m = proj.shape[0]
    nq = seq // QB
    kn3 = kn.reshape(batch, seq, KV_WIDTH)
    ik3 = ik.reshape(batch, seq, IDX_DIM)
    return pl.pallas_call(
        functools.partial(_dsa_kernel, topk=topk),
        grid=(batch, nq),
        in_specs=[
            pl.BlockSpec((1, ATT_WIDTH, QB), lambda b, i: (b, 0, i)),
            pl.BlockSpec((1, seq, KV_WIDTH), lambda b, i: (b, 0, 0)),
            pl.BlockSpec((1, KV_WIDTH, seq), lambda b, i: (b, 0, 0)),
            pl.BlockSpec((1, seq, IDX_DIM), lambda b, i: (b, 0, 0)),
            pl.BlockSpec((1, IDX_HEADS * IDX_DIM, QB), lambda b, i: (b, 0, i)),
            pl.BlockSpec((1, IDX_HEADS, QB), lambda b, i: (b, 0, i)),
            pl.BlockSpec((QB, ATT_WIDTH), lambda b, i: (b * nq + i, OFF_GB // ATT_WIDTH)),
            pl.BlockSpec((ATT_HEADS, BAND_ROWS, QB), lambda b, i: (0, 0, 0)),
            pl.BlockSpec((1, ATT_HEADS, seq), lambda b, i: (b, 0, 0)),
        ],
        out_specs=pl.BlockSpec((QB, ATT_WIDTH), lambda b, i: (b * nq + i, 0)),
        out_shape=jax.ShapeDtypeStruct((m, ATT_WIDTH), bf16),
        scratch_shapes=[
            pltpu.VMEM((seq, QB), i32),
            pltpu.VMEM((seq, QB), bf16),
            pltpu.VMEM((QB, REP * QB), bf16),
            pltpu.SMEM((1,), i32),
        ] + [pltpu.VMEM((1, REP * QB), f32)] * KV_HEADS
          + [pltpu.VMEM((HEAD_DIM + SUM_ROWS, REP * QB), f32)] * KV_HEADS,
        compiler_params=_cparams(("parallel", "arbitrary")),
        name="dsa",
    )(qT, kn3, vT, ik3, iqT, iwT, proj, band, qn2)


def _gelu(x):
    c = math.sqrt(2.0 / math.pi)
    half = 0.5 * x
    return half + half * jnp.tanh(x * (c + (c * 0.044715) * (x * x)))


def _sgu_kernel(u_ref, v_ref, g_ref, lng_ref, lnb_ref, ws_ref, bsg_ref, o_ref, wsm_ref, bs_ref):
    @pl.when(pl.program_id(0) == 0)
    def _():
        tpos = lax.broadcasted_iota(i32, (SG_CHUNK, SG_CHUNK), 0) // CHUNK
        spos = lax.broadcasted_iota(i32, (SG_CHUNK, SG_CHUNK), 1) // CHUNK
        for g in range(SG_GROUPS):
            wsm_ref[g] = jnp.where(tpos >= spos, ws_ref[g], 0.0).astype(bf16)
        bs_ref[...] = bsg_ref[...].T

    v = _gelu(v_ref[...].astype(f32))
    mu = jnp.mean(v, axis=-1, keepdims=True)
    vc = v - mu
    var = jnp.mean(vc * vc, axis=-1, keepdims=True)
    vn = (vc * lax.rsqrt(var + EPS) * lng_ref[...] + lnb_ref[...]).astype(bf16)
    for g in range(SG_GROUPS):
        sl = slice(g * SG_GROUP, (g + 1) * SG_GROUP)
        mixed = jnp.dot(wsm_ref[g], vn[:, sl], preferred_element_type=f32) + bs_ref[:, g:g + 1]
        gate = g_ref[:, sl].astype(f32)
        o_ref[:, sl] = (_gelu(u_ref[:, sl].astype(f32)) * mixed * _silu(gate)).astype(o_ref.dtype)


def _sgu(proj, ln_g, ln_b, w_s, b_s, lj):
    m = proj.shape[0]
    no = ln_g.shape[0]
    return pl.pallas_call(
        _sgu_kernel,
        grid=(m // SG_CHUNK,),
        in_specs=[
            pl.BlockSpec((SG_CHUNK, SG_WIDTH), lambda i: (i, 0)),
            pl.BlockSpec((SG_CHUNK, SG_WIDTH), lambda i: (i, 1)),
            pl.BlockSpec((SG_CHUNK, SG_WIDTH), lambda i: (i, 2)),
            pl.BlockSpec((None, 1, SG_WIDTH), lambda i: (lj, 0, 0)),
            pl.BlockSpec((None, 1, SG_WIDTH), lambda i: (lj, 0, 0)),
            pl.BlockSpec((None, SG_GROUPS, SG_CHUNK, SG_CHUNK), lambda i: (lj, 0, 0, 0)),
            pl.BlockSpec((None, SG_GROUPS, SG_CHUNK), lambda i: (lj, 0, 0)),
        ],
        out_specs=pl.BlockSpec((SG_CHUNK, SG_WIDTH), lambda i: (i, 0)),
        out_shape=jax.ShapeDtypeStruct((m, SG_WIDTH), bf16),
        scratch_shapes=[pltpu.VMEM((SG_GROUPS, SG_CHUNK, SG_CHUNK), bf16), pltpu.VMEM((SG_CHUNK, SG_GROUPS), f32)],
        compiler_params=_cparams(("arbitrary",)),
        name="sgu",
    )(proj, proj, proj, ln_g.reshape(no, 1, SG_WIDTH), ln_b.reshape(no, 1, SG_WIDTH), w_s, b_s)


PROJ_TN = 1024
EVEN_GATE_BLOCKS = (0, 1, 2, 3, 7, 8)
EVEN_ATT_BLOCKS = (4, 5, 6, 9)
ODD_COL_BLOCKS = tuple(range(3 * SG_WIDTH // PROJ_TN))
TAIL_START = 10240


def kernel(x, p, norm_gain, rel_bias, even_w_in, conv_w, conv_b, lru_w_r, lru_b_r, lru_w_i, lru_b_i,
           lru_lambda, q_norm, k_norm, even_w_out, odd_w_in, sg_ln_g, sg_ln_b, sg_w_s, sg_b_s, odd_w_out,
           pe_w, pe_gate_norm, pe_w_gate):
    batch, seq, d = x.shape
    depth = p.shape[0]
    topk = min(TOPK_MAX, seq // 4)
    m = batch * seq
    h = x.reshape(m, d)
    band = _band(rel_bias)
    hn = _norm(h, norm_gain, 0)
    even_w_in_t = jnp.swapaxes(even_w_in, 1, 2)
    wg_b, pew_b = _cast_bf16(pe_w_gate), _cast_bf16(pe_w)
    for layer in range(depth):
        j = layer // 2
        if layer % 2 == 0:
            proj = _proj(hn, even_w_in_t, j, EVEN_GATE_BLOCKS, transposed=True, tn=PROJ_TN)
            proj_att = _proj(hn, even_w_in_t, j, EVEN_ATT_BLOCKS, transposed=True, tn=PROJ_TN, out_dtype=bf16)
            tail = _proj_tail(hn, even_w_in_t, j, TAIL_START)
            ya = _lru(proj, conv_w, conv_b, lru_w_r, lru_b_r, lru_w_i, lru_b_i, lru_lambda, j,
                      batch=batch, seq=seq)
            qT, kn, vT, iqT, ik, iwT, qn2 = _prep(proj_att, tail, q_norm, k_norm, j, batch=batch, seq=seq)
            yb = _dsa(qT, kn, vT, ik, iqT, iwT, proj, band, qn2, batch=batch, seq=seq, topk=topk)
            h = _outproj(h, [ya, yb], even_w_out, j)
        else:
            proj = _proj(hn, odd_w_in, j, ODD_COL_BLOCKS, tn=PROJ_TN, out_dtype=bf16)
            h = _outproj(h, [_sgu(proj, sg_ln_g, sg_ln_b, sg_w_s, sg_b_s, j)], odd_w_out, j, tn=1024)
        nxt = (norm_gain, layer + 1) if layer + 1 < depth else (None, None)
        h, hn = _pe(h, pe_gate_norm, wg_b, p, pew_b, layer, *nxt)
    return h.reshape(batch, seq, d)
```

```python
import functools
import math

import jax
import jax.numpy as jnp
from jax import lax
from jax.experimental import pallas as pl
from jax.experimental.pallas import tpu as pltpu

f32 = jnp.float32
bf16 = jnp.bfloat16
i32 = jnp.int32

D_MODEL = 2048
CHUNK = 64
PE_DIM = 256
EPS = 1e-6
RNN_WIDTH = 2048
RNN_BLOCK = 128
CONV_WIDTH = 4
LRU_C = 8.0
ATT_HEADS = 16
HEAD_DIM = 128
KV_HEADS = 4
REP = ATT_HEADS // KV_HEADS
ATT_WIDTH = 2048
KV_WIDTH = 512
IDX_HEADS = 16
IDX_DIM = 64
TOPK_MAX = 256
N_BUCKETS = 32
SG_CHUNK = 128
SG_GROUPS = 16
SG_GROUP = 128
SG_WIDTH = 2048

LANES = 128
SUBLANES = 8
QB = 128
KT = 128
KB = 512
TILES_PER_BLOCK = KB // KT
QK_PIECES = 4
RADIX_FIXED_BITS = 26
RADIX_BITS_PER_CHECK = 2
SUM_ROWS = 16
LOG2E = math.log2(math.e)
ATT_C2 = HEAD_DIM ** -0.5 * LOG2E
VMEM_LIMIT = 52 * 1024 * 1024
NEG_BIG = -1e30
FAST_LOGIT_LIMIT = 60.0

OFF_XA, OFF_GA, OFF_GB = 0, 2048, 4096
OFF_Q, OFF_K, OFF_V, OFF_IQ = 0, 2048, 2560, 3072


def _cparams(sem):
    return pltpu.CompilerParams(dimension_semantics=sem, vmem_limit_bytes=VMEM_LIMIT)


def _sigmoid(x):
    return 0.5 * jnp.tanh(0.5 * x) + 0.5


def _silu(x):
    return x * _sigmoid(x)


def _block_lookup(table):
    def f(j):
        out = jnp.int32(table[-1])
        for idx in range(len(table) - 2, -1, -1):
            out = jnp.where(j == idx, table[idx], out)
        return out
    return f


def _rms(x, gain):
    ms = jnp.mean(x * x, axis=-1, keepdims=True)
    return x * lax.rsqrt(ms + EPS) * gain


def _norm_kernel(h_ref, g_ref, o_ref):
    o_ref[...] = _rms(h_ref[...], g_ref[...]).astype(o_ref.dtype)


def _norm(h, gains, layer, *, tm=512):
    m, d = h.shape
    return pl.pallas_call(
        _norm_kernel,
        grid=(m // tm,),
        in_specs=[pl.BlockSpec((tm, d), lambda i: (i, 0)),
                  pl.BlockSpec((None, 1, d), lambda i: (layer, 0, 0))],
        out_specs=pl.BlockSpec((tm, d), lambda i: (i, 0)),
        out_shape=jax.ShapeDtypeStruct((m, d), bf16),
        compiler_params=_cparams(("parallel",)),
        name="norm",
    )(h, gains.reshape(gains.shape[0], 1, d))


_NT = (((1,), (1,)), ((), ()))


def _proj_kernel(hn_ref, w_ref, o_ref, wb_ref, *, transposed):
    @pl.when(pl.program_id(1) == 0)
    def _():
        wb_ref[...] = w_ref[...].astype(bf16)

    if transposed:
        acc = lax.dot_general(hn_ref[...], wb_ref[...], _NT, preferred_element_type=f32)
    else:
        acc = jnp.dot(hn_ref[...], wb_ref[...], preferred_element_type=f32)
    o_ref[...] = acc.astype(o_ref.dtype)


def _proj(hn, ws, widx, col_blocks, *, transposed=False, tm=1024, tn=1024, out_dtype=f32):
    m, d = hn.shape
    n = len(col_blocks) * tn
    colmap = _block_lookup(col_blocks)
    if transposed:
        w_spec = pl.BlockSpec((None, tn, d), lambda j, i: (widx, colmap(j), 0))
        wb_shape = (tn, d)
    else:
        w_spec = pl.BlockSpec((None, d, tn), lambda j, i: (widx, 0, colmap(j)))
        wb_shape = (d, tn)
    return pl.pallas_call(
        functools.partial(_proj_kernel, transposed=transposed),
        grid=(n // tn, m // tm),
        in_specs=[pl.BlockSpec((tm, d), lambda j, i: (i, 0)), w_spec],
        out_specs=pl.BlockSpec((tm, tn), lambda j, i: (i, j)),
        out_shape=jax.ShapeDtypeStruct((m, n), out_dtype),
        scratch_shapes=[pltpu.VMEM(wb_shape, bf16)],
        compiler_params=_cparams(("parallel", "arbitrary")),
        name="proj",
    )(hn, ws)


def _proj_tail_kernel(hn_ref, wt_ref, o_ref, *, tail_rows):
    row = lax.broadcasted_iota(i32, wt_ref.shape, 0)
    wt = jnp.where(row < tail_rows, wt_ref[...], 0.0).astype(bf16)
    o_ref[...] = lax.dot_general(hn_ref[...], wt, _NT, preferred_element_type=f32)


def _proj_tail(hn, wts, widx, tail_start, *, tm=1024):
    m, d = hn.shape
    tail_rows = wts.shape[1] - tail_start
    assert 0 < tail_rows < LANES and tail_start % LANES == 0
    return pl.pallas_call(
        functools.partial(_proj_tail_kernel, tail_rows=tail_rows),
        grid=(m // tm,),
        in_specs=[
            pl.BlockSpec((tm, d), lambda i: (i, 0)),
            pl.BlockSpec((None, LANES, d), lambda i: (widx, tail_start // LANES, 0)),
        ],
        out_specs=pl.BlockSpec((tm, LANES), lambda i: (i, 0)),
        out_shape=jax.ShapeDtypeStruct((m, LANES), f32),
        compiler_params=_cparams(("parallel",)),
        name="proj_tail",
    )(hn, wts)


def _cast_kernel(w_ref, o_ref):
    o_ref[...] = w_ref[...].astype(o_ref.dtype)


def _cast_bf16(w, *, tr=512):
    lead, rows, cols = w.shape
    tr = min(tr, rows)
    return pl.pallas_call(
        _cast_kernel,
        grid=(lead, rows // tr),
        in_specs=[pl.BlockSpec((None, tr, cols), lambda a, r: (a, r, 0))],
        out_specs=pl.BlockSpec((None, tr, cols), lambda a, r: (a, r, 0)),
        out_shape=jax.ShapeDtypeStruct(w.shape, bf16),
        compiler_params=_cparams(("parallel", "parallel")),
        name="cast_bf16",
    )(w)


def _outproj_kernel(*refs, n_y):
    h_ref = refs[0]
    y_refs = refs[1:1 + n_y]
    w_refs = refs[1 + n_y:1 + 2 * n_y]
    o_ref = refs[1 + 2 * n_y]
    wb_refs = refs[2 + 2 * n_y:]

    @pl.when(pl.program_id(1) == 0)
    def _():
        for w_ref, wb_ref in zip(w_refs, wb_refs):
            wb_ref[...] = w_ref[...].astype(bf16)

    acc = h_ref[...]
    for y_ref, wb_ref in zip(y_refs, wb_refs):
        acc = acc + jnp.dot(y_ref[...], wb_ref[...], preferred_element_type=f32)
    o_ref[...] = acc


def _outproj(h, ys, ws, widx, *, tm=1024, tn=512):
    m, d = h.shape
    n_y = len(ys)
    kw = ys[0].shape[1]
    in_specs = [pl.BlockSpec((tm, tn), lambda j, i: (i, j))]
    in_specs += [pl.BlockSpec((tm, kw), lambda j, i: (i, 0)) for _ in ys]
    in_specs += [pl.BlockSpec((None, kw, tn), lambda j, i, c=c: (widx, c, j)) for c in range(n_y)]
    return pl.pallas_call(
        functools.partial(_outproj_kernel, n_y=n_y),
        grid=(d // tn, m // tm),
        in_specs=in_specs,
        out_specs=pl.BlockSpec((tm, tn), lambda j, i: (i, j)),
        out_shape=jax.ShapeDtypeStruct((m, d), f32),
        scratch_shapes=[pltpu.VMEM((kw, tn), bf16)] * n_y,
        compiler_params=_cparams(("parallel", "arbitrary")),
        name="outproj",
    )(h, *ys, *([ws] * n_y))


def _pe_kernel(h_ref, gn_ref, wg_ref, p_ref, pew_ref, *rest, tn, emit_next):
    if emit_next:
        gnext_ref, o_ref, hnext_ref, hn_ref, pb_ref, orow_ref = rest
    else:
        o_ref, hn_ref, pb_ref = rest
    j = pl.program_id(1)

    @pl.when(j == 0)
    def _():
        hn_ref[...] = _rms(h_ref[...], gn_ref[...]).astype(bf16)
        pb_ref[...] = p_ref[...].astype(bf16)

    cols = pl.ds(pl.multiple_of(j * tn, tn), tn)
    gate = _sigmoid(jnp.dot(hn_ref[...], wg_ref[...], preferred_element_type=f32))
    e = jnp.dot(pb_ref[...], pew_ref[...], preferred_element_type=f32)
    out = h_ref[:, cols] + gate * e
    o_ref[...] = out

    if emit_next:
        orow_ref[:, cols] = out

        @pl.when(j == pl.num_programs(1) - 1)
        def _():
            hnext_ref[...] = _rms(orow_ref[...], gnext_ref[...]).astype(bf16)


def _pe(h, gns, wgs, p, pews, layer, next_gains=None, next_layer=None, *, tm=1024, tn=512):
    m, d = h.shape
    depth = p.shape[0]
    emit_next = next_gains is not None
    in_specs = [
        pl.BlockSpec((tm, d), lambda i, j: (i, 0)),
        pl.BlockSpec((None, 1, d), lambda i, j: (layer, 0, 0)),
        pl.BlockSpec((None, d, tn), lambda i, j: (layer, 0, j)),
        pl.BlockSpec((None, tm, PE_DIM), lambda i, j: (layer, i, 0)),
        pl.BlockSpec((None, PE_DIM, tn), lambda i, j: (layer, 0, j)),
    ]
    args = [h, gns.reshape(depth, 1, d), wgs, p.reshape(depth, m, PE_DIM), pews]
    out_specs = [pl.BlockSpec((tm, tn), lambda i, j: (i, j))]
    out_shape = [jax.ShapeDtypeStruct((m, d), f32)]
    scratch = [pltpu.VMEM((tm, d), bf16), pltpu.VMEM((tm, PE_DIM), bf16)]
    if emit_next:
        in_specs.append(pl.BlockSpec((None, 1, d), lambda i, j: (next_layer, 0, 0)))
        args.append(next_gains.reshape(next_gains.shape[0], 1, d))
        out_specs.append(pl.BlockSpec((tm, d), lambda i, j: (i, 0)))
        out_shape.append(jax.ShapeDtypeStruct((m, d), bf16))
        scratch.append(pltpu.VMEM((tm, d), f32))
    res = pl.pallas_call(
        functools.partial(_pe_kernel, tn=tn, emit_next=emit_next),
        grid=(m // tm, d // tn),
        in_specs=in_specs,
        out_specs=out_specs,
        out_shape=out_shape,
        scratch_shapes=scratch,
        compiler_params=_cparams(("parallel", "arbitrary")),
        name="pe",
    )(*args)
    return res if emit_next else (res[0], None)


LRU_TT = 1024
LRU_CB = 512


def _lru_kernel(x_ref, ga_ref, cw_ref, cb_ref, wr_ref, br_ref, wi_ref, bi_ref, lam_ref, o_ref,
                xext_ref, hcar_ref, a_ref, b_ref):
    tstep = pl.program_id(2)
    pad = SUBLANES

    @pl.when(tstep == 0)
    def _():
        xext_ref[0:pad, :] = jnp.zeros((pad, LRU_CB), f32)
        hcar_ref[...] = jnp.zeros_like(hcar_ref)

    @pl.when(tstep > 0)
    def _():
        xext_ref[0:pad, :] = xext_ref[LRU_TT:LRU_TT + pad, :]

    xext_ref[pad:, :] = x_ref[...]

    xc = cb_ref[...] + cw_ref[CONV_WIDTH - 1:CONV_WIDTH, :] * x_ref[...]
    for j in range(CONV_WIDTH - 1):
        sh = CONV_WIDTH - 1 - j
        xc = xc + cw_ref[j:j + 1, :] * xext_ref[pad - sh:pad - sh + LRU_TT, :]

    xcb = xc.astype(bf16)
    sp = jax.nn.softplus(-lam_ref[...])
    nb = LRU_CB // RNN_BLOCK
    for g in range(nb):
        sl = slice(g * RNN_BLOCK, (g + 1) * RNN_BLOCK)
        xg = xcb[:, sl]
        r = _sigmoid(jnp.dot(xg, wr_ref[g].astype(bf16), preferred_element_type=f32) + br_ref[:, sl])
        ig = _sigmoid(jnp.dot(xg, wi_ref[g].astype(bf16), preferred_element_type=f32) + bi_ref[:, sl])
        log_a = (-LRU_C) * r * sp[:, sl]
        a = jnp.exp(log_a)
        z = jnp.tanh(-log_a) * (1.0 + a * a)
        mult = jnp.where(z > 0.0, z * lax.rsqrt(z), 0.0)
        a_ref[:, sl] = a
        b_ref[:, sl] = mult * ig * xc[:, sl]

    nchunk = LRU_TT // SUBLANES
    a3 = a_ref[...].reshape(nchunk, SUBLANES, LRU_CB)
    b3 = b_ref[...].reshape(nchunk, SUBLANES, LRU_CB)
    row = lax.broadcasted_iota(i32, a3.shape, 1)
    d = 1
    while d < SUBLANES:
        keep = row >= d
        a_sh = jnp.where(keep, pltpu.roll(a3, d, axis=1), 1.0)
        b_sh = jnp.where(keep, pltpu.roll(b3, d, axis=1), 0.0)
        b3 = a3 * b_sh + b3
        a3 = a3 * a_sh
        d *= 2
    a_ref[...] = a3.reshape(LRU_TT, LRU_CB)
    b_ref[...] = b3.reshape(LRU_TT, LRU_CB)

    def carry(c, hprev):
        r0 = pl.multiple_of(c * SUBLANES, SUBLANES)
        h8 = a_ref[pl.ds(r0, SUBLANES), :] * hprev + b_ref[pl.ds(r0, SUBLANES), :]
        gate = ga_ref[pl.ds(r0, SUBLANES), :]
        o_ref[pl.ds(r0, SUBLANES), :] = (h8 * _silu(gate)).astype(o_ref.dtype)
        return jnp.broadcast_to(h8[SUBLANES - 1:SUBLANES, :], (SUBLANES, LRU_CB))

    hcar_ref[...] = lax.fori_loop(0, nchunk, carry, hcar_ref[...], unroll=8)


def _lru(proj, cw, cb, wr, br, wi, bi, lam, lj, *, batch, seq):
    m = proj.shape[0]
    ne = cw.shape[0]
    nt = seq // LRU_TT
    ncb = RNN_WIDTH // LRU_CB
    gpb = LRU_CB // RNN_BLOCK
    ga_off = OFF_GA // LRU_CB
    row = lambda b, c, t: b * nt + t
    vec = pl.BlockSpec((None, 1, LRU_CB), lambda b, c, t: (lj, 0, c))
    gate_w = pl.BlockSpec((None, gpb, RNN_BLOCK, RNN_BLOCK), lambda b, c, t: (lj, c, 0, 0))
    as_vec = lambda a: a.reshape(ne, 1, RNN_WIDTH)
    return pl.pallas_call(
        _lru_kernel,
        grid=(batch, ncb, nt),
        in_specs=[
            pl.BlockSpec((LRU_TT, LRU_CB), lambda b, c, t: (row(b, c, t), c)),
            pl.BlockSpec((LRU_TT, LRU_CB), lambda b, c, t: (row(b, c, t), ga_off + c)),
            pl.BlockSpec((None, CONV_WIDTH, LRU_CB), lambda b, c, t: (lj, 0, c)),
            vec, gate_w, vec, gate_w, vec, vec,
        ],
        out_specs=pl.BlockSpec((LRU_TT, LRU_CB), lambda b, c, t: (row(b, c, t), c)),
        out_shape=jax.ShapeDtypeStruct((m, RNN_WIDTH), bf16),
        scratch_shapes=[
            pltpu.VMEM((LRU_TT + SUBLANES, LRU_CB), f32),
            pltpu.VMEM((SUBLANES, LRU_CB), f32),
            pltpu.VMEM((LRU_TT, LRU_CB), f32),
            pltpu.VMEM((LRU_TT, LRU_CB), f32),
        ],
        compiler_params=_cparams(("parallel", "parallel", "arbitrary")),
        name="lru",
    )(proj, proj, cw, as_vec(cb), wr, as_vec(br), wi, as_vec(bi), as_vec(lam))


PREP_T = 512


def _head_rms(x, gain):
    ms = jnp.mean(x * x, axis=-1, keepdims=True)
    return x * lax.rsqrt(ms + EPS) * gain


def _prep_kernel(q_ref, k_ref, v_ref, iq_ref, tail_ref, qg_ref, kg_ref,
                 qT_ref, kn_ref, vT_ref, iqT_ref, ik_ref, iwT_ref, qn2_ref):
    for h in range(ATT_HEADS):
        sl = slice(h * HEAD_DIM, (h + 1) * HEAD_DIM)
        qt = (_head_rms(q_ref[:, sl].astype(f32), qg_ref[...]) * ATT_C2).T.astype(bf16)
        qT_ref[0, sl, :] = qt
        qf = qt.astype(f32)
        qn2_ref[0, h:h + 1, :] = jnp.sum(qf * qf, axis=0, keepdims=True)
    for g in range(KV_HEADS):
        sl = slice(g * HEAD_DIM, (g + 1) * HEAD_DIM)
        kn_ref[:, sl] = _head_rms(k_ref[:, sl].astype(f32), kg_ref[...]).astype(bf16)
        vT_ref[0, sl, :] = v_ref[:, sl].T
    for c in range(IDX_HEADS * IDX_DIM // LANES):
        sl = slice(c * LANES, (c + 1) * LANES)
        iqT_ref[0, sl, :] = iq_ref[:, sl].T
    tail_t = tail_ref[...].T
    ik_ref[...] = tail_ref[:, 0:IDX_DIM].astype(bf16)
    iwT_ref[0] = tail_t[IDX_DIM:IDX_DIM + IDX_HEADS, :]


def _prep(proj, tail, q_gains, k_gains, lj, *, batch, seq):
    m = proj.shape[0]
    ne = q_gains.shape[0]
    nt = seq // PREP_T
    bt = lambda i: (i // nt, 0, i % nt)
    return pl.pallas_call(
        _prep_kernel,
        grid=(m // PREP_T,),
        in_specs=[
            pl.BlockSpec((PREP_T, ATT_WIDTH), lambda i: (i, OFF_Q // ATT_WIDTH)),
            pl.BlockSpec((PREP_T, KV_WIDTH), lambda i: (i, OFF_K // KV_WIDTH)),
            pl.BlockSpec((PREP_T, KV_WIDTH), lambda i: (i, OFF_V // KV_WIDTH)),
            pl.BlockSpec((PREP_T, IDX_HEADS * IDX_DIM), lambda i: (i, OFF_IQ // (IDX_HEADS * IDX_DIM))),
            pl.BlockSpec((PREP_T, LANES), lambda i: (i, 0)),
            pl.BlockSpec((None, 1, HEAD_DIM), lambda i: (lj, 0, 0)),
            pl.BlockSpec((None, 1, HEAD_DIM), lambda i: (lj, 0, 0)),
        ],
        out_specs=[
            pl.BlockSpec((1, ATT_WIDTH, PREP_T), bt),
            pl.BlockSpec((PREP_T, KV_WIDTH), lambda i: (i, 0)),
            pl.BlockSpec((1, KV_WIDTH, PREP_T), bt),
            pl.BlockSpec((1, IDX_HEADS * IDX_DIM, PREP_T), bt),
            pl.BlockSpec((PREP_T, IDX_DIM), lambda i: (i, 0)),
            pl.BlockSpec((1, IDX_HEADS, PREP_T), bt),
            pl.BlockSpec((1, ATT_HEADS, PREP_T), bt),
        ],
        out_shape=[
            jax.ShapeDtypeStruct((batch, ATT_WIDTH, seq), bf16),
            jax.ShapeDtypeStruct((m, KV_WIDTH), bf16),
            jax.ShapeDtypeStruct((batch, KV_WIDTH, seq), bf16),
            jax.ShapeDtypeStruct((batch, IDX_HEADS * IDX_DIM, seq), bf16),
            jax.ShapeDtypeStruct((m, IDX_DIM), bf16),
            jax.ShapeDtypeStruct((batch, IDX_HEADS, seq), f32),
            jax.ShapeDtypeStruct((batch, ATT_HEADS, seq), f32),
        ],
        compiler_params=_cparams(("parallel",)),
        name="attn_prep",
    )(proj, proj, proj, proj, tail, q_gains.reshape(ne, 1, HEAD_DIM), k_gains.reshape(ne, 1, HEAD_DIM))


BAND_ROWS = 3 * KT
FAR_BUCKET = N_BUCKETS // 2 - 1


def _band_kernel(tab_ref, o_ref):
    jj = lax.broadcasted_iota(i32, (BAND_ROWS, QB), 0)
    qi = lax.broadcasted_iota(i32, (BAND_ROWS, QB), 1)
    rel = jnp.where(jj < KT, -2 * KT, jj - 2 * KT - qi)
    n = jnp.abs(rel)
    large = jnp.full(rel.shape, 8, i32)
    for thr in (12, 16, 23, 32, 46, 64, 91):
        large = large + (n >= thr).astype(i32)
    bucket = jnp.where(rel > 0, N_BUCKETS // 2, 0) + jnp.where(n < 8, n, large)

    def per_head(h, carry):
        acc = jnp.zeros((BAND_ROWS, QB), f32)
        for b in range(N_BUCKETS):
            acc = jnp.where(bucket == b, tab_ref[b, h], acc)
        o_ref[h] = (acc - tab_ref[FAR_BUCKET, h]) * LOG2E
        return carry

    lax.fori_loop(0, ATT_HEADS, per_head, 0)


def _band(rel_bias):
    return pl.pallas_call(
        _band_kernel,
        in_specs=[pl.BlockSpec(memory_space=pltpu.SMEM)],
        out_specs=pl.BlockSpec(memory_space=pltpu.VMEM),
        out_shape=jax.ShapeDtypeStruct((ATT_HEADS, BAND_ROWS, QB), f32),
        name="bias_band",
    )(rel_bias)


def _dsa_kernel(qT_ref, k_ref, vT_ref, ik_ref, iqT_ref, iwT_ref, gb_ref, band_ref, qn2_ref, o_ref,
                key_ref, madd_ref, eye_ref, shift_ref, *state_refs, topk):
    m_refs = state_refs[:KV_HEADS]
    acc_refs = state_refs[KV_HEADS:]
    i = pl.program_id(1)
    nb = i // TILES_PER_BLOCK + 1
    t0 = i * QB
    qpos = t0 + lax.broadcasted_iota(i32, (1, QB), 1)
    limit = (qpos // CHUNK + 1) * CHUNK
    sub_iota = lax.broadcasted_iota(i32, (KB, QB), 0)
    idx_scale = (IDX_DIM ** -0.5) * (IDX_HEADS ** -0.5)
    int_min = jnp.int32(-2 ** 31)

    def score_block(j, carry):
        r0 = pl.multiple_of(j * KB, KB)
        ks = ik_ref[0, pl.ds(r0, KB), :]
        acc = jnp.zeros((KB, QB), f32)
        for hp in range(IDX_HEADS // 2):
            h0, h1 = 2 * hp, 2 * hp + 1
            w = jnp.concatenate([iqT_ref[0, h0 * IDX_DIM:(h0 + 1) * IDX_DIM, :],
                                 iqT_ref[0, h1 * IDX_DIM:(h1 + 1) * IDX_DIM, :]], axis=1)
            s = jnp.dot(ks, w, preferred_element_type=f32)
            acc = acc + jnp.maximum(s[:, :QB], 0.0) * iwT_ref[0, h0:h0 + 1, :]
            acc = acc + jnp.maximum(s[:, QB:], 0.0) * iwT_ref[0, h1:h1 + 1, :]
        score = acc * idx_scale
        score = jnp.where(score == 0.0, 0.0, score)
        score = jnp.where(r0 + sub_iota < limit, score, -jnp.inf)
        bits = pltpu.bitcast(score, i32)
        key_ref[pl.ds(r0, KB), :] = jnp.where(bits >= 0, bits, bits ^ jnp.int32(0x7FFFFFFF))
        return carry

    lax.fori_loop(0, nb, score_block, 0)

    def count_rows(pred_fn):
        def one(j, cnt8):
            r0 = pl.multiple_of(j * KB, KB)
            hit = pred_fn(key_ref[pl.ds(r0, KB), :], r0).astype(i32)
            return cnt8 + jnp.sum(hit.reshape(KB // SUBLANES, SUBLANES, QB), axis=0)

        cnt8 = lax.fori_loop(0, nb // 2, lambda jj, c: one(2 * jj + 1, one(2 * jj, c)),
                             jnp.zeros((SUBLANES, QB), i32))
        cnt8 = lax.cond(nb % 2 == 1, lambda c: one(nb - 1, c), lambda c: c, cnt8)
        return jnp.sum(cnt8, axis=0, keepdims=True)

    n_nonneg = count_rows(lambda kt, r0: kt >= 0)
    nonneg = n_nonneg >= topk
    prefix0 = jnp.where(nonneg, jnp.int32(0), int_min)
    cur0 = jnp.where(nonneg, n_nonneg, nb * KB)
    n_bits = 31

    def bit_pass(b, state):
        prefix, cur = state
        bit = jnp.where(b < n_bits, lax.shift_left(jnp.int32(1), jnp.maximum(n_bits - 1 - b, 0)), 0)
        cand = prefix | bit
        cnt = count_rows(lambda kt, r0: kt >= cand)
        take = cnt >= topk
        return jnp.where(take, cand, prefix), jnp.where(take, cnt, cur)

    state1 = lax.fori_loop(0, RADIX_FIXED_BITS, bit_pass, (prefix0, cur0))

    def bits_cond(state):
        b, _, _, more = state
        return jnp.logical_and(b < n_bits, more > 0)

    def bits_body(state):
        b, prefix, cur, _ = state
        for u in range(RADIX_BITS_PER_CHECK):
            prefix, cur = bit_pass(b + u, (prefix, cur))
        return b + RADIX_BITS_PER_CHECK, prefix, cur, jnp.max((cur > topk).astype(i32))

    more1 = jnp.max((state1[1] > topk).astype(i32))
    _, tau, cur, _ = lax.while_loop(bits_cond, bits_body, (jnp.int32(RADIX_FIXED_BITS), *state1, more1))

    neg_inf_key = jnp.int32(-2 ** 31 + 0x7FFFFF)
    excess = jnp.logical_and(cur > topk, tau > neg_inf_key)
    any_excess = jnp.max(excess.astype(i32)) > 0
    n_gt = lax.cond(any_excess, lambda: count_rows(lambda kt, r0: kt > tau), lambda: jnp.zeros((1, QB), i32))
    room = topk - n_gt
    n_iter = jnp.where(any_excess, 13, 0)

    def cut_step(b, cut):
        cand = cut | lax.shift_left(jnp.int32(1), 12 - b)
        cnt = count_rows(lambda kt, r0: jnp.logical_and(kt == tau, r0 + sub_iota < cand))
        return jnp.where(cnt <= room, cand, cut)

    cut = lax.fori_loop(0, n_iter, cut_step, jnp.zeros((1, QB), i32))
    cut = jnp.where(excess, cut, jnp.int32(2 ** 30))

    def mask_block(j, carry):
        r0 = pl.multiple_of(j * KB, KB)
        kt = key_ref[pl.ds(r0, KB), :]
        pos = r0 + sub_iota
        sel = jnp.logical_or(kt > tau, jnp.logical_and(kt == tau, pos < cut))
        sel = jnp.logical_and(sel, pos < limit)
        madd_ref[pl.ds(r0, KB), :] = jnp.where(sel, 0.0, NEG_BIG).astype(bf16)
        return carry

    lax.fori_loop(0, nb, mask_block, 0)

    @pl.when(i == 0)
    def _():
        rowi = lax.broadcasted_iota(i32, (QB, REP * QB), 0)
        coli = lax.broadcasted_iota(i32, (QB, REP * QB), 1)
        eye_ref[...] = ((coli & (QB - 1)) == rowi).astype(bf16)
        bmax = jnp.max(jnp.abs(band_ref[...]).reshape(ATT_HEADS * BAND_ROWS, QB), axis=0, keepdims=True)
        room = FAST_LOGIT_LIMIT - jnp.max(bmax, axis=1, keepdims=True)
        room2 = jnp.where(room > 0.0, room * room, -1.0)
        over = jnp.zeros((1, qn2_ref.shape[2]), i32)
        for g in range(KV_HEADS):
            def kmax_body(j, mx, g=g):
                r0 = pl.multiple_of(j * KB, KB)
                kf = k_ref[0, pl.ds(r0, KB), g * HEAD_DIM:(g + 1) * HEAD_DIM].astype(f32)
                n2 = jnp.sum(kf * kf, axis=-1, keepdims=True)
                return jnp.maximum(mx, jnp.max(n2, axis=0, keepdims=True))
            k2 = lax.fori_loop(0, k_ref.shape[1] // KB, kmax_body, jnp.zeros((1, 1), f32))
            q2 = jnp.max(qn2_ref[0, g * REP:(g + 1) * REP, :], axis=0, keepdims=True)
            over = over | (q2 * k2 * 1.002 > room2).astype(i32)
        shift_ref[0] = jnp.max(over)

    needs_shift = shift_ref[0] > 0

    for m_ref, acc_ref in zip(m_refs, acc_refs):
        m_ref[...] = jnp.full(m_ref.shape, NEG_BIG, f32)
        acc_ref[...] = jnp.zeros(acc_ref.shape, f32)

    ones_rows = jnp.ones((SUM_ROWS, KB), bf16)

    def att_block(j, carry, near, online):
        r0 = pl.multiple_of(j * KB, KB)

        def qk(g):
            qg = jnp.concatenate([qT_ref[0, h * HEAD_DIM:(h + 1) * HEAD_DIM, :]
                                  for h in range(g * REP, (g + 1) * REP)], axis=1)
            rhs = jnp.concatenate([qg, eye_ref[...]], axis=0)
            piece = KB // QK_PIECES
            parts = []
            for c in range(QK_PIECES):
                rows = pl.ds(r0 + c * piece, piece)
                lhs = jnp.concatenate([k_ref[0, rows, g * HEAD_DIM:(g + 1) * HEAD_DIM], madd_ref[rows, :]],
                                      axis=1)
                parts.append(jnp.dot(lhs, rhs, preferred_element_type=f32))
            return jnp.concatenate(parts, axis=0)

        ts = {0: qk(0)}
        for g in range(KV_HEADS):
            heads = range(g * REP, (g + 1) * REP)
            if g + 1 < KV_HEADS:
                ts[g + 1] = qk(g + 1)
            t = ts.pop(g)
            if near:
                rows = []
                for u in range(TILES_PER_BLOCK):
                    band_row = pl.multiple_of(jnp.clip(j * TILES_PER_BLOCK + u - i + 2, 0, 2) * KT, KT)
                    rows.append(jnp.concatenate([band_ref[h, pl.ds(band_row, KT), :] for h in heads], axis=1))
                t = t + jnp.concatenate(rows, axis=0)
            vt = jnp.concatenate([vT_ref[0, g * HEAD_DIM:(g + 1) * HEAD_DIM, pl.ds(r0, KB)], ones_rows], axis=0)
            if online:
                m_old = m_refs[g][...]
                m_new = jnp.maximum(m_old, jnp.max(t, axis=0, keepdims=True))
                alpha = jnp.exp2(m_old - m_new)
                p = jnp.exp2(t - m_new)
                m_refs[g][...] = m_new
                acc_refs[g][...] = (acc_refs[g][...] * alpha
                                    + jnp.dot(vt, p.astype(bf16), preferred_element_type=f32))
            else:
                p = jnp.exp2(t)
                acc_refs[g][...] = acc_refs[g][...] + jnp.dot(vt, p.astype(bf16), preferred_element_type=f32)
        return carry

    n_far = jnp.maximum(i - 1, 0) // TILES_PER_BLOCK

    def run_blocks(online):
        lax.fori_loop(0, n_far, functools.partial(att_block, near=False, online=online), 0)
        lax.fori_loop(n_far, nb, functools.partial(att_block, near=True, online=online), 0)

    lax.cond(needs_shift, functools.partial(run_blocks, True), functools.partial(run_blocks, False))

    for g in range(KV_HEADS):
        heads = range(g * REP, (g + 1) * REP)
        oT = acc_refs[g][0:HEAD_DIM, :] / acc_refs[g][HEAD_DIM:HEAD_DIM + 1, :]
        for r, h in enumerate(heads):
            gate = gb_ref[:, h * HEAD_DIM:(h + 1) * HEAD_DIM]
            o = oT[:, r * QB:(r + 1) * QB].T
            o_ref[:, h * HEAD_DIM:(h + 1) * HEAD_DIM] = (o * _silu(gate)).astype(o_ref.dtype)


def _dsa(qT, kn, vT, ik, iqT, iwT, proj, band, qn2, *, batch, seq, topk):
    m = proj.shape[0]
    nq = seq // QB
    kn3 = kn.reshape(batch, seq, KV_WIDTH)
    ik3 = ik.reshape(batch, seq, IDX_DIM)
    return pl.pallas_call(
        functools.partial(_dsa_kernel, topk=topk),
        grid=(batch, nq),
        in_specs=[
            pl.BlockSpec((1, ATT_WIDTH, QB), lambda b, i: (b, 0, i)),
            pl.BlockSpec((1, seq, KV_WIDTH), lambda b, i: (b, 0, 0)),
            pl.BlockSpec((1, KV_WIDTH, seq), lambda b, i: (b, 0, 0)),
            pl.BlockSpec((1, seq, IDX_DIM), lambda b, i: (b, 0, 0)),
            pl.BlockSpec((1, IDX_HEADS * IDX_DIM, QB), lambda b, i: (b, 0, i)),
            pl.BlockSpec((1, IDX_HEADS, QB), lambda b, i: (b, 0, i)),
            pl.BlockSpec((QB, ATT_WIDTH), lambda b, i: (b * nq + i, OFF_GB // ATT_WIDTH)),
            pl.BlockSpec((ATT_HEADS, BAND_ROWS, QB), lambda b, i: (0, 0, 0)),
            pl.BlockSpec((1, ATT_HEADS, seq), lambda b, i: (b, 0, 0)),
        ],
        out_specs=pl.BlockSpec((QB, ATT_WIDTH), lambda b, i: (b * nq + i, 0)),
        out_shape=jax.ShapeDtypeStruct((m, ATT_WIDTH), bf16),
        scratch_shapes=[
            pltpu.VMEM((seq, QB), i32),
            pltpu.VMEM((seq, QB), bf16),
            pltpu.VMEM((QB, REP * QB), bf16),
            pltpu.SMEM((1,), i32),
        ] + [pltpu.VMEM((1, REP * QB), f32)] * KV_HEADS
          + [pltpu.VMEM((HEAD_DIM + SUM_ROWS, REP * QB), f32)] * KV_HEADS,
        compiler_params=_cparams(("parallel", "arbitrary")),
        name="dsa",
    )(qT, kn3, vT, ik3, iqT, iwT, proj, band, qn2)


def _gelu(x):
    c = math.sqrt(2.0 / math.pi)
    half = 0.5 * x
    return half + half * jnp.tanh(x * (c + (c * 0.044715) * (x * x)))


def _sgu_kernel(u_ref, v_ref, g_ref, lng_ref, lnb_ref, ws_ref, bsg_ref, o_ref, wsm_ref, bs_ref):
    @pl.when(pl.program_id(0) == 0)
    def _():
        tpos = lax.broadcasted_iota(i32, (SG_CHUNK, SG_CHUNK), 0) // CHUNK
        spos = lax.broadcasted_iota(i32, (SG_CHUNK, SG_CHUNK), 1) // CHUNK
        for g in range(SG_GROUPS):
            wsm_ref[g] = jnp.where(tpos >= spos, ws_ref[g], 0.0).astype(bf16)
        bs_ref[...] = bsg_ref[...].T

    v = _gelu(v_ref[...].astype(f32))
    mu = jnp.mean(v, axis=-1, keepdims=True)
    vc = v - mu
    var = jnp.mean(vc * vc, axis=-1, keepdims=True)
    vn = (vc * lax.rsqrt(var + EPS) * lng_ref[...] + lnb_ref[...]).astype(bf16)
    for g in range(SG_GROUPS):
        sl = slice(g * SG_GROUP, (g + 1) * SG_GROUP)
        mixed = jnp.dot(wsm_ref[g], vn[:, sl], preferred_element_type=f32) + bs_ref[:, g:g + 1]
        gate = g_ref[:, sl].astype(f32)
        o_ref[:, sl] = (_gelu(u_ref[:, sl].astype(f32)) * mixed * _silu(gate)).astype(o_ref.dtype)


def _sgu(proj, ln_g, ln_b, w_s, b_s, lj):
    m = proj.shape[0]
    no = ln_g.shape[0]
    return pl.pallas_call(
        _sgu_kernel,
        grid=(m // SG_CHUNK,),
        in_specs=[
            pl.BlockSpec((SG_CHUNK, SG_WIDTH), lambda i: (i, 0)),
            pl.BlockSpec((SG_CHUNK, SG_WIDTH), lambda i: (i, 1)),
            pl.BlockSpec((SG_CHUNK, SG_WIDTH), lambda i: (i, 2)),
            pl.BlockSpec((None, 1, SG_WIDTH), lambda i: (lj, 0, 0)),
            pl.BlockSpec((None, 1, SG_WIDTH), lambda i: (lj, 0, 0)),
            pl.BlockSpec((None, SG_GROUPS, SG_CHUNK, SG_CHUNK), lambda i: (lj, 0, 0, 0)),
            pl.BlockSpec((None, SG_GROUPS, SG_CHUNK), lambda i: (lj, 0, 0)),
        ],
        out_specs=pl.BlockSpec((SG_CHUNK, SG_WIDTH), lambda i: (i, 0)),
        out_shape=jax.ShapeDtypeStruct((m, SG_WIDTH), bf16),
        scratch_shapes=[pltpu.VMEM((SG_GROUPS, SG_CHUNK, SG_CHUNK), bf16), pltpu.VMEM((SG_CHUNK, SG_GROUPS), f32)],
        compiler_params=_cparams(("arbitrary",)),
        name="sgu",
    )(proj, proj, proj, ln_g.reshape(no, 1, SG_WIDTH), ln_b.reshape(no, 1, SG_WIDTH), w_s, b_s)


PROJ_TN = 1024
EVEN_GATE_BLOCKS = (0, 1, 2, 3, 7, 8)
EVEN_ATT_BLOCKS = (4, 5, 6, 9)
ODD_COL_BLOCKS = tuple(range(3 * SG_WIDTH // PROJ_TN))
TAIL_START = 10240


def kernel(x, p, norm_gain, rel_bias, even_w_in, conv_w, conv_b, lru_w_r, lru_b_r, lru_w_i, lru_b_i,
           lru_lambda, q_norm, k_norm, even_w_out, odd_w_in, sg_ln_g, sg_ln_b, sg_w_s, sg_b_s, odd_w_out,
           pe_w, pe_gate_norm, pe_w_gate):
    batch, seq, d = x.shape
    depth = p.shape[0]
    topk = min(TOPK_MAX, seq // 4)
    m = batch * seq
    h = x.reshape(m, d)
    band = _band(rel_bias)
    hn = _norm(h, norm_gain, 0)
    even_w_in_t = jnp.swapaxes(even_w_in, 1, 2)
    wg_b, pew_b = _cast_bf16(pe_w_gate), _cast_bf16(pe_w)
    for layer in range(depth):
        j = layer // 2
        if layer % 2 == 0:
            proj = _proj(hn, even_w_in_t, j, EVEN_GATE_BLOCKS, transposed=True, tn=PROJ_TN)
            proj_att = _proj(hn, even_w_in_t, j, EVEN_ATT_BLOCKS, transposed=True, tn=PROJ_TN, out_dtype=bf16)
            tail = _proj_tail(hn, even_w_in_t, j, TAIL_START)
            ya = _lru(proj, conv_w, conv_b, lru_w_r, lru_b_r, lru_w_i, lru_b_i, lru_lambda, j,
                      batch=batch, seq=seq)
            qT, kn, vT, iqT, ik, iwT, qn2 = _prep(proj_att, tail, q_norm, k_norm, j, batch=batch, seq=seq)
            yb = _dsa(qT, kn, vT, ik, iqT, iwT, proj, band, qn2, batch=batch, seq=seq, topk=topk)
            h = _outproj(h, [ya, yb], even_w_out, j)
        else:
            proj = _proj(hn, odd_w_in, j, ODD_COL_BLOCKS, tn=PROJ_TN, out_dtype=bf16)
            h = _outproj(h, [_sgu(proj, sg_ln_g, sg_ln_b, sg_w_s, sg_b_s, j)], odd_w_out, j, tn=1024)
        nxt = (norm_gain, layer + 1) if layer + 1 < depth else (None, None)
        h, hn = _pe(h, pe_gate_norm, wg_b, p, pew_b, layer, *nxt)
    return h.reshape(batch, seq, d)
```

```python
import functools
import math

import jax
import jax.numpy as jnp
from jax import lax
from jax.experimental import pallas as pl
from jax.experimental.pallas import tpu as pltpu

f32 = jnp.float32
bf16 = jnp.bfloat16
i32 = jnp.int32

D_MODEL = 2048
CHUNK = 64
PE_DIM = 256
EPS = 1e-6
RNN_WIDTH = 2048
RNN_BLOCK = 128
CONV_WIDTH = 4
LRU_C = 8.0
ATT_HEADS = 16
HEAD_DIM = 128
KV_HEADS = 4
REP = ATT_HEADS // KV_HEADS
ATT_WIDTH = 2048
KV_WIDTH = 512
IDX_HEADS = 16
IDX_DIM = 64
TOPK_MAX = 256
N_BUCKETS = 32
SG_CHUNK = 128
SG_GROUPS = 16
SG_GROUP = 128
SG_WIDTH = 2048

LANES = 128
SUBLANES = 8
QB = 128
KT = 128
KB = 512
TILES_PER_BLOCK = KB // KT
QK_PIECES = 4
RADIX_FIXED_BITS = 26
RADIX_BITS_PER_CHECK = 2
SUM_ROWS = 16
LOG2E = math.log2(math.e)
ATT_C2 = HEAD_DIM ** -0.5 * LOG2E
VMEM_LIMIT = 52 * 1024 * 1024
NEG_BIG = -1e30
FAST_LOGIT_LIMIT = 60.0

OFF_XA, OFF_GA, OFF_GB = 0, 2048, 4096
OFF_Q, OFF_K, OFF_V, OFF_IQ = 0, 2048, 2560, 3072


def _cparams(sem):
    return pltpu.CompilerParams(dimension_semantics=sem, vmem_limit_bytes=VMEM_LIMIT)


def _sigmoid(x):
    return 0.5 * jnp.tanh(0.5 * x) + 0.5


def _silu(x):
    return x * _sigmoid(x)


def _block_lookup(table):
    def f(j):
        out = jnp.int32(table[-1])
        for idx in range(len(table) - 2, -1, -1):
            out = jnp.where(j == idx, table[idx], out)
        return out
    return f


def _rms(x, gain):
    ms = jnp.mean(x * x, axis=-1, keepdims=True)
    return x * lax.rsqrt(ms + EPS) * gain


def _norm_kernel(h_ref, g_ref, o_ref):
    o_ref[...] = _rms(h_ref[...], g_ref[...]).astype(o_ref.dtype)


def _norm(h, gains, layer, *, tm=512):
    m, d = h.shape
    return pl.pallas_call(
        _norm_kernel,
        grid=(m // tm,),
        in_specs=[pl.BlockSpec((tm, d), lambda i: (i, 0)),
                  pl.BlockSpec((None, 1, d), lambda i: (layer, 0, 0))],
        out_specs=pl.BlockSpec((tm, d), lambda i: (i, 0)),
        out_shape=jax.ShapeDtypeStruct((m, d), bf16),
        compiler_params=_cparams(("parallel",)),
        name="norm",
    )(h, gains.reshape(gains.shape[0], 1, d))


_NT = (((1,), (1,)), ((), ()))


def _proj_kernel(hn_ref, w_ref, o_ref, wb_ref, *, transposed):
    @pl.when(pl.program_id(1) == 0)
    def _():
        wb_ref[...] = w_ref[...].astype(bf16)

    if transposed:
        acc = lax.dot_general(hn_ref[...], wb_ref[...], _NT, preferred_element_type=f32)
    else:
        acc = jnp.dot(hn_ref[...], wb_ref[...], preferred_element_type=f32)
    o_ref[...] = acc.astype(o_ref.dtype)


def _proj(hn, ws, widx, col_blocks, *, transposed=False, tm=1024, tn=1024, out_dtype=f32):
    m, d = hn.shape
    n = len(col_blocks) * tn
    colmap = _block_lookup(col_blocks)
    if transposed:
        w_spec = pl.BlockSpec((None, tn, d), lambda j, i: (widx, colmap(j), 0))
        wb_shape = (tn, d)
    else:
        w_spec = pl.BlockSpec((None, d, tn), lambda j, i: (widx, 0, colmap(j)))
        wb_shape = (d, tn)
    return pl.pallas_call(
        functools.partial(_proj_kernel, transposed=transposed),
        grid=(n // tn, m // tm),
        in_specs=[pl.BlockSpec((tm, d), lambda j, i: (i, 0)), w_spec],
        out_specs=pl.BlockSpec((tm, tn), lambda j, i: (i, j)),
        out_shape=jax.ShapeDtypeStruct((m, n), out_dtype),
        scratch_shapes=[pltpu.VMEM(wb_shape, bf16)],
        compiler_params=_cparams(("parallel", "arbitrary")),
        name="proj",
    )(hn, ws)


def _proj_tail_kernel(hn_ref, wt_ref, o_ref, *, tail_rows):
    row = lax.broadcasted_iota(i32, wt_ref.shape, 0)
    wt = jnp.where(row < tail_rows, wt_ref[...], 0.0).astype(bf16)
    o_ref[...] = lax.dot_general(hn_ref[...], wt, _NT, preferred_element_type=f32)


def _proj_tail(hn, wts, widx, tail_start, *, tm=1024):
    m, d = hn.shape
    tail_rows = wts.shape[1] - tail_start
    assert 0 < tail_rows < LANES and tail_start % LANES == 0
    return pl.pallas_call(
        functools.partial(_proj_tail_kernel, tail_rows=tail_rows),
        grid=(m // tm,),
        in_specs=[
            pl.BlockSpec((tm, d), lambda i: (i, 0)),
            pl.BlockSpec((None, LANES, d), lambda i: (widx, tail_start // LANES, 0)),
        ],
        out_specs=pl.BlockSpec((tm, LANES), lambda i: (i, 0)),
        out_shape=jax.ShapeDtypeStruct((m, LANES), f32),
        compiler_params=_cparams(("parallel",)),
        name="proj_tail",
    )(hn, wts)


def _cast_kernel(w_ref, o_ref):
    o_ref[...] = w_ref[...].astype(o_ref.dtype)


def _cast_bf16(w, *, tr=512):
    lead, rows, cols = w.shape
    tr = min(tr, rows)
    return pl.pallas_call(
        _cast_kernel,
        grid=(lead, rows // tr),
        in_specs=[pl.BlockSpec((None, tr, cols), lambda a, r: (a, r, 0))],
        out_specs=pl.BlockSpec((None, tr, cols), lambda a, r: (a, r, 0)),
        out_shape=jax.ShapeDtypeStruct(w.shape, bf16),
        compiler_params=_cparams(("parallel", "parallel")),
        name="cast_bf16",
    )(w)


def _outproj_kernel(*refs, n_y):
    h_ref = refs[0]
    y_refs = refs[1:1 + n_y]
    w_refs = refs[1 + n_y:1 + 2 * n_y]
    o_ref = refs[1 + 2 * n_y]
    wb_refs = refs[2 + 2 * n_y:]

    @pl.when(pl.program_id(1) == 0)
    def _():
        for w_ref, wb_ref in zip(w_refs, wb_refs):
            wb_ref[...] = w_ref[...].astype(bf16)

    acc = h_ref[...]
    for y_ref, wb_ref in zip(y_refs, wb_refs):
        acc = acc + jnp.dot(y_ref[...], wb_ref[...], preferred_element_type=f32)
    o_ref[...] = acc


def _outproj(h, ys, ws, widx, *, tm=1024, tn=512):
    m, d = h.shape
    n_y = len(ys)
    kw = ys[0].shape[1]
    in_specs = [pl.BlockSpec((tm, tn), lambda j, i: (i, j))]
    in_specs += [pl.BlockSpec((tm, kw), lambda j, i: (i, 0)) for _ in ys]
    in_specs += [pl.BlockSpec((None, kw, tn), lambda j, i, c=c: (widx, c, j)) for c in range(n_y)]
    return pl.pallas_call(
        functools.partial(_outproj_kernel, n_y=n_y),
        grid=(d // tn, m // tm),
        in_specs=in_specs,
        out_specs=pl.BlockSpec((tm, tn), lambda j, i: (i, j)),
        out_shape=jax.ShapeDtypeStruct((m, d), f32),
        scratch_shapes=[pltpu.VMEM((kw, tn), bf16)] * n_y,
        compiler_params=_cparams(("parallel", "arbitrary")),
        name="outproj",
    )(h, *ys, *([ws] * n_y))


def _pe_kernel(h_ref, gn_ref, wg_ref, p_ref, pew_ref, *rest, tn, emit_next):
    if emit_next:
        gnext_ref, o_ref, hnext_ref, hn_ref, pb_ref, orow_ref = rest
    else:
        o_ref, hn_ref, pb_ref = rest
    j = pl.program_id(1)

    @pl.when(j == 0)
    def _():
        hn_ref[...] = _rms(h_ref[...], gn_ref[...]).astype(bf16)
        pb_ref[...] = p_ref[...].astype(bf16)

    cols = pl.ds(pl.multiple_of(j * tn, tn), tn)
    gate = _sigmoid(jnp.dot(hn_ref[...], wg_ref[...], preferred_element_type=f32))
    e = jnp.dot(pb_ref[...], pew_ref[...], preferred_element_type=f32)
    out = h_ref[:, cols] + gate * e
    o_ref[...] = out

    if emit_next:
        orow_ref[:, cols] = out

        @pl.when(j == pl.num_programs(1) - 1)
        def _():
            hnext_ref[...] = _rms(orow_ref[...], gnext_ref[...]).astype(bf16)


def _pe(h, gns, wgs, p, pews, layer, next_gains=None, next_layer=None, *, tm=1024, tn=512):
    m, d = h.shape
    depth = p.shape[0]
    emit_next = next_gains is not None
    in_specs = [
        pl.BlockSpec((tm, d), lambda i, j: (i, 0)),
        pl.BlockSpec((None, 1, d), lambda i, j: (layer, 0, 0)),
        pl.BlockSpec((None, d, tn), lambda i, j: (layer, 0, j)),
        pl.BlockSpec((None, tm, PE_DIM), lambda i, j: (layer, i, 0)),
        pl.BlockSpec((None, PE_DIM, tn), lambda i, j: (layer, 0, j)),
    ]
    args = [h, gns.reshape(depth, 1, d), wgs, p.reshape(depth, m, PE_DIM), pews]
    out_specs = [pl.BlockSpec((tm, tn), lambda i, j: (i, j))]
    out_shape = [jax.ShapeDtypeStruct((m, d), f32)]
    scratch = [pltpu.VMEM((tm, d), bf16), pltpu.VMEM((tm, PE_DIM), bf16)]
    if emit_next:
        in_specs.append(pl.BlockSpec((None, 1, d), lambda i, j: (next_layer, 0, 0)))
        args.append(next_gains.reshape(next_gains.shape[0], 1, d))
        out_specs.append(pl.BlockSpec((tm, d), lambda i, j: (i, 0)))
        out_shape.append(jax.ShapeDtypeStruct((m, d), bf16))
        scratch.append(pltpu.VMEM((tm, d), f32))
    res = pl.pallas_call(
        functools.partial(_pe_kernel, tn=tn, emit_next=emit_next),
        grid=(m // tm, d // tn),
        in_specs=in_specs,
        out_specs=out_specs,
        out_shape=out_shape,
        scratch_shapes=scratch,
        compiler_params=_cparams(("parallel", "arbitrary")),
        name="pe",
    )(*args)
    return res if emit_next else (res[0], None)


LRU_TT = 1024
LRU_CB = 512


def _lru_kernel(x_ref, ga_ref, cw_ref, cb_ref, wr_ref, br_ref, wi_ref, bi_ref, lam_ref, o_ref,
                xext_ref, hcar_ref, a_ref, b_ref):
    tstep = pl.program_id(2)
    pad = SUBLANES

    @pl.when(tstep == 0)
    def _():
        xext_ref[0:pad, :] = jnp.zeros((pad, LRU_CB), f32)
        hcar_ref[...] = jnp.zeros_like(hcar_ref)

    @pl.when(tstep > 0)
    def _():
        xext_ref[0:pad, :] = xext_ref[LRU_TT:LRU_TT + pad, :]

    xext_ref[pad:, :] = x_ref[...]

    xc = cb_ref[...] + cw_ref[CONV_WIDTH - 1:CONV_WIDTH, :] * x_ref[...]
    for j in range(CONV_WIDTH - 1):
        sh = CONV_WIDTH - 1 - j
        xc = xc + cw_ref[j:j + 1, :] * xext_ref[pad - sh:pad - sh + LRU_TT, :]

    xcb = xc.astype(bf16)
    sp = jax.nn.softplus(-lam_ref[...])
    nb = LRU_CB // RNN_BLOCK
    for g in range(nb):
        sl = slice(g * RNN_BLOCK, (g + 1) * RNN_BLOCK)
        xg = xcb[:, sl]
        r = _sigmoid(jnp.dot(xg, wr_ref[g].astype(bf16), preferred_element_type=f32) + br_ref[:, sl])
        ig = _sigmoid(jnp.dot(xg, wi_ref[g].astype(bf16), preferred_element_type=f32) + bi_ref[:, sl])
        log_a = (-LRU_C) * r * sp[:, sl]
        a = jnp.exp(log_a)
        z = jnp.tanh(-log_a) * (1.0 + a * a)
        mult = jnp.where(z > 0.0, z * lax.rsqrt(z), 0.0)
        a_ref[:, sl] = a
        b_ref[:, sl] = mult * ig * xc[:, sl]

    nchunk = LRU_TT // SUBLANES
    a3 = a_ref[...].reshape(nchunk, SUBLANES, LRU_CB)
    b3 = b_ref[...].reshape(nchunk, SUBLANES, LRU_CB)
    row = lax.broadcasted_iota(i32, a3.shape, 1)
    d = 1
    while d < SUBLANES:
        keep = row >= d
        a_sh = jnp.where(keep, pltpu.roll(a3, d, axis=1), 1.0)
        b_sh = jnp.where(keep, pltpu.roll(b3, d, axis=1), 0.0)
        b3 = a3 * b_sh + b3
        a3 = a3 * a_sh
        d *= 2
    a_ref[...] = a3.reshape(LRU_TT, LRU_CB)
    b_ref[...] = b3.reshape(LRU_TT, LRU_CB)

    def carry(c, hprev):
        r0 = pl.multiple_of(c * SUBLANES, SUBLANES)
        h8 = a_ref[pl.ds(r0, SUBLANES), :] * hprev + b_ref[pl.ds(r0, SUBLANES), :]
        gate = ga_ref[pl.ds(r0, SUBLANES), :]
        o_ref[pl.ds(r0, SUBLANES), :] = (h8 * _silu(gate)).astype(o_ref.dtype)
        return jnp.broadcast_to(h8[SUBLANES - 1:SUBLANES, :], (SUBLANES, LRU_CB))

    hcar_ref[...] = lax.fori_loop(0, nchunk, carry, hcar_ref[...], unroll=8)


def _lru(proj, cw, cb, wr, br, wi, bi, lam, lj, *, batch, seq):
    m = proj.shape[0]
    ne = cw.shape[0]
    nt = seq // LRU_TT
    ncb = RNN_WIDTH // LRU_CB
    gpb = LRU_CB // RNN_BLOCK
    ga_off = OFF_GA // LRU_CB
    row = lambda b, c, t: b * nt + t
    vec = pl.BlockSpec((None, 1, LRU_CB), lambda b, c, t: (lj, 0, c))
    gate_w = pl.BlockSpec((None, gpb, RNN_BLOCK, RNN_BLOCK), lambda b, c, t: (lj, c, 0, 0))
    as_vec = lambda a: a.reshape(ne, 1, RNN_WIDTH)
    return pl.pallas_call(
        _lru_kernel,
        grid=(batch, ncb, nt),
        in_specs=[
            pl.BlockSpec((LRU_TT, LRU_CB), lambda b, c, t: (row(b, c, t), c)),
            pl.BlockSpec((LRU_TT, LRU_CB), lambda b, c, t: (row(b, c, t), ga_off + c)),
            pl.BlockSpec((None, CONV_WIDTH, LRU_CB), lambda b, c, t: (lj, 0, c)),
            vec, gate_w, vec, gate_w, vec, vec,
        ],
        out_specs=pl.BlockSpec((LRU_TT, LRU_CB), lambda b, c, t: (row(b, c, t), c)),
        out_shape=jax.ShapeDtypeStruct((m, RNN_WIDTH), bf16),
        scratch_shapes=[
            pltpu.VMEM((LRU_TT + SUBLANES, LRU_CB), f32),
            pltpu.VMEM((SUBLANES, LRU_CB), f32),
            pltpu.VMEM((LRU_TT, LRU_CB), f32),
            pltpu.VMEM((LRU_TT, LRU_CB), f32),
        ],
        compiler_params=_cparams(("parallel", "parallel", "arbitrary")),
        name="lru",
    )(proj, proj, cw, as_vec(cb), wr, as_vec(br), wi, as_vec(bi), as_vec(lam))


PREP_T = 512


def _head_rms(x, gain):
    ms = jnp.mean(x * x, axis=-1, keepdims=True)
    return x * lax.rsqrt(ms + EPS) * gain


def _prep_kernel(q_ref, k_ref, v_ref, iq_ref, tail_ref, qg_ref, kg_ref,
                 qT_ref, kn_ref, vT_ref, iqT_ref, ik_ref, iwT_ref, qn2_ref):
    for h in range(ATT_HEADS):
        sl = slice(h * HEAD_DIM, (h + 1) * HEAD_DIM)
        qt = (_head_rms(q_ref[:, sl].astype(f32), qg_ref[...]) * ATT_C2).T.astype(bf16)
        qT_ref[0, sl, :] = qt
        qf = qt.astype(f32)
        qn2_ref[0, h:h + 1, :] = jnp.sum(qf * qf, axis=0, keepdims=True)
    for g in range(KV_HEADS):
        sl = slice(g * HEAD_DIM, (g + 1) * HEAD_DIM)
        kn_ref[:, sl] = _head_rms(k_ref[:, sl].astype(f32), kg_ref[...]).astype(bf16)
        vT_ref[0, sl, :] = v_ref[:, sl].T
    for c in range(IDX_HEADS * IDX_DIM // LANES):
        sl = slice(c * LANES, (c + 1) * LANES)
        iqT_ref[0, sl, :] = iq_ref[:, sl].T
    tail_t = tail_ref[...].T
    ik_ref[...] = tail_ref[:, 0:IDX_DIM].astype(bf16)
    iwT_ref[0] = tail_t[IDX_DIM:IDX_DIM + IDX_HEADS, :]


def _prep(proj, tail, q_gains, k_gains, lj, *, batch, seq):
    m = proj.shape[0]
    ne = q_gains.shape[0]
    nt = seq // PREP_T
    bt = lambda i: (i // nt, 0, i % nt)
    return pl.pallas_call(
        _prep_kernel,
        grid=(m // PREP_T,),
        in_specs=[
            pl.BlockSpec((PREP_T, ATT_WIDTH), lambda i: (i, OFF_Q // ATT_WIDTH)),
            pl.BlockSpec((PREP_T, KV_WIDTH), lambda i: (i, OFF_K // KV_WIDTH)),
            pl.BlockSpec((PREP_T, KV_WIDTH), lambda i: (i, OFF_V // KV_WIDTH)),
            pl.BlockSpec((PREP_T, IDX_HEADS * IDX_DIM), lambda i: (i, OFF_IQ // (IDX_HEADS * IDX_DIM))),
            pl.BlockSpec((PREP_T, LANES), lambda i: (i, 0)),
            pl.BlockSpec((None, 1, HEAD_DIM), lambda i: (lj, 0, 0)),
            pl.BlockSpec((None, 1, HEAD_DIM), lambda i: (lj, 0, 0)),
        ],
        out_specs=[
            pl.BlockSpec((1, ATT_WIDTH, PREP_T), bt),
            pl.BlockSpec((PREP_T, KV_WIDTH), lambda i: (i, 0)),
            pl.BlockSpec((1, KV_WIDTH, PREP_T), bt),
            pl.BlockSpec((1, IDX_HEADS * IDX_DIM, PREP_T), bt),
            pl.BlockSpec((PREP_T, IDX_DIM), lambda i: (i, 0)),
            pl.BlockSpec((1, IDX_HEADS, PREP_T), bt),
            pl.BlockSpec((1, ATT_HEADS, PREP_T), bt),
        ],
        out_shape=[
            jax.ShapeDtypeStruct((batch, ATT_WIDTH, seq), bf16),
            jax.ShapeDtypeStruct((m, KV_WIDTH), bf16),
            jax.ShapeDtypeStruct((batch, KV_WIDTH, seq), bf16),
            jax.ShapeDtypeStruct((batch, IDX_HEADS * IDX_DIM, seq), bf16),
            jax.ShapeDtypeStruct((m, IDX_DIM), bf16),
            jax.ShapeDtypeStruct((batch, IDX_HEADS, seq), f32),
            jax.ShapeDtypeStruct((batch, ATT_HEADS, seq), f32),
        ],
        compiler_params=_cparams(("parallel",)),
        name="attn_prep",
    )(proj, proj, proj, proj, tail, q_gains.reshape(ne, 1, HEAD_DIM), k_gains.reshape(ne, 1, HEAD_DIM))


BAND_ROWS = 3 * KT
FAR_BUCKET = N_BUCKETS // 2 - 1


def _band_kernel(tab_ref, o_ref):
    jj = lax.broadcasted_iota(i32, (BAND_ROWS, QB), 0)
    qi = lax.broadcasted_iota(i32, (BAND_ROWS, QB), 1)
    rel = jnp.where(jj < KT, -2 * KT, jj - 2 * KT - qi)
    n = jnp.abs(rel)
    large = jnp.full(rel.shape, 8, i32)
    for thr in (12, 16, 23, 32, 46, 64, 91):
        large = large + (n >= thr).astype(i32)
    bucket = jnp.where(rel > 0, N_BUCKETS // 2, 0) + jnp.where(n < 8, n, large)

    def per_head(h, carry):
        acc = jnp.zeros((BAND_ROWS, QB), f32)
        for b in range(N_BUCKETS):
            acc = jnp.where(bucket == b, tab_ref[b, h], acc)
        o_ref[h] = (acc - tab_ref[FAR_BUCKET, h]) * LOG2E
        return carry

    lax.fori_loop(0, ATT_HEADS, per_head, 0)


def _band(rel_bias):
    return pl.pallas_call(
        _band_kernel,
        in_specs=[pl.BlockSpec(memory_space=pltpu.SMEM)],
        out_specs=pl.BlockSpec(memory_space=pltpu.VMEM),
        out_shape=jax.ShapeDtypeStruct((ATT_HEADS, BAND_ROWS, QB), f32),
        name="bias_band",
    )(rel_bias)


def _dsa_kernel(qT_ref, k_ref, vT_ref, ik_ref, iqT_ref, iwT_ref, gb_ref, band_ref, qn2_ref, o_ref,
                key_ref, madd_ref, eye_ref, shift_ref, *state_refs, topk):
    m_refs = state_refs[:KV_HEADS]
    acc_refs = state_refs[KV_HEADS:]
    i = pl.program_id(1)
    nb = i // TILES_PER_BLOCK + 1
    t0 = i * QB
    qpos = t0 + lax.broadcasted_iota(i32, (1, QB), 1)
    limit = (qpos // CHUNK + 1) * CHUNK
    sub_iota = lax.broadcasted_iota(i32, (KB, QB), 0)
    idx_scale = (IDX_DIM ** -0.5) * (IDX_HEADS ** -0.5)
    int_min = jnp.int32(-2 ** 31)

    def score_block(j, carry):
        r0 = pl.multiple_of(j * KB, KB)
        ks = ik_ref[0, pl.ds(r0, KB), :]
        acc = jnp.zeros((KB, QB), f32)
        for hp in range(IDX_HEADS // 2):
            h0, h1 = 2 * hp, 2 * hp + 1
            w = jnp.concatenate([iqT_ref[0, h0 * IDX_DIM:(h0 + 1) * IDX_DIM, :],
                                 iqT_ref[0, h1 * IDX_DIM:(h1 + 1) * IDX_DIM, :]], axis=1)
            s = jnp.dot(ks, w, preferred_element_type=f32)
            acc = acc + jnp.maximum(s[:, :QB], 0.0) * iwT_ref[0, h0:h0 + 1, :]
            acc = acc + jnp.maximum(s[:, QB:], 0.0) * iwT_ref[0, h1:h1 + 1, :]
        score = acc * idx_scale
        score = jnp.where(score == 0.0, 0.0, score)
        score = jnp.where(r0 + sub_iota < limit, score, -jnp.inf)
        bits = pltpu.bitcast(score, i32)
        key_ref[pl.ds(r0, KB), :] = jnp.where(bits >= 0, bits, bits ^ jnp.int32(0x7FFFFFFF))
        return carry

    lax.fori_loop(0, nb // 2, lambda jj, c: score_block(2 * jj + 1, score_block(2 * jj, c)), 0)
    lax.cond(nb % 2 == 1, lambda: score_block(nb - 1, 0), lambda: 0)

    def count_rows(pred_fn):
        def one(j, cnt8):
            r0 = pl.multiple_of(j * KB, KB)
            hit = pred_fn(key_ref[pl.ds(r0, KB), :], r0).astype(i32)
            return cnt8 + jnp.sum(hit.reshape(KB // SUBLANES, SUBLANES, QB), axis=0)

        cnt8 = lax.fori_loop(0, nb // 2, lambda jj, c: one(2 * jj + 1, one(2 * jj, c)),
                             jnp.zeros((SUBLANES, QB), i32))
        cnt8 = lax.cond(nb % 2 == 1, lambda c: one(nb - 1, c), lambda c: c, cnt8)
        return jnp.sum(cnt8, axis=0, keepdims=True)

    n_nonneg = count_rows(lambda kt, r0: kt >= 0)
    nonneg = n_nonneg >= topk
    prefix0 = jnp.where(nonneg, jnp.int32(0), int_min)
    cur0 = jnp.where(nonneg, n_nonneg, nb * KB)
    n_bits = 31

    def bit_pass(b, state):
        prefix, cur = state
        bit = jnp.where(b < n_bits, lax.shift_left(jnp.int32(1), jnp.maximum(n_bits - 1 - b, 0)), 0)
        cand = prefix | bit
        cnt = count_rows(lambda kt, r0: kt >= cand)
        take = cnt >= topk
        return jnp.where(take, cand, prefix), jnp.where(take, cnt, cur)

    state1 = lax.fori_loop(0, RADIX_FIXED_BITS, bit_pass, (prefix0, cur0))

    def bits_cond(state):
        b, _, _, more = state
        return jnp.logical_and(b < n_bits, more > 0)

    def bits_body(state):
        b, prefix, cur, _ = state
        for u in range(RADIX_BITS_PER_CHECK):
            prefix, cur = bit_pass(b + u, (prefix, cur))
        return b + RADIX_BITS_PER_CHECK, prefix, cur, jnp.max((cur > topk).astype(i32))

    more1 = jnp.max((state1[1] > topk).astype(i32))
    _, tau, cur, _ = lax.while_loop(bits_cond, bits_body, (jnp.int32(RADIX_FIXED_BITS), *state1, more1))

    neg_inf_key = jnp.int32(-2 ** 31 + 0x7FFFFF)
    excess = jnp.logical_and(cur > topk, tau > neg_inf_key)
    any_excess = jnp.max(excess.astype(i32)) > 0
    n_gt = lax.cond(any_excess, lambda: count_rows(lambda kt, r0: kt > tau), lambda: jnp.zeros((1, QB), i32))
    room = topk - n_gt
    n_iter = jnp.where(any_excess, 13, 0)

    def cut_step(b, cut):
        cand = cut | lax.shift_left(jnp.int32(1), 12 - b)
        cnt = count_rows(lambda kt, r0: jnp.logical_and(kt == tau, r0 + sub_iota < cand))
        return jnp.where(cnt <= room, cand, cut)

    cut = lax.fori_loop(0, n_iter, cut_step, jnp.zeros((1, QB), i32))
    cut = jnp.where(excess, cut, jnp.int32(2 ** 30))

    def mask_block(j, carry):
        r0 = pl.multiple_of(j * KB, KB)
        kt = key_ref[pl.ds(r0, KB), :]
        pos = r0 + sub_iota
        sel = jnp.logical_or(kt > tau, jnp.logical_and(kt == tau, pos < cut))
        sel = jnp.logical_and(sel, pos < limit)
        madd_ref[pl.ds(r0, KB), :] = jnp.where(sel, 0.0, NEG_BIG).astype(bf16)
        return carry

    lax.fori_loop(0, nb, mask_block, 0)

    @pl.when(i == 0)
    def _():
        rowi = lax.broadcasted_iota(i32, (QB, REP * QB), 0)
        coli = lax.broadcasted_iota(i32, (QB, REP * QB), 1)
        eye_ref[...] = ((coli & (QB - 1)) == rowi).astype(bf16)
        bmax = jnp.max(jnp.abs(band_ref[...]).reshape(ATT_HEADS * BAND_ROWS, QB), axis=0, keepdims=True)
        room = FAST_LOGIT_LIMIT - jnp.max(bmax, axis=1, keepdims=True)
        room2 = jnp.where(room > 0.0, room * room, -1.0)
        over = jnp.zeros((1, qn2_ref.shape[2]), i32)
        for g in range(KV_HEADS):
            def kmax_body(j, mx, g=g):
                r0 = pl.multiple_of(j * KB, KB)
                kf = k_ref[0, pl.ds(r0, KB), g * HEAD_DIM:(g + 1) * HEAD_DIM].astype(f32)
                n2 = jnp.sum(kf * kf, axis=-1, keepdims=True)
                return jnp.maximum(mx, jnp.max(n2, axis=0, keepdims=True))
            k2 = lax.fori_loop(0, k_ref.shape[1] // KB, kmax_body, jnp.zeros((1, 1), f32))
            q2 = jnp.max(qn2_ref[0, g * REP:(g + 1) * REP, :], axis=0, keepdims=True)
            over = over | (q2 * k2 * 1.002 > room2).astype(i32)
        shift_ref[0] = jnp.max(over)

    needs_shift = shift_ref[0] > 0

    for m_ref, acc_ref in zip(m_refs, acc_refs):
        m_ref[...] = jnp.full(m_ref.shape, NEG_BIG, f32)
        acc_ref[...] = jnp.zeros(acc_ref.shape, f32)

    ones_rows = jnp.ones((SUM_ROWS, KB), bf16)

    def att_block(j, carry, near, online):
        r0 = pl.multiple_of(j * KB, KB)

        def qk(g):
            qg = jnp.concatenate([qT_ref[0, h * HEAD_DIM:(h + 1) * HEAD_DIM, :]
                                  for h in range(g * REP, (g + 1) * REP)], axis=1)
            rhs = jnp.concatenate([qg, eye_ref[...]], axis=0)
            piece = KB // QK_PIECES
            parts = []
            for c in range(QK_PIECES):
                rows = pl.ds(r0 + c * piece, piece)
                lhs = jnp.concatenate([k_ref[0, rows, g * HEAD_DIM:(g + 1) * HEAD_DIM], madd_ref[rows, :]],
                                      axis=1)
                parts.append(jnp.dot(lhs, rhs, preferred_element_type=f32))
            return jnp.concatenate(parts, axis=0)

        ts = {0: qk(0)}
        for g in range(KV_HEADS):
            heads = range(g * REP, (g + 1) * REP)
            if g + 1 < KV_HEADS:
                ts[g + 1] = qk(g + 1)
            t = ts.pop(g)
            if near:
                rows = []
                for u in range(TILES_PER_BLOCK):
                    band_row = pl.multiple_of(jnp.clip(j * TILES_PER_BLOCK + u - i + 2, 0, 2) * KT, KT)
                    rows.append(jnp.concatenate([band_ref[h, pl.ds(band_row, KT), :] for h in heads], axis=1))
                t = t + jnp.concatenate(rows, axis=0)
            vt = jnp.concatenate([vT_ref[0, g * HEAD_DIM:(g + 1) * HEAD_DIM, pl.ds(r0, KB)], ones_rows], axis=0)
            if online:
                m_old = m_refs[g][...]
                m_new = jnp.maximum(m_old, jnp.max(t, axis=0, keepdims=True))
                alpha = jnp.exp2(m_old - m_new)
                p = jnp.exp2(t - m_new)
                m_refs[g][...] = m_new
                acc_refs[g][...] = (acc_refs[g][...] * alpha
                                    + jnp.dot(vt, p.astype(bf16), preferred_element_type=f32))
            else:
                p = jnp.exp2(t)
                acc_refs[g][...] = acc_refs[g][...] + jnp.dot(vt, p.astype(bf16), preferred_element_type=f32)
        return carry

    n_far = jnp.maximum(i - 1, 0) // TILES_PER_BLOCK

    def run_blocks(online):
        lax.fori_loop(0, n_far, functools.partial(att_block, near=False, online=online), 0)
        lax.fori_loop(n_far, nb, functools.partial(att_block, near=True, online=online), 0)

    lax.cond(needs_shift, functools.partial(run_blocks, True), functools.partial(run_blocks, False))

    for g in range(KV_HEADS):
        heads = range(g * REP, (g + 1) * REP)
        oT = acc_refs[g][0:HEAD_DIM, :] / acc_refs[g][HEAD_DIM:HEAD_DIM + 1, :]
        for r, h in enumerate(heads):
            gate = gb_ref[:, h * HEAD_DIM:(h + 1) * HEAD_DIM]
            o = oT[:, r * QB:(r + 1) * QB].T
            o_ref[:, h * HEAD_DIM:(h + 1) * HEAD_DIM] = (o * _silu(gate)).astype(o_ref.dtype)


def _dsa(qT, kn, vT, ik, iqT, iwT, proj, band, qn2, *, batch, seq, topk):
    m = proj.shape[0]
    nq = seq // QB
    kn3 = kn.reshape(batch, seq, KV_WIDTH)
    ik3 = ik.reshape(batch, seq, IDX_DIM)
    return pl.pallas_call(
        functools.partial(_dsa_kernel, topk=topk),
        grid=(batch, nq),
        in_specs=[
            pl.BlockSpec((1, ATT_WIDTH, QB), lambda b, i: (b, 0, i)),
            pl.BlockSpec((1, seq, KV_WIDTH), lambda b, i: (b, 0, 0)),
            pl.BlockSpec((1, KV_WIDTH, seq), lambda b, i: (b, 0, 0)),
            pl.BlockSpec((1, seq, IDX_DIM), lambda b, i: (b, 0, 0)),
            pl.BlockSpec((1, IDX_HEADS * IDX_DIM, QB), lambda b, i: (b, 0, i)),
            pl.BlockSpec((1, IDX_HEADS, QB), lambda b, i: (b, 0, i)),
            pl.BlockSpec((QB, ATT_WIDTH), lambda b, i: (b * nq + i, OFF_GB // ATT_WIDTH)),
            pl.BlockSpec((ATT_HEADS, BAND_ROWS, QB), lambda b, i: (0, 0, 0)),
            pl.BlockSpec((1, ATT_HEADS, seq), lambda b, i: (b, 0, 0)),
        ],
        out_specs=pl.BlockSpec((QB, ATT_WIDTH), lambda b, i: (b * nq + i, 0)),
        out_shape=jax.ShapeDtypeStruct((m, ATT_WIDTH), bf16),
        scratch_shapes=[
            pltpu.VMEM((seq, QB), i32),
            pltpu.VMEM((seq, QB), bf16),
            pltpu.VMEM((QB, REP * QB), bf16),
            pltpu.SMEM((1,), i32),
        ] + [pltpu.VMEM((1, REP * QB), f32)] * KV_HEADS
          + [pltpu.VMEM((HEAD_DIM + SUM_ROWS, REP * QB), f32)] * KV_HEADS,
        compiler_params=_cparams(("parallel", "arbitrary")),
        name="dsa",
    )(qT, kn3, vT, ik3, iqT, iwT, proj, band, qn2)


def _gelu(x):
    c = math.sqrt(2.0 / math.pi)
    half = 0.5 * x
    return half + half * jnp.tanh(x * (c + (c * 0.044715) * (x * x)))


def _sgu_kernel(u_ref, v_ref, g_ref, lng_ref, lnb_ref, ws_ref, bsg_ref, o_ref, wsm_ref, bs_ref):
    @pl.when(pl.program_id(0) == 0)
    def _():
        tpos = lax.broadcasted_iota(i32, (SG_CHUNK, SG_CHUNK), 0) // CHUNK
        spos = lax.broadcasted_iota(i32, (SG_CHUNK, SG_CHUNK), 1) // CHUNK
        for g in range(SG_GROUPS):
            wsm_ref[g] = jnp.where(tpos >= spos, ws_ref[g], 0.0).astype(bf16)
        bs_ref[...] = bsg_ref[...].T

    v = _gelu(v_ref[...].astype(f32))
    mu = jnp.mean(v, axis=-1, keepdims=True)
    vc = v - mu
    var = jnp.mean(vc * vc, axis=-1, keepdims=True)
    vn = (vc * lax.rsqrt(var + EPS) * lng_ref[...] + lnb_ref[...]).astype(bf16)
    for g in range(SG_GROUPS):
        sl = slice(g * SG_GROUP, (g + 1) * SG_GROUP)
        mixed = jnp.dot(wsm_ref[g], vn[:, sl], preferred_element_type=f32) + bs_ref[:, g:g + 1]
        gate = g_ref[:, sl].astype(f32)
        o_ref[:, sl] = (_gelu(u_ref[:, sl].astype(f32)) * mixed * _silu(gate)).astype(o_ref.dtype)


def _sgu(proj, ln_g, ln_b, w_s, b_s, lj):
    m = proj.shape[0]
    no = ln_g.shape[0]
    return pl.pallas_call(
        _sgu_kernel,
        grid=(m // SG_CHUNK,),
        in_specs=[
            pl.BlockSpec((SG_CHUNK, SG_WIDTH), lambda i: (i, 0)),
            pl.BlockSpec((SG_CHUNK, SG_WIDTH), lambda i: (i, 1)),
            pl.BlockSpec((SG_CHUNK, SG_WIDTH), lambda i: (i, 2)),
            pl.BlockSpec((None, 1, SG_WIDTH), lambda i: (lj, 0, 0)),
            pl.BlockSpec((None, 1, SG_WIDTH), lambda i: (lj, 0, 0)),
            pl.BlockSpec((None, SG_GROUPS, SG_CHUNK, SG_CHUNK), lambda i: (lj, 0, 0, 0)),
            pl.BlockSpec((None, SG_GROUPS, SG_CHUNK), lambda i: (lj, 0, 0)),
        ],
        out_specs=pl.BlockSpec((SG_CHUNK, SG_WIDTH), lambda i: (i, 0)),
        out_shape=jax.ShapeDtypeStruct((m, SG_WIDTH), bf16),
        scratch_shapes=[pltpu.VMEM((SG_GROUPS, SG_CHUNK, SG_CHUNK), bf16), pltpu.VMEM((SG_CHUNK, SG_GROUPS), f32)],
        compiler_params=_cparams(("arbitrary",)),
        name="sgu",
    )(proj, proj, proj, ln_g.reshape(no, 1, SG_WIDTH), ln_b.reshape(no, 1, SG_WIDTH), w_s, b_s)


PROJ_TN = 1024
EVEN_GATE_BLOCKS = (0, 1, 2, 3, 7, 8)
EVEN_ATT_BLOCKS = (4, 5, 6, 9)
ODD_COL_BLOCKS = tuple(range(3 * SG_WIDTH // PROJ_TN))
TAIL_START = 10240


def kernel(x, p, norm_gain, rel_bias, even_w_in, conv_w, conv_b, lru_w_r, lru_b_r, lru_w_i, lru_b_i,
           lru_lambda, q_norm, k_norm, even_w_out, odd_w_in, sg_ln_g, sg_ln_b, sg_w_s, sg_b_s, odd_w_out,
           pe_w, pe_gate_norm, pe_w_gate):
    batch, seq, d = x.shape
    depth = p.shape[0]
    topk = min(TOPK_MAX, seq // 4)
    m = batch * seq
    h = x.reshape(m, d)
    band = _band(rel_bias)
    hn = _norm(h, norm_gain, 0)
    even_w_in_t = jnp.swapaxes(even_w_in, 1, 2)
    wg_b, pew_b = _cast_bf16(pe_w_gate), _cast_bf16(pe_w)
    for layer in range(depth):
        j = layer // 2
        if layer % 2 == 0:
            proj = _proj(hn, even_w_in_t, j, EVEN_GATE_BLOCKS, transposed=True, tn=PROJ_TN)
            proj_att = _proj(hn, even_w_in_t, j, EVEN_ATT_BLOCKS, transposed=True, tn=PROJ_TN, out_dtype=bf16)
            tail = _proj_tail(hn, even_w_in_t, j, TAIL_START)
            ya = _lru(proj, conv_w, conv_b, lru_w_r, lru_b_r, lru_w_i, lru_b_i, lru_lambda, j,
                      batch=batch, seq=seq)
            qT, kn, vT, iqT, ik, iwT, qn2 = _prep(proj_att, tail, q_norm, k_norm, j, batch=batch, seq=seq)
            yb = _dsa(qT, kn, vT, ik, iqT, iwT, proj, band, qn2, batch=batch, seq=seq, topk=topk)
            h = _outproj(h, [ya, yb], even_w_out, j)
        else:
            proj = _proj(hn, odd_w_in, j, ODD_COL_BLOCKS, tn=PROJ_TN, out_dtype=bf16)
            h = _outproj(h, [_sgu(proj, sg_ln_g, sg_ln_b, sg_w_s, sg_b_s, j)], odd_w_out, j, tn=1024)
        nxt = (norm_gain, layer + 1) if layer + 1 < depth else (None, None)
        h, hn = _pe(h, pe_gate_norm, wg_b, p, pew_b, layer, *nxt)
    return h.reshape(batch, seq, d)
```
